```python
import jax, jax.numpy as jnp
from jax import lax
import numpy as np

D_MODEL = 1024
BATCH = 2
SEQ = 16384
DEPTH = 1
DEC_BATCH = 16
DEC_SEQ = 2048
PAST_LEN = 128

GLA_HEADS = 4
GLA_DK = 64
GLA_DV = 128
GLA_KEY = GLA_HEADS * GLA_DK
GLA_VAL = GLA_HEADS * GLA_DV
GATE_RANK = 16
GATE_NORMALIZER = 16
CHUNK = 64
CONV_CH = 512
CONV_K = 31
MIX_WIDTH = GLA_VAL + CONV_CH
IN_COLS = 2 * GLA_KEY + 2 * GLA_VAL + 2 * GATE_RANK + 2 * CONV_CH
N_GROUPS = 4
EXPERTS_PER_GROUP = 8
N_EXPERTS = N_GROUPS * EXPERTS_PER_GROUP
TOP_K = 2
D_EXPERT = 512
MOE_BLOCK = 256
EPS = 1e-6

kernel_name = 'hybrid_gla_conformer_hmoe_encoder'


def _rms_norm(x, g):
    xf = x.astype(jnp.float32)
    y = xf * lax.rsqrt(jnp.mean(jnp.square(xf), axis=-1, keepdims=True) + EPS)
    return (y * g.astype(jnp.float32)).astype(x.dtype)


def _layer_norm(x, g, b):
    xf = x.astype(jnp.float32)
    mu = jnp.mean(xf, axis=-1, keepdims=True)
    xc = xf - mu
    y = xc * lax.rsqrt(jnp.mean(jnp.square(xc), axis=-1, keepdims=True) + EPS)
    return (y * g.astype(jnp.float32) + b.astype(jnp.float32)).astype(x.dtype)


def _gla_scan(q, k, v, log_a, strict):
    B, L, H, DK = q.shape
    DV = v.shape[-1]
    n = L // CHUNK

    def to_chunks(t):
        return t.reshape(B, n, CHUNK, H, t.shape[-1]).transpose(1, 0, 3, 2, 4)

    qc, kc, vc, ac = to_chunks(q), to_chunks(k), to_chunks(v), to_chunks(log_a)
    mask = jnp.tril(jnp.ones((CHUNK, CHUNK), dtype=bool), -1 if strict else 0)[:, :, None]

    def step(S, inp):
        qb, kb, vb, ab = inp
        b = jnp.cumsum(ab, axis=2)
        inter = jnp.einsum('bhtd,bhdv->bhtv', qb * jnp.exp(b), S)
        diff = b[:, :, :, None, :] - b[:, :, None, :, :]
        decay = jnp.exp(jnp.where(mask, diff, -jnp.inf))
        scores = jnp.einsum('bhtd,bhsd,bhtsd->bhts', qb, kb, decay)
        intra = jnp.einsum('bhts,bhsv->bhtv', scores, vb)
        b_last = b[:, :, -1:, :]
        S_new = jnp.exp(b_last[:, :, 0, :])[..., None] * S + jnp.einsum(
            'bhsd,bhsv->bhdv', kb * jnp.exp(b_last - b), vb)
        return S_new, inter + intra

    S0 = jnp.zeros((B, H, DK, DV), jnp.float32)
    _, out = lax.scan(step, S0, (qc, kc, vc, ac))
    return out.transpose(1, 0, 3, 2, 4).reshape(B, L, H, DV)


def _gla_mixer(q, k, v, og, a_f, a_b, w_af2, b_af, w_ab2, b_ab, g_head):
    B, L, _ = q.shape
    f32 = jnp.float32

    def heads(t, d):
        return t.astype(f32).reshape(B, L, GLA_HEADS, d)

    qh = heads(q, GLA_DK) * (GLA_DK ** -0.5)
    kh = heads(k, GLA_DK)
    vh = heads(v, GLA_DV)
    la_f = heads(jax.nn.log_sigmoid((a_f @ w_af2).astype(f32) + b_af.astype(f32)) / GATE_NORMALIZER, GLA_DK)
    la_b = heads(jax.nn.log_sigmoid((a_b @ w_ab2).astype(f32) + b_ab.astype(f32)) / GATE_NORMALIZER, GLA_DK)
    rev = lambda t: jnp.flip(t, axis=1)
    o = _gla_scan(qh, kh, vh, la_f, strict=False) + rev(
        _gla_scan(rev(qh), rev(kh), rev(vh), rev(la_b), strict=True))
    o = o * lax.rsqrt(jnp.mean(jnp.square(o), axis=-1, keepdims=True) + EPS)
    o = o * g_head.astype(f32).reshape(GLA_HEADS, GLA_DV)
    return o.reshape(B, L, GLA_VAL) * jax.nn.silu(og.astype(f32))


def _conv_mixer(c, w_dw, b_dw, g_ln, b_ln, w_pw2):
    glu = c[..., :CONV_CH] * jax.nn.sigmoid(c[..., CONV_CH:])
    y = lax.conv_general_dilated(
        glu, w_dw.reshape(CONV_K, 1, CONV_CH), window_strides=(1,),
        padding=((CONV_K // 2, CONV_K // 2),),
        dimension_numbers=('NWC', 'WIO', 'NWC'), feature_group_count=CONV_CH)
    y = jax.nn.silu(_layer_norm(y + b_dw, g_ln, b_ln))
    return y @ w_pw2


def _hier_moe(u, w_rg, b_rg, w_re, b_re, w_e_gate, w_e_up, w_e_down):
    B, L, D = u.shape
    f32 = jnp.float32
    xf = u.reshape(-1, D)
    N = xf.shape[0]
    lg = (xf @ w_rg).astype(f32) + b_rg.astype(f32)
    pg = jax.nn.softmax(lg, axis=-1)
    gsel = jnp.argmax(lg, axis=-1).astype(jnp.int32)
    gate_g = jnp.take_along_axis(pg, gsel[:, None], axis=-1)[:, 0]
    le = ((xf @ w_re).astype(f32) + b_re.astype(f32)).reshape(N, N_GROUPS, EXPERTS_PER_GROUP)
    le_sel = jnp.take_along_axis(le, gsel[:, None, None], axis=1)[:, 0]
    pe = jax.nn.softmax(le_sel, axis=-1)
    ptop, itop = lax.top_k(pe, TOP_K)
    wts = gate_g[:, None] * ptop / jnp.sum(ptop, axis=-1, keepdims=True)
    eid = gsel[:, None] * EXPERTS_PER_GROUP + itop.astype(jnp.int32)

    A = N * TOP_K
    nb = A // MOE_BLOCK + N_EXPERTS
    flat_e = eid.reshape(-1)
    flat_w = wts.reshape(-1)
    flat_tok = jnp.repeat(jnp.arange(N, dtype=jnp.int32), TOP_K)
    order = jnp.argsort(flat_e)
    s_e, s_tok, s_w = flat_e[order], flat_tok[order], flat_w[order]
    counts = jnp.bincount(flat_e, length=N_EXPERTS).astype(jnp.int32)
    pcounts = (counts + MOE_BLOCK - 1) // MOE_BLOCK * MOE_BLOCK
    start = jnp.cumsum(counts) - counts
    pend = jnp.cumsum(pcounts)
    pstart = pend - pcounts
    dest = pstart[s_e] + (jnp.arange(A, dtype=jnp.int32) - start[s_e])
    tok_buf = jnp.full((nb * MOE_BLOCK,), N, jnp.int32).at[dest].set(s_tok)
    block_e = jnp.minimum(
        jnp.searchsorted(pend, jnp.arange(nb, dtype=jnp.int32) * MOE_BLOCK, side='right'),
        N_EXPERTS - 1).astype(jnp.int32)
    xpad = jnp.concatenate([xf, jnp.zeros((1, D), xf.dtype)], axis=0)

    def expert_block(args):
        tok, e = args
        xb = xpad[tok]
        hb = jax.nn.silu(xb @ w_e_gate[e]) * (xb @ w_e_up[e])
        return hb @ w_e_down[e]

    ybuf = lax.map(expert_block, (tok_buf.reshape(nb, MOE_BLOCK), block_e)).reshape(-1, D)
    contrib = ybuf[dest].astype(f32) * s_w[:, None]
    out = jnp.zeros((N, D), f32).at[s_tok].add(contrib)
    return out.reshape(B, L, D).astype(u.dtype)


def _encoder_layer(x, g_mix, w_in, w_af2, b_af, w_ab2, b_ab, g_head, w_dw, b_dw, g_ln, b_ln,
                   w_pw2, w_out, g_ffn, w_rg, b_rg, w_re, b_re, w_e_gate, w_e_up, w_e_down):
    u = _rms_norm(x, g_mix)
    proj = u @ w_in
    cuts = np.cumsum([GLA_KEY, GLA_KEY, GLA_VAL, GLA_VAL, GATE_RANK, GATE_RANK]).tolist()
    q, k, v, og, a_f, a_b, c = jnp.split(proj, cuts, axis=-1)
    o_gla = _gla_mixer(q, k, v, og, a_f, a_b, w_af2, b_af, w_ab2, b_ab, g_head).astype(x.dtype)
    o_conv = _conv_mixer(c, w_dw, b_dw, g_ln, b_ln, w_pw2)
    h = x + jnp.concatenate([o_gla, o_conv], axis=-1) @ w_out
    h = h + _hier_moe(_rms_norm(h, g_ffn), w_rg, b_rg, w_re, b_re, w_e_gate, w_e_up, w_e_down)
    return h


def setup_inputs(seed: int = 0) -> dict:
    key = jax.random.key(seed)
    ks = jax.random.split(key, 32)
    nrm = lambda kk, shape, scale: jax.random.normal(kk, shape, jnp.float32) * scale
    Lr = DEPTH
    return {
        'x_prompt': nrm(ks[0], (BATCH, SEQ, D_MODEL), 1.0),
        'x_sample': nrm(ks[1], (DEC_BATCH, DEC_SEQ, D_MODEL), 1.0),
        'g_mix': 1.0 + nrm(ks[2], (Lr, D_MODEL), 0.02),
        'w_in': nrm(ks[3], (Lr, D_MODEL, IN_COLS), D_MODEL ** -0.5),
        'w_af2': nrm(ks[4], (Lr, GATE_RANK, GLA_KEY), GATE_RANK ** -0.5),
        'b_af': nrm(ks[5], (Lr, GLA_KEY), 0.1),
        'w_ab2': nrm(ks[6], (Lr, GATE_RANK, GLA_KEY), GATE_RANK ** -0.5),
        'b_ab': nrm(ks[7], (Lr, GLA_KEY), 0.1),
        'g_head': 1.0 + nrm(ks[8], (Lr, GLA_VAL), 0.02),
        'w_dw': nrm(ks[9], (Lr, CONV_K, CONV_CH), CONV_K ** -0.5),
        'b_dw': nrm(ks[10], (Lr, CONV_CH), 0.02),
        'g_ln': 1.0 + nrm(ks[11], (Lr, CONV_CH), 0.02),
        'b_ln': nrm(ks[12], (Lr, CONV_CH), 0.02),
        'w_pw2': nrm(ks[13], (Lr, CONV_CH, CONV_CH), CONV_CH ** -0.5),
        'w_out': nrm(ks[14], (Lr, MIX_WIDTH, D_MODEL), MIX_WIDTH ** -0.5),
        'g_ffn': 1.0 + nrm(ks[15], (Lr, D_MODEL), 0.02),
        'w_rg': nrm(ks[16], (Lr, D_MODEL, N_GROUPS), D_MODEL ** -0.5),
        'b_rg': nrm(ks[17], (Lr, N_GROUPS), 0.01),
        'w_re': nrm(ks[18], (Lr, D_MODEL, N_EXPERTS), D_MODEL ** -0.5),
        'b_re': nrm(ks[19], (Lr, N_EXPERTS), 0.01),
        'w_e_gate': nrm(ks[20], (Lr, N_EXPERTS, D_MODEL, D_EXPERT), D_MODEL ** -0.5),
        'w_e_up': nrm(ks[21], (Lr, N_EXPERTS, D_MODEL, D_EXPERT), D_MODEL ** -0.5),
        'w_e_down': nrm(ks[22], (Lr, N_EXPERTS, D_EXPERT, D_MODEL), D_EXPERT ** -0.5),
        'g_final': 1.0 + nrm(ks[23], (D_MODEL,), 0.02),
    }


def reference(x_prompt, x_sample, g_mix, w_in, w_af2, b_af, w_ab2, b_ab, g_head, w_dw, b_dw,
              g_ln, b_ln, w_pw2, w_out, g_ffn, w_rg, b_rg, w_re, b_re, w_e_gate, w_e_up,
              w_e_down, g_final):
    def trunk(x):
        for l in range(DEPTH):
            x = _encoder_layer(x, g_mix[l], w_in[l], w_af2[l], b_af[l], w_ab2[l], b_ab[l],
                               g_head[l], w_dw[l], b_dw[l], g_ln[l], b_ln[l], w_pw2[l],
                               w_out[l], g_ffn[l], w_rg[l], b_rg[l], w_re[l], b_re[l],
                               w_e_gate[l], w_e_up[l], w_e_down[l])
        return _rms_norm(x, g_final)

    y_prompt = trunk(x_prompt)
    y_sample = trunk(x_sample)
    return (y_prompt, y_sample)
```

```python
import functools

import jax
import jax.numpy as jnp
from jax import lax
from jax.experimental import pallas as pl
from jax.experimental.pallas import tpu as pltpu

f32 = jnp.float32
bf16 = jnp.bfloat16
i32 = jnp.int32
u32 = jnp.uint32

D_MODEL = 1024
GLA_HEADS = 4
GLA_DK = 64
GLA_DV = 128
GLA_KEY = GLA_HEADS * GLA_DK
GLA_VAL = GLA_HEADS * GLA_DV
GATE_RANK = 16
GATE_NORMALIZER = 16
CONV_CH = 512
CONV_K = 31
N_GROUPS = 4
EXPERTS_PER_GROUP = 8
N_EXPERTS = N_GROUPS * EXPERTS_PER_GROUP
TOP_K = 2
D_EXPERT = 512
EPS = 1e-6

LANES = 128
SUBLANES = 8
GLA_CHUNK = 64
GLA_TILE = 256
PAIR_DK = 2 * GLA_DK
PAIR_DV = 2 * GLA_DV
CONV_HALO = 16
CONV_ROWS = 32
HALF = D_MODEL // 2
VMEM_LIMIT = 56 * 1024 * 1024


def _cparams(sem, **kw):
    return pltpu.CompilerParams(dimension_semantics=sem, vmem_limit_bytes=VMEM_LIMIT, **kw)


def _dot(a, b):
    return jnp.dot(a, b, preferred_element_type=f32)


def _dot_nt(a, b):
    return lax.dot_general(a, b, (((1,), (1,)), ((), ())), preferred_element_type=f32)


def _dot_tn(a, b):
    return lax.dot_general(a, b, (((0,), (0,)), ((), ())), preferred_element_type=f32)


def _split_bf16(x):
    hi = x.astype(bf16)
    lo = (x - hi.astype(f32)).astype(bf16)
    return hi, lo


def _pack_rows(x):
    a = lax.bitcast_convert_type(x[:, :HALF].astype(bf16).astype(f32), u32)
    b = lax.bitcast_convert_type(x[:, HALF:].astype(bf16).astype(f32), u32)
    return a | (b >> 16)


def _unpack_rows(w):
    a = lax.bitcast_convert_type(w & jnp.uint32(0xFFFF0000), f32)
    b = lax.bitcast_convert_type(w << 16, f32)
    return a, b


def _rms(x):
    return x * lax.rsqrt(jnp.mean(x * x, axis=-1, keepdims=True) + EPS)


def _inproj_kernel(x_ref, g_ref, wqkvo_ref, wa_ref, wc_ref, wgate_ref, bgate_ref,
                   q_ref, k_ref, v_ref, og_ref, la_ref, glu_ref):
    u = (_rms(x_ref[...]) * g_ref[...]).astype(bf16)
    p = _dot(u, wqkvo_ref[...])
    q_ref[...] = (p[:, :GLA_KEY] * (GLA_DK ** -0.5)).astype(bf16)
    k_ref[...] = p[:, GLA_KEY:2 * GLA_KEY].astype(bf16)
    v_ref[...] = p[:, 2 * GLA_KEY:2 * GLA_KEY + GLA_VAL].astype(bf16)
    og_ref[...] = p[:, 2 * GLA_KEY + GLA_VAL:].astype(bf16)
    a = _dot(u, wa_ref[...])
    z = _dot(a.astype(bf16), wgate_ref[...]) + bgate_ref[...]
    la_ref[...] = (jnp.minimum(z, 0.0) - jnp.log1p(jnp.exp(-jnp.abs(z)))) * (1.0 / GATE_NORMALIZER)
    c = _dot(u, wc_ref[...])
    glu_ref[...] = (c[:, :CONV_CH] * jax.nn.sigmoid(c[:, CONV_CH:])).astype(bf16)


def _inproj(x, g_mix, wqkvo, wa, wc, wgate, bgate, tm):
    n = x.shape[0]
    row = lambda w: pl.BlockSpec((tm, w), lambda i: (i, 0))
    full = lambda a: pl.BlockSpec(a.shape, lambda i: (0, 0))
    return pl.pallas_call(
        _inproj_kernel,
        grid=(n // tm,),
        in_specs=[row(D_MODEL), full(g_mix), full(wqkvo), full(wa), full(wc), full(wgate), full(bgate)],
        out_specs=[row(GLA_KEY), row(GLA_KEY), row(GLA_VAL), row(GLA_VAL), row(2 * GLA_KEY), row(CONV_CH)],
        out_shape=[jax.ShapeDtypeStruct((n, GLA_KEY), bf16), jax.ShapeDtypeStruct((n, GLA_KEY), bf16),
                   jax.ShapeDtypeStruct((n, GLA_VAL), bf16), jax.ShapeDtypeStruct((n, GLA_VAL), bf16),
                   jax.ShapeDtypeStruct((n, 2 * GLA_KEY), f32), jax.ShapeDtypeStruct((n, CONV_CH), bf16)],
        compiler_params=_cparams(("parallel",)),
        name="inproj",
    )(x, g_mix, wqkvo, wa, wc, wgate, bgate)


def _gla_direction(q_ref, k_ref, v_ref, la_ref, o_ref, st_ref, reverse):
    tl = la_ref.shape[0]
    la = la_ref[...]
    r = lax.broadcasted_iota(i32, (tl, tl), 0)
    c = lax.broadcasted_iota(i32, (tl, tl), 1)
    tri = jnp.where((r // GLA_CHUNK == c // GLA_CHUNK) & (c <= r), 1.0, 0.0).astype(bf16)
    la_hi, la_lo = _split_bf16(la)
    cs = _dot(tri, la_hi) + _dot(tri, la_lo)

    lane_k = lax.broadcasted_iota(i32, (GLA_CHUNK, PAIR_DK), 1)
    head0_k = lane_k < GLA_DK
    lane_v = lax.broadcasted_iota(i32, (GLA_CHUNK, PAIR_DV), 1)
    head0_v = lane_v < GLA_DV
    t_idx = lax.broadcasted_iota(i32, (GLA_CHUNK, PAIR_DK), 0)
    s_idx = lane_k % GLA_CHUNK
    keep = (s_idx > t_idx) if reverse else (s_idx <= t_idx)
    st_row = lax.broadcasted_iota(i32, (PAIR_DV, PAIR_DK), 0)
    st_col = lax.broadcasted_iota(i32, (PAIR_DV, PAIR_DK), 1)
    same_head = (st_row < GLA_DV) == (st_col < GLA_DK)

    n_chunks = tl // GLA_CHUNK
    order = range(n_chunks - 1, -1, -1) if reverse else range(n_chunks)
    for j in order:
        sl = slice(j * GLA_CHUNK, (j + 1) * GLA_CHUNK)
        cj = cs[sl]
        q = q_ref[sl, :].astype(f32)
        k = k_ref[sl, :].astype(f32)
        v = v_ref[sl, :]
        c_last = cj[GLA_CHUNK - 1:GLA_CHUNK, :]
        mid = 0.5 * c_last
        if reverse:
            e = cj - la[sl]
            qd = q * jnp.exp(mid - e)
            kd = k * jnp.exp(e - mid)
            qi = q * jnp.exp(c_last - e)
            ku = k * jnp.exp(e)
        else:
            qd = q * jnp.exp(cj - mid)
            kd = k * jnp.exp(mid - cj)
            qi = q * jnp.exp(cj)
            ku = k * jnp.exp(c_last - cj)
        kd = kd.astype(bf16)
        zk = jnp.zeros_like(kd)
        kstack = jnp.concatenate([jnp.where(head0_k, kd, zk), jnp.where(head0_k, zk, kd)], axis=0)
        sc = _dot_nt(qd.astype(bf16), kstack)
        p = jnp.where(keep, sc, 0.0).astype(bf16)
        zv = jnp.zeros_like(v)
        vbd = jnp.concatenate([jnp.where(head0_v, v, zv), jnp.where(head0_v, zv, v)], axis=0)
        st = st_ref[...]
        o = _dot(p, vbd) + _dot_nt(qi.astype(bf16), st.astype(bf16))
        o_ref[sl, :] = o.astype(o_ref.dtype)
        kv = _dot_tn(v, ku.astype(bf16))
        st_ref[...] = jnp.exp(c_last) * st + jnp.where(same_head, kv, 0.0)


def _gla_kernel(qf, kf, vf, laf, qb, kb, vb, lab, of, ob, stf, stb):
    @pl.when(pl.program_id(2) == 0)
    def _():
        stf[...] = jnp.zeros_like(stf)
        stb[...] = jnp.zeros_like(stb)

    _gla_direction(qf, kf, vf, laf, of, stf, reverse=False)
    _gla_direction(qb, kb, vb, lab, ob, stb, reverse=True)


def _gla(q, k, v, la, n_seq, seq_len):
    n = q.shape[0]
    nblk = seq_len // GLA_TILE
    n_pairs = GLA_HEADS // 2
    fwd = lambda b, p, i: (b * nblk + i, p)
    bwd = lambda b, p, i: (b * nblk + nblk - 1 - i, p)
    bwd_la = lambda b, p, i: (b * nblk + nblk - 1 - i, n_pairs + p)
    kq = lambda im: pl.BlockSpec((GLA_TILE, PAIR_DK), im)
    vv = lambda im: pl.BlockSpec((GLA_TILE, PAIR_DV), im)
    return pl.pallas_call(
        _gla_kernel,
        grid=(n_seq, n_pairs, nblk),
        in_specs=[kq(fwd), kq(fwd), vv(fwd), kq(fwd), kq(bwd), kq(bwd), vv(bwd), kq(bwd_la)],
        out_specs=[vv(fwd), vv(bwd)],
        out_shape=[jax.ShapeDtypeStruct((n, GLA_VAL), bf16), jax.ShapeDtypeStruct((n, GLA_VAL), bf16)],
        scratch_shapes=[pltpu.VMEM((PAIR_DV, PAIR_DK), f32), pltpu.VMEM((PAIR_DV, PAIR_DK), f32)],
        compiler_params=_cparams(("parallel", "parallel", "arbitrary")),
        name="gla",
    )(q, k, v, la, q, k, v, la)


def _conv_kernel(prev_ref, cur_ref, next_ref, wdw_ref, bdw_ref, gln_ref, bln_ref, wpw_ref,
                 o_ref, ext_ref, y_ref, *, tiles_per_seq):
    tl = cur_ref.shape[0]
    pos = pl.program_id(0) % tiles_per_seq
    prev = prev_ref[...].astype(f32)
    nxt = next_ref[...].astype(f32)
    ext_ref[0:CONV_HALO, :] = jnp.where(pos == 0, 0.0, prev)
    ext_ref[CONV_HALO:CONV_HALO + tl, :] = cur_ref[...].astype(f32)
    ext_ref[CONV_HALO + tl:, :] = jnp.where(pos == tiles_per_seq - 1, 0.0, nxt)
    off = CONV_HALO - CONV_K // 2

    def body(rb, carry):
        base = pl.multiple_of(rb * CONV_ROWS, CONV_ROWS)
        win = ext_ref[pl.ds(base, CONV_ROWS + 2 * CONV_HALO), :]
        acc = jnp.zeros((CONV_ROWS, CONV_CH), f32) + bdw_ref[...]
        for j in range(CONV_K):
            acc = acc + win[off + j:off + j + CONV_ROWS, :] * wdw_ref[j:j + 1, :]
        mu = jnp.mean(acc, axis=-1, keepdims=True)
        xc = acc - mu
        yn = xc * lax.rsqrt(jnp.mean(xc * xc, axis=-1, keepdims=True) + EPS) * gln_ref[...] + bln_ref[...]
        y_ref[pl.ds(base, CONV_ROWS), :] = (yn * jax.nn.sigmoid(yn)).astype(bf16)
        return carry

    lax.fori_loop(0, tl // CONV_ROWS, body, 0)
    o_ref[...] = _dot(y_ref[...], wpw_ref[...]).astype(bf16)


def _conv(glu, wdw, bdw, gln, bln, wpw, seq_len, tl):
    n = glu.shape[0]
    hb = tl // CONV_HALO
    n_halo = n // CONV_HALO
    full = lambda a: pl.BlockSpec(a.shape, lambda i: (0, 0))
    return pl.pallas_call(
        functools.partial(_conv_kernel, tiles_per_seq=seq_len // tl),
        grid=(n // tl,),
        in_specs=[pl.BlockSpec((CONV_HALO, CONV_CH), lambda i: (jnp.maximum(i * hb - 1, 0), 0)),
                  pl.BlockSpec((tl, CONV_CH), lambda i: (i, 0)),
                  pl.BlockSpec((CONV_HALO, CONV_CH), lambda i: (jnp.minimum((i + 1) * hb, n_halo - 1), 0)),
                  full(wdw), full(bdw), full(gln), full(bln), full(wpw)],
        out_specs=pl.BlockSpec((tl, CONV_CH), lambda i: (i, 0)),
        out_shape=jax.ShapeDtypeStruct((n, CONV_CH), bf16),
        scratch_shapes=[pltpu.VMEM((tl + 2 * CONV_HALO, CONV_CH), f32), pltpu.VMEM((tl, CONV_CH), bf16)],
        compiler_params=_cparams(("parallel",)),
        name="conv",
    )(glu, glu, glu, wdw, bdw, gln, bln, wpw)


def _mix_kernel(x_ref, of_ref, ob_ref, og_ref, oc_ref, gh_ref, woa_ref, wob_ref, gf_ref, wr_ref, br_ref,
                h_ref, up_ref, eid_ref, wts_ref):
    tm = x_ref.shape[0]
    o = of_ref[...].astype(f32) + ob_ref[...].astype(f32)
    o = jnp.concatenate([_rms(o[:, h * GLA_DV:(h + 1) * GLA_DV]) for h in range(GLA_HEADS)], axis=1)
    og = og_ref[...].astype(f32)
    o = o * gh_ref[...] * (og * jax.nn.sigmoid(og))
    h = x_ref[...] + _dot(o.astype(bf16), woa_ref[...]) + _dot(oc_ref[...], wob_ref[...])
    h_ref[...] = h
    u = _rms(h) * gf_ref[...]
    up_ref[...] = _pack_rows(u)

    u_hi, u_lo = _split_bf16(u)
    w_hi, w_lo = _split_bf16(wr_ref[...])
    logits = _dot_nt(w_hi, u_hi) + _dot_nt(w_lo, u_hi) + _dot_nt(w_hi, u_lo) + br_ref[...]
    le = logits[0:N_EXPERTS]
    lg = logits[N_EXPERTS:N_EXPERTS + N_GROUPS]
    gmax = jnp.max(lg, axis=0, keepdims=True)
    gi = lax.broadcasted_iota(i32, lg.shape, 0)
    gsel = jnp.min(jnp.where(lg == gmax, gi, N_GROUPS), axis=0, keepdims=True)
    gate = 1.0 / jnp.sum(jnp.exp(lg - gmax), axis=0, keepdims=True)
    ri = lax.broadcasted_iota(i32, le.shape, 0)
    lm = jnp.where(ri // EXPERTS_PER_GROUP == gsel, le, -jnp.inf)
    m1 = jnp.max(lm, axis=0, keepdims=True)
    i1 = jnp.min(jnp.where(lm == m1, ri, N_EXPERTS), axis=0, keepdims=True)
    lm2 = jnp.where(ri == i1, -jnp.inf, lm)
    m2 = jnp.max(lm2, axis=0, keepdims=True)
    i2 = jnp.min(jnp.where(lm2 == m2, ri, N_EXPERTS), axis=0, keepdims=True)
    t = jnp.exp(m2 - m1)
    den = 1.0 / (1.0 + t)
    r8 = lax.broadcasted_iota(i32, (SUBLANES, tm), 0)
    eid_ref[...] = jnp.where(r8 == 0, i1, jnp.where(r8 == 1, i2, 0))
    wts_ref[...] = jnp.where(r8 == 0, gate * den, jnp.where(r8 == 1, gate * t * den, 0.0))


def _mix(x, o_f, o_b, og, oc, g_head, woa, wob, g_ffn, wr, br, tm):
    n = x.shape[0]
    row = lambda w: pl.BlockSpec((tm, w), lambda i: (i, 0))
    col = pl.BlockSpec((SUBLANES, tm), lambda i: (0, i))
    full = lambda a: pl.BlockSpec(a.shape, lambda i: (0, 0))
    return pl.pallas_call(
        _mix_kernel,
        grid=(n // tm,),
        in_specs=[row(D_MODEL), row(GLA_VAL), row(GLA_VAL), row(GLA_VAL), row(CONV_CH),
                  full(g_head), full(woa), full(wob), full(g_ffn), full(wr), full(br)],
        out_specs=[row(D_MODEL), row(HALF), col, col],
        out_shape=[jax.ShapeDtypeStruct((n, D_MODEL), f32), jax.ShapeDtypeStruct((n, HALF), u32),
                   jax.ShapeDtypeStruct((SUBLANES, n), i32), jax.ShapeDtypeStruct((SUBLANES, n), f32)],
        compiler_params=_cparams(("parallel",)),
        name="mix",
    )(x, o_f, o_b, og, oc, g_head, woa, wob, g_ffn, wr, br)


def _rank_kernel(eid_ref, dest_ref, blk_ref, cnt_ref, base_ref, *, blk, nbp):
    ps = pl.program_id(0)
    i = pl.program_id(1)
    tm = eid_ref.shape[1]
    ri = lax.broadcasted_iota(i32, (N_EXPERTS, tm), 0)
    oh1 = ri == eid_ref[0:1, :]
    oh2 = ri == eid_ref[1:2, :]
    ohf = jnp.where(oh1 | oh2, 1.0, 0.0)
    tile_cnt = jnp.sum(ohf, axis=1, keepdims=True)

    @pl.when((ps == 0) & (i == 0))
    def _():
        cnt_ref[...] = jnp.zeros_like(cnt_ref)

    @pl.when(ps == 0)
    def _():
        cnt_ref[...] += tile_cnt
        dest_ref[...] = jnp.zeros_like(dest_ref)

    @pl.when((ps == 1) & (i == 0))
    def _():
        cnt = cnt_ref[...]
        pc = jnp.floor((cnt + (blk - 1)) * (1.0 / blk)) * blk
        rr = lax.broadcasted_iota(i32, cnt.shape, 0)
        inc = pc
        for s in (1, 2, 4, 8, 16):
            inc = inc + jnp.where(rr >= s, pltpu.roll(inc, s, axis=0), 0.0)
        base_ref[...] = inc - pc
        pend = jnp.concatenate([inc] * (nbp // LANES), axis=1)
        jl = lax.broadcasted_iota(i32, (N_EXPERTS, nbp), 1).astype(f32) * blk
        be = jnp.minimum(jnp.sum(jnp.where(pend <= jl, 1.0, 0.0), axis=0, keepdims=True), N_EXPERTS - 1.0)
        nused = jnp.concatenate([inc[N_EXPERTS - 1:N_EXPERTS, :]] * (nbp // LANES), axis=1) * (1.0 / blk)
        r8 = lax.broadcasted_iota(i32, (SUBLANES, nbp), 0)
        blk_ref[...] = jnp.where(r8 == 0, be, jnp.where(r8 == 1, nused, 0.0)).astype(i32)

    @pl.when(ps == 1)
    def _():
        a = lax.broadcasted_iota(i32, (tm, tm), 0)
        b = lax.broadcasted_iota(i32, (tm, tm), 1)
        upper = jnp.where(a < b, 1.0, 0.0).astype(bf16)
        before = _dot(ohf.astype(bf16), upper)
        slot = base_ref[:, 0:1] + before
        d1 = jnp.sum(jnp.where(oh1, slot, 0.0), axis=0, keepdims=True)
        d2 = jnp.sum(jnp.where(oh2, slot, 0.0), axis=0, keepdims=True)
        r8 = lax.broadcasted_iota(i32, (SUBLANES, tm), 0)
        dest_ref[0] = jnp.where(r8 == 0, d1, jnp.where(r8 == 1, d2, 0.0)).astype(i32)
        base_ref[...] += tile_cnt


def _rank(eid, blk, nbp, tm):
    n = eid.shape[1]
    return pl.pallas_call(
        functools.partial(_rank_kernel, blk=blk, nbp=nbp),
        grid=(2, n // tm),
        in_specs=[pl.BlockSpec((SUBLANES, tm), lambda p, i: (0, i))],
        out_specs=[pl.BlockSpec((1, SUBLANES, tm), lambda p, i: (p, 0, i)),
                   pl.BlockSpec((SUBLANES, nbp), lambda p, i: (0, 0))],
        out_shape=[jax.ShapeDtypeStruct((2, SUBLANES, n), i32), jax.ShapeDtypeStruct((SUBLANES, nbp), i32)],
        scratch_shapes=[pltpu.VMEM((N_EXPERTS, LANES), f32), pltpu.VMEM((N_EXPERTS, LANES), f32)],
        compiler_params=_cparams(("arbitrary", "arbitrary")),
        name="rank",
    )(eid)


def _dispatch_kernel(d1_ref, d2_ref, u_ref, xz_ref, xs_ref, sem):
    del xz_ref
    tm = u_ref.shape[0]

    def body(n, carry):
        src = u_ref.at[pl.ds(n, 1)]
        pltpu.make_async_copy(src, xs_ref.at[pl.ds(d1_ref[n], 1)], sem).start()
        pltpu.make_async_copy(src, xs_ref.at[pl.ds(d2_ref[n], 1)], sem).start()
        return carry

    lax.fori_loop(0, tm, body, 0)
    pltpu.make_async_copy(u_ref, xs_ref.at[pl.ds(0, tm)], sem).wait()
    pltpu.make_async_copy(u_ref, xs_ref.at[pl.ds(0, tm)], sem).wait()


def _dispatch(d1, d2, up, xz, tm):
    n = up.shape[0]
    smem = pl.BlockSpec((tm,), lambda i: (i,), memory_space=pltpu.SMEM)
    return pl.pallas_call(
        _dispatch_kernel,
        grid=(n // tm,),
        in_specs=[smem, smem, pl.BlockSpec((tm, HALF), lambda i: (i, 0)), pl.BlockSpec(memory_space=pl.ANY)],
        out_specs=pl.BlockSpec(memory_space=pl.ANY),
        out_shape=jax.ShapeDtypeStruct(xz.shape, u32),
        scratch_shapes=[pltpu.SemaphoreType.DMA(())],
        input_output_aliases={3: 0},
        compiler_params=_cparams(("arbitrary",)),
        name="dispatch",
    )(d1, d2, up, xz)


def _expert_kernel(be_ref, nu_ref, xs_ref, wg_ref, wu_ref, wd_ref, y_ref):
    j = pl.program_id(0)

    @pl.when(j < nu_ref[0])
    def _():
        xa, xb = _unpack_rows(xs_ref[...])
        xa = xa.astype(bf16)
        xb = xb.astype(bf16)
        g = _dot(xa, wg_ref[0, :HALF, :]) + _dot(xb, wg_ref[0, HALF:, :])
        u = _dot(xa, wu_ref[0, :HALF, :]) + _dot(xb, wu_ref[0, HALF:, :])
        hb = (g * jax.nn.sigmoid(g) * u).astype(bf16)
        y_ref[...] = _pack_rows(_dot(hb, wd_ref[0]))

    @pl.when(j >= nu_ref[0])
    def _():
        y_ref[...] = jnp.zeros_like(y_ref)


def _experts(block_e, nused, xs, wg, wu, wd, blk):
    nb = xs.shape[0] // blk
    return pl.pallas_call(
        _expert_kernel,
        grid_spec=pltpu.PrefetchScalarGridSpec(
            num_scalar_prefetch=2,
            grid=(nb,),
            in_specs=[pl.BlockSpec((blk, HALF), lambda j, be, nu: (j, 0)),
                      pl.BlockSpec((1, D_MODEL, D_EXPERT), lambda j, be, nu: (be[j], 0, 0)),
                      pl.BlockSpec((1, D_MODEL, D_EXPERT), lambda j, be, nu: (be[j], 0, 0)),
                      pl.BlockSpec((1, D_EXPERT, D_MODEL), lambda j, be, nu: (be[j], 0, 0))],
            out_specs=pl.BlockSpec((blk, HALF), lambda j, be, nu: (j, 0)),
        ),
        out_shape=jax.ShapeDtypeStruct(xs.shape, u32),
        compiler_params=_cparams(("arbitrary",)),
        name="experts",
    )(block_e, nused, xs, wg, wu, wd)


def _combine_kernel(d1_ref, d2_ref, h_ref, wts_ref, gfin_ref, y_ref, o_ref, y1_buf, y2_buf, sem):
    tm = h_ref.shape[0]

    def body(n, carry):
        pltpu.make_async_copy(y_ref.at[pl.ds(d1_ref[n], 1)], y1_buf.at[pl.ds(n, 1)], sem).start()
        pltpu.make_async_copy(y_ref.at[pl.ds(d2_ref[n], 1)], y2_buf.at[pl.ds(n, 1)], sem).start()
        return carry

    lax.fori_loop(0, tm, body, 0)
    w_rows = jnp.concatenate([wts_ref[...], jnp.zeros((LANES - SUBLANES, tm), f32)], axis=0)
    w_cols = jnp.transpose(w_rows)
    w1 = w_cols[:, 0:1]
    w2 = w_cols[:, 1:2]
    pltpu.make_async_copy(y_ref.at[pl.ds(0, tm)], y1_buf, sem).wait()
    pltpu.make_async_copy(y_ref.at[pl.ds(0, tm)], y2_buf, sem).wait()
    a1, b1 = _unpack_rows(y1_buf[...])
    a2, b2 = _unpack_rows(y2_buf[...])
    h = h_ref[...]
    ha = h[:, :HALF] + a1 * w1 + a2 * w2
    hb = h[:, HALF:] + b1 * w1 + b2 * w2
    ms = (jnp.sum(ha * ha, axis=-1, keepdims=True) + jnp.sum(hb * hb, axis=-1, keepdims=True)) * (1.0 / D_MODEL)
    inv = lax.rsqrt(ms + EPS)
    o_ref[:, :HALF] = ha * inv * gfin_ref[:, :HALF]
    o_ref[:, HALF:] = hb * inv * gfin_ref[:, HALF:]


def _combine(d1, d2, h, wts, g_final, ybuf, tm):
    n = h.shape[0]
    smem = pl.BlockSpec((tm,), lambda i: (i,), memory_space=pltpu.SMEM)
    return pl.pallas_call(
        _combine_kernel,
        grid=(n // tm,),
        in_specs=[smem, smem, pl.BlockSpec((tm, D_MODEL), lambda i: (i, 0)),
                  pl.BlockSpec((SUBLANES, tm), lambda i: (0, i)),
                  pl.BlockSpec(g_final.shape, lambda i: (0, 0)),
                  pl.BlockSpec(memory_space=pl.ANY)],
        out_specs=pl.BlockSpec((tm, D_MODEL), lambda i: (i, 0)),
        out_shape=jax.ShapeDtypeStruct((n, D_MODEL), f32),
        scratch_shapes=[pltpu.VMEM((tm, HALF), u32), pltpu.VMEM((tm, HALF), u32), pltpu.SemaphoreType.DMA(())],
        compiler_params=_cparams(("arbitrary",)),
        name="combine",
    )(d1, d2, h, wts, g_final, ybuf)


def _tile(n, pref):
    t = pref
    while n % t:
        t //= 2
    return t


def _prep_weights(g_mix, w_in, w_af2, b_af, w_ab2, b_ab, g_head, w_dw, b_dw, g_ln, b_ln, w_pw2, w_out,
                  g_ffn, w_rg, b_rg, w_re, b_re, w_e_gate, w_e_up, w_e_down, g_final):
    n_qkvo = 2 * GLA_KEY + 2 * GLA_VAL
    n_a = 2 * GATE_RANK
    w_in = w_in[0]
    zg = jnp.zeros((GATE_RANK, GLA_KEY), f32)
    wgate = jnp.concatenate([jnp.concatenate([w_af2[0], zg], axis=1), jnp.concatenate([zg, w_ab2[0]], axis=1)], axis=0)
    wdw = jnp.concatenate([w_dw[0], jnp.zeros((1, CONV_CH), f32)], axis=0)
    pad = jnp.zeros((SUBLANES - N_GROUPS, D_MODEL), f32)
    wr = jnp.concatenate([w_re[0].T, w_rg[0].T, pad], axis=0)
    br = jnp.concatenate([b_re[0], b_rg[0], jnp.zeros((SUBLANES - N_GROUPS,), f32)])[:, None]
    return dict(
        g_mix=g_mix[0][None, :],
        wqkvo=w_in[:, :n_qkvo].astype(bf16),
        wa=w_in[:, n_qkvo:n_qkvo + n_a].astype(bf16),
        wc=w_in[:, n_qkvo + n_a:].astype(bf16),
        wgate=wgate.astype(bf16),
        bgate=jnp.concatenate([b_af[0], b_ab[0]])[None, :],
        g_head=g_head[0][None, :],
        wdw=wdw, bdw=b_dw[0][None, :], gln=g_ln[0][None, :], bln=b_ln[0][None, :],
        wpw=w_pw2[0].astype(bf16),
        woa=w_out[0, :GLA_VAL].astype(bf16), wob=w_out[0, GLA_VAL:].astype(bf16),
        g_ffn=g_ffn[0][None, :], wr=wr, br=br,
        wg=w_e_gate[0].astype(bf16), wu=w_e_up[0].astype(bf16), wd=w_e_down[0].astype(bf16),
        g_final=g_final[None, :],
    )


def _trunk(x3, w, moe_block):
    n_seq, seq_len, _ = x3.shape
    n = n_seq * seq_len
    x = x3.reshape(n, D_MODEL)
    tm = _tile(n, 512)
    q, k, v, og, la, glu = _inproj(x, w["g_mix"], w["wqkvo"], w["wa"], w["wc"], w["wgate"], w["bgate"], tm)
    o_f, o_b = _gla(q, k, v, la, n_seq, seq_len)
    oc = _conv(glu, w["wdw"], w["bdw"], w["gln"], w["bln"], w["wpw"], seq_len, _tile(seq_len, 512))
    h, up, eid, wts = _mix(x, o_f, o_b, og, oc, w["g_head"], w["woa"], w["wob"], w["g_ffn"], w["wr"], w["br"], tm)

    nb = (n * TOP_K) // moe_block + N_EXPERTS
    nbp = -(-nb // LANES) * LANES
    dest, blkinfo = _rank(eid, moe_block, nbp, tm)
    d1, d2 = dest[1, 0], dest[1, 1]
    xs = _dispatch(d1, d2, up, jnp.zeros((nb * moe_block, HALF), u32), _tile(n, 512))
    ybuf = _experts(blkinfo[0], blkinfo[1, :1], xs, w["wg"], w["wu"], w["wd"], moe_block)
    y = _combine(d1, d2, h, wts, w["g_final"], ybuf, _tile(n, 256))
    return y.reshape(x3.shape)


def kernel(x_prompt, x_sample, g_mix, w_in, w_af2, b_af, w_ab2, b_ab, g_head, w_dw, b_dw, g_ln, b_ln, w_pw2,
           w_out, g_ffn, w_rg, b_rg, w_re, b_re, w_e_gate, w_e_up, w_e_down, g_final):
    w = _prep_weights(g_mix, w_in, w_af2, b_af, w_ab2, b_ab, g_head, w_dw, b_dw, g_ln, b_ln, w_pw2, w_out,
                      g_ffn, w_rg, b_rg, w_re, b_re, w_e_gate, w_e_up, w_e_down, g_final)
    moe_block = 512 if x_prompt.shape[0] * x_prompt.shape[1] >= 8192 else 128
    return (_trunk(x_prompt, w, moe_block), _trunk(x_sample, w, moe_block))
```

```python
import functools

import jax
import jax.numpy as jnp
from jax import lax
from jax.experimental import pallas as pl
from jax.experimental.pallas import tpu as pltpu

f32 = jnp.float32
bf16 = jnp.bfloat16
i32 = jnp.int32
u32 = jnp.uint32

D_MODEL = 1024
GLA_HEADS = 4
GLA_DK = 64
GLA_DV = 128
GLA_KEY = GLA_HEADS * GLA_DK
GLA_VAL = GLA_HEADS * GLA_DV
GATE_RANK = 16
GATE_NORMALIZER = 16
CONV_CH = 512
CONV_K = 31
N_GROUPS = 4
EXPERTS_PER_GROUP = 8
N_EXPERTS = N_GROUPS * EXPERTS_PER_GROUP
TOP_K = 2
D_EXPERT = 512
EPS = 1e-6

LANES = 128
SUBLANES = 8
GLA_CHUNK = 64
GLA_TILE = 256
PAIR_DK = 2 * GLA_DK
PAIR_DV = 2 * GLA_DV
CONV_HALO = 16
CONV_ROWS = 16
HALF = D_MODEL // 2
VMEM_LIMIT = 56 * 1024 * 1024


def _cparams(sem, **kw):
    return pltpu.CompilerParams(dimension_semantics=sem, vmem_limit_bytes=VMEM_LIMIT, **kw)


def _dot(a, b):
    return jnp.dot(a, b, preferred_element_type=f32)


def _dot_nt(a, b):
    return lax.dot_general(a, b, (((1,), (1,)), ((), ())), preferred_element_type=f32)


def _dot_tn(a, b):
    return lax.dot_general(a, b, (((0,), (0,)), ((), ())), preferred_element_type=f32)


def _split_bf16(x):
    hi = x.astype(bf16)
    lo = (x - hi.astype(f32)).astype(bf16)
    return hi, lo


def _pack_rows(x):
    a = lax.bitcast_convert_type(x[:, :HALF].astype(bf16).astype(f32), u32)
    b = lax.bitcast_convert_type(x[:, HALF:].astype(bf16).astype(f32), u32)
    return a | (b >> 16)


def _unpack_rows(w):
    a = lax.bitcast_convert_type(w & jnp.uint32(0xFFFF0000), f32)
    b = lax.bitcast_convert_type(w << 16, f32)
    return a, b


def _rms(x):
    return x * lax.rsqrt(jnp.mean(x * x, axis=-1, keepdims=True) + EPS)


def _inproj_kernel(x_ref, g_ref, wqkvo_ref, wa_ref, wc_ref, wgate_ref, bgate_ref,
                   q_ref, k_ref, v_ref, og_ref, la_ref, glu_ref):
    u = (_rms(x_ref[...]) * g_ref[...]).astype(bf16)
    p = _dot(u, wqkvo_ref[...])
    q_ref[...] = (p[:, :GLA_KEY] * (GLA_DK ** -0.5)).astype(bf16)
    k_ref[...] = p[:, GLA_KEY:2 * GLA_KEY].astype(bf16)
    v_ref[...] = p[:, 2 * GLA_KEY:2 * GLA_KEY + GLA_VAL].astype(bf16)
    og_ref[...] = p[:, 2 * GLA_KEY + GLA_VAL:].astype(bf16)
    a = _dot(u, wa_ref[...])
    z = _dot(a.astype(bf16), wgate_ref[...]) + bgate_ref[...]
    la_ref[...] = (jnp.minimum(z, 0.0) - jnp.log1p(jnp.exp(-jnp.abs(z)))) * (1.0 / GATE_NORMALIZER)
    c = _dot(u, wc_ref[...])
    glu_ref[...] = (c[:, :CONV_CH] * jax.nn.sigmoid(c[:, CONV_CH:])).astype(bf16)


def _inproj(x, g_mix, wqkvo, wa, wc, wgate, bgate, tm):
    n = x.shape[0]
    row = lambda w: pl.BlockSpec((tm, w), lambda i: (i, 0))
    full = lambda a: pl.BlockSpec(a.shape, lambda i: (0, 0))
    return pl.pallas_call(
        _inproj_kernel,
        grid=(n // tm,),
        in_specs=[row(D_MODEL), full(g_mix), full(wqkvo), full(wa), full(wc), full(wgate), full(bgate)],
        out_specs=[row(GLA_KEY), row(GLA_KEY), row(GLA_VAL), row(GLA_VAL), row(2 * GLA_KEY), row(CONV_CH)],
        out_shape=[jax.ShapeDtypeStruct((n, GLA_KEY), bf16), jax.ShapeDtypeStruct((n, GLA_KEY), bf16),
                   jax.ShapeDtypeStruct((n, GLA_VAL), bf16), jax.ShapeDtypeStruct((n, GLA_VAL), bf16),
                   jax.ShapeDtypeStruct((n, 2 * GLA_KEY), f32), jax.ShapeDtypeStruct((n, CONV_CH), bf16)],
        compiler_params=_cparams(("parallel",)),
        name="inproj",
    )(x, g_mix, wqkvo, wa, wc, wgate, bgate)


def _gla_direction(q_ref, k_ref, v_ref, la_ref, o_ref, st_ref, reverse):
    tl = la_ref.shape[0]
    la = la_ref[...]
    r = lax.broadcasted_iota(i32, (tl, tl), 0)
    c = lax.broadcasted_iota(i32, (tl, tl), 1)
    tri = jnp.where((r // GLA_CHUNK == c // GLA_CHUNK) & (c <= r), 1.0, 0.0).astype(bf16)
    la_hi, la_lo = _split_bf16(la)
    cs = _dot(tri, la_hi) + _dot(tri, la_lo)

    lane_k = lax.broadcasted_iota(i32, (GLA_CHUNK, PAIR_DK), 1)
    head0_k = lane_k < GLA_DK
    lane_v = lax.broadcasted_iota(i32, (GLA_CHUNK, PAIR_DV), 1)
    head0_v = lane_v < GLA_DV
    t_idx = lax.broadcasted_iota(i32, (GLA_CHUNK, PAIR_DK), 0)
    s_idx = lane_k % GLA_CHUNK
    keep = (s_idx > t_idx) if reverse else (s_idx <= t_idx)
    st_row = lax.broadcasted_iota(i32, (PAIR_DV, PAIR_DK), 0)
    st_col = lax.broadcasted_iota(i32, (PAIR_DV, PAIR_DK), 1)
    same_head = (st_row < GLA_DV) == (st_col < GLA_DK)

    n_chunks = tl // GLA_CHUNK
    order = range(n_chunks - 1, -1, -1) if reverse else range(n_chunks)
    for j in order:
        sl = slice(j * GLA_CHUNK, (j + 1) * GLA_CHUNK)
        cj = cs[sl]
        q = q_ref[sl, :].astype(f32)
        k = k_ref[sl, :].astype(f32)
        v = v_ref[sl, :]
        c_last = cj[GLA_CHUNK - 1:GLA_CHUNK, :]
        mid = 0.5 * c_last
        if reverse:
            e = cj - la[sl]
            qd = q * jnp.exp(mid - e)
            kd = k * jnp.exp(e - mid)
            qi = q * jnp.exp(c_last - e)
            ku = k * jnp.exp(e)
        else:
            qd = q * jnp.exp(cj - mid)
            kd = k * jnp.exp(mid - cj)
            qi = q * jnp.exp(cj)
            ku = k * jnp.exp(c_last - cj)
        kd = kd.astype(bf16)
        zk = jnp.zeros_like(kd)
        kstack = jnp.concatenate([jnp.where(head0_k, kd, zk), jnp.where(head0_k, zk, kd)], axis=0)
        sc = _dot_nt(qd.astype(bf16), kstack)
        p = jnp.where(keep, sc, 0.0).astype(bf16)
        zv = jnp.zeros_like(v)
        vbd = jnp.concatenate([jnp.where(head0_v, v, zv), jnp.where(head0_v, zv, v)], axis=0)
        st = st_ref[...]
        o = _dot(p, vbd) + _dot_nt(qi.astype(bf16), st.astype(bf16))
        o_ref[sl, :] = o.astype(o_ref.dtype)
        kv = _dot_tn(v, ku.astype(bf16))
        st_ref[...] = jnp.exp(c_last) * st + jnp.where(same_head, kv, 0.0)


def _gla_kernel(qf, kf, vf, laf, qb, kb, vb, lab, of, ob, stf, stb):
    @pl.when(pl.program_id(2) == 0)
    def _():
        stf[...] = jnp.zeros_like(stf)
        stb[...] = jnp.zeros_like(stb)

    _gla_direction(qf, kf, vf, laf, of, stf, reverse=False)
    _gla_direction(qb, kb, vb, lab, ob, stb, reverse=True)


def _gla(q, k, v, la, n_seq, seq_len):
    n = q.shape[0]
    nblk = seq_len // GLA_TILE
    n_pairs = GLA_HEADS // 2
    fwd = lambda b, p, i: (b * nblk + i, p)
    bwd = lambda b, p, i: (b * nblk + nblk - 1 - i, p)
    bwd_la = lambda b, p, i: (b * nblk + nblk - 1 - i, n_pairs + p)
    kq = lambda im: pl.BlockSpec((GLA_TILE, PAIR_DK), im)
    vv = lambda im: pl.BlockSpec((GLA_TILE, PAIR_DV), im)
    return pl.pallas_call(
        _gla_kernel,
        grid=(n_seq, n_pairs, nblk),
        in_specs=[kq(fwd), kq(fwd), vv(fwd), kq(fwd), kq(bwd), kq(bwd), vv(bwd), kq(bwd_la)],
        out_specs=[vv(fwd), vv(bwd)],
        out_shape=[jax.ShapeDtypeStruct((n, GLA_VAL), bf16), jax.ShapeDtypeStruct((n, GLA_VAL), bf16)],
        scratch_shapes=[pltpu.VMEM((PAIR_DV, PAIR_DK), f32), pltpu.VMEM((PAIR_DV, PAIR_DK), f32)],
        compiler_params=_cparams(("parallel", "parallel", "arbitrary")),
        name="gla",
    )(q, k, v, la, q, k, v, la)


def _conv_kernel(prev_ref, cur_ref, next_ref, wdw_ref, bdw_ref, gln_ref, bln_ref, wpw_ref,
                 o_ref, ext_ref, y_ref, *, tiles_per_seq):
    tl = cur_ref.shape[0]
    pos = pl.program_id(0) % tiles_per_seq
    prev = jnp.where(pos == 0, 0.0, prev_ref[...].astype(f32))
    nxt = jnp.where(pos == tiles_per_seq - 1, 0.0, next_ref[...].astype(f32))
    cur = cur_ref[...].astype(f32)
    n_slabs = CONV_CH // LANES
    for c in range(n_slabs):
        lanes = slice(c * LANES, (c + 1) * LANES)
        ext_ref[c, 0:CONV_HALO, :] = prev[:, lanes]
        ext_ref[c, CONV_HALO:CONV_HALO + tl, :] = cur[:, lanes]
        ext_ref[c, CONV_HALO + tl:, :] = nxt[:, lanes]
    off = CONV_HALO - CONV_K // 2

    def body(rb, carry):
        base = pl.multiple_of(rb * CONV_ROWS, CONV_ROWS)
        for c in range(n_slabs):
            lanes = slice(c * LANES, (c + 1) * LANES)
            a = jnp.zeros((CONV_ROWS, LANES), f32) + bdw_ref[:, lanes]
            for j in range(CONV_K):
                taps = ext_ref.at[c, pl.ds(off + j, tl)]
                a = a + taps[pl.ds(base, CONV_ROWS), :] * wdw_ref[j:j + 1, lanes]
            y_ref[pl.ds(base, CONV_ROWS), lanes] = a
        return carry

    lax.fori_loop(0, tl // CONV_ROWS, body, 0)
    acc = y_ref[...]
    mu = jnp.mean(acc, axis=-1, keepdims=True)
    xc = acc - mu
    yn = xc * lax.rsqrt(jnp.mean(xc * xc, axis=-1, keepdims=True) + EPS) * gln_ref[...] + bln_ref[...]
    o_ref[...] = _dot((yn * jax.nn.sigmoid(yn)).astype(bf16), wpw_ref[...]).astype(bf16)


def _conv(glu, wdw, bdw, gln, bln, wpw, seq_len, tl):
    n = glu.shape[0]
    hb = tl // CONV_HALO
    n_halo = n // CONV_HALO
    full = lambda a: pl.BlockSpec(a.shape, lambda i: (0, 0))
    return pl.pallas_call(
        functools.partial(_conv_kernel, tiles_per_seq=seq_len // tl),
        grid=(n // tl,),
        in_specs=[pl.BlockSpec((CONV_HALO, CONV_CH), lambda i: (jnp.maximum(i * hb - 1, 0), 0)),
                  pl.BlockSpec((tl, CONV_CH), lambda i: (i, 0)),
                  pl.BlockSpec((CONV_HALO, CONV_CH), lambda i: (jnp.minimum((i + 1) * hb, n_halo - 1), 0)),
                  full(wdw), full(bdw), full(gln), full(bln), full(wpw)],
        out_specs=pl.BlockSpec((tl, CONV_CH), lambda i: (i, 0)),
        out_shape=jax.ShapeDtypeStruct((n, CONV_CH), bf16),
        scratch_shapes=[pltpu.VMEM((CONV_CH // LANES, tl + 2 * CONV_HALO, LANES), f32),
                        pltpu.VMEM((tl, CONV_CH), f32)],
        compiler_params=_cparams(("parallel",)),
        name="conv",
    )(glu, glu, glu, wdw, bdw, gln, bln, wpw)


def _mix_kernel(x_ref, of_ref, ob_ref, og_ref, oc_ref, gh_ref, woa_ref, wob_ref, gf_ref, wr_ref, br_ref,
                h_ref, up_ref, eid_ref, wts_ref):
    tm = x_ref.shape[0]
    o = of_ref[...].astype(f32) + ob_ref[...].astype(f32)
    o = jnp.concatenate([_rms(o[:, h * GLA_DV:(h + 1) * GLA_DV]) for h in range(GLA_HEADS)], axis=1)
    og = og_ref[...].astype(f32)
    o = o * gh_ref[...] * (og * jax.nn.sigmoid(og))
    h = x_ref[...] + _dot(o.astype(bf16), woa_ref[...]) + _dot(oc_ref[...], wob_ref[...])
    h_ref[...] = h
    u = _rms(h) * gf_ref[...]
    up_ref[...] = _pack_rows(u)

    u_hi, u_lo = _split_bf16(u)
    w_hi, w_lo = _split_bf16(wr_ref[...])
    logits = _dot_nt(w_hi, u_hi) + _dot_nt(w_lo, u_hi) + _dot_nt(w_hi, u_lo) + br_ref[...]
    le = logits[0:N_EXPERTS]
    lg = logits[N_EXPERTS:N_EXPERTS + N_GROUPS]
    gmax = jnp.max(lg, axis=0, keepdims=True)
    gi = lax.broadcasted_iota(i32, lg.shape, 0)
    gsel = jnp.min(jnp.where(lg == gmax, gi, N_GROUPS), axis=0, keepdims=True)
    gate = 1.0 / jnp.sum(jnp.exp(lg - gmax), axis=0, keepdims=True)
    ri = lax.broadcasted_iota(i32, le.shape, 0)
    lm = jnp.where(ri // EXPERTS_PER_GROUP == gsel, le, -jnp.inf)
    m1 = jnp.max(lm, axis=0, keepdims=True)
    i1 = jnp.min(jnp.where(lm == m1, ri, N_EXPERTS), axis=0, keepdims=True)
    lm2 = jnp.where(ri == i1, -jnp.inf, lm)
    m2 = jnp.max(lm2, axis=0, keepdims=True)
    i2 = jnp.min(jnp.where(lm2 == m2, ri, N_EXPERTS), axis=0, keepdims=True)
    t = jnp.exp(m2 - m1)
    den = 1.0 / (1.0 + t)
    r8 = lax.broadcasted_iota(i32, (SUBLANES, tm), 0)
    eid_ref[...] = jnp.where(r8 == 0, i1, jnp.where(r8 == 1, i2, 0))
    wts_ref[...] = jnp.where(r8 == 0, gate * den, jnp.where(r8 == 1, gate * t * den, 0.0))


def _mix(x, o_f, o_b, og, oc, g_head, woa, wob, g_ffn, wr, br, tm):
    n = x.shape[0]
    row = lambda w: pl.BlockSpec((tm, w), lambda i: (i, 0))
    col = pl.BlockSpec((SUBLANES, tm), lambda i: (0, i))
    full = lambda a: pl.BlockSpec(a.shape, lambda i: (0, 0))
    return pl.pallas_call(
        _mix_kernel,
        grid=(n // tm,),
        in_specs=[row(D_MODEL), row(GLA_VAL), row(GLA_VAL), row(GLA_VAL), row(CONV_CH),
                  full(g_head), full(woa), full(wob), full(g_ffn), full(wr), full(br)],
        out_specs=[row(D_MODEL), row(HALF), col, col],
        out_shape=[jax.ShapeDtypeStruct((n, D_MODEL), f32), jax.ShapeDtypeStruct((n, HALF), u32),
                   jax.ShapeDtypeStruct((SUBLANES, n), i32), jax.ShapeDtypeStruct((SUBLANES, n), f32)],
        compiler_params=_cparams(("parallel",)),
        name="mix",
    )(x, o_f, o_b, og, oc, g_head, woa, wob, g_ffn, wr, br)


def _rank_kernel(eid_ref, dest_ref, blk_ref, cnt_ref, base_ref, *, blk, nbp):
    ps = pl.program_id(0)
    i = pl.program_id(1)
    tm = eid_ref.shape[1]
    ri = lax.broadcasted_iota(i32, (N_EXPERTS, tm), 0)
    oh1 = ri == eid_ref[0:1, :]
    oh2 = ri == eid_ref[1:2, :]
    ohf = jnp.where(oh1 | oh2, 1.0, 0.0)
    tile_cnt = jnp.sum(ohf, axis=1, keepdims=True)

    @pl.when((ps == 0) & (i == 0))
    def _():
        cnt_ref[...] = jnp.zeros_like(cnt_ref)

    @pl.when(ps == 0)
    def _():
        cnt_ref[...] += tile_cnt
        dest_ref[...] = jnp.zeros_like(dest_ref)

    @pl.when((ps == 1) & (i == 0))
    def _():
        cnt = cnt_ref[...]
        pc = jnp.floor((cnt + (blk - 1)) * (1.0 / blk)) * blk
        rr = lax.broadcasted_iota(i32, cnt.shape, 0)
        inc = pc
        for s in (1, 2, 4, 8, 16):
            inc = inc + jnp.where(rr >= s, pltpu.roll(inc, s, axis=0), 0.0)
        base_ref[...] = inc - pc
        pend = jnp.concatenate([inc] * (nbp // LANES), axis=1)
        jl = lax.broadcasted_iota(i32, (N_EXPERTS, nbp), 1).astype(f32) * blk
        be = jnp.minimum(jnp.sum(jnp.where(pend <= jl, 1.0, 0.0), axis=0, keepdims=True), N_EXPERTS - 1.0)
        nused = jnp.concatenate([inc[N_EXPERTS - 1:N_EXPERTS, :]] * (nbp // LANES), axis=1) * (1.0 / blk)
        r8 = lax.broadcasted_iota(i32, (SUBLANES, nbp), 0)
        blk_ref[...] = jnp.where(r8 == 0, be, jnp.where(r8 == 1, nused, 0.0)).astype(i32)

    @pl.when(ps == 1)
    def _():
        a = lax.broadcasted_iota(i32, (tm, tm), 0)
        b = lax.broadcasted_iota(i32, (tm, tm), 1)
        upper = jnp.where(a < b, 1.0, 0.0).astype(bf16)
        before = _dot(ohf.astype(bf16), upper)
        slot = base_ref[:, 0:1] + before
        d1 = jnp.sum(jnp.where(oh1, slot, 0.0), axis=0, keepdims=True)
        d2 = jnp.sum(jnp.where(oh2, slot, 0.0), axis=0, keepdims=True)
        r8 = lax.broadcasted_iota(i32, (SUBLANES, tm), 0)
        dest_ref[0] = jnp.where(r8 == 0, d1, jnp.where(r8 == 1, d2, 0.0)).astype(i32)
        base_ref[...] += tile_cnt


def _rank(eid, blk, nbp, tm):
    n = eid.shape[1]
    return pl.pallas_call(
        functools.partial(_rank_kernel, blk=blk, nbp=nbp),
        grid=(2, n // tm),
        in_specs=[pl.BlockSpec((SUBLANES, tm), lambda p, i: (0, i))],
        out_specs=[pl.BlockSpec((1, SUBLANES, tm), lambda p, i: (p, 0, i)),
                   pl.BlockSpec((SUBLANES, nbp), lambda p, i: (0, 0))],
        out_shape=[jax.ShapeDtypeStruct((2, SUBLANES, n), i32), jax.ShapeDtypeStruct((SUBLANES, nbp), i32)],
        scratch_shapes=[pltpu.VMEM((N_EXPERTS, LANES), f32), pltpu.VMEM((N_EXPERTS, LANES), f32)],
        compiler_params=_cparams(("arbitrary", "arbitrary")),
        name="rank",
    )(eid)


def _dispatch_kernel(d1_ref, d2_ref, u_ref, xz_ref, xs_ref, sem):
    del xz_ref
    tm = u_ref.shape[0]

    def body(n, carry):
        src = u_ref.at[pl.ds(n, 1)]
        pltpu.make_async_copy(src, xs_ref.at[pl.ds(d1_ref[n], 1)], sem).start()
        pltpu.make_async_copy(src, xs_ref.at[pl.ds(d2_ref[n], 1)], sem).start()
        return carry

    lax.fori_loop(0, tm, body, 0)
    pltpu.make_async_copy(u_ref, xs_ref.at[pl.ds(0, tm)], sem).wait()
    pltpu.make_async_copy(u_ref, xs_ref.at[pl.ds(0, tm)], sem).wait()


def _dispatch(d1, d2, up, xz, tm):
    n = up.shape[0]
    smem = pl.BlockSpec((tm,), lambda i: (i,), memory_space=pltpu.SMEM)
    return pl.pallas_call(
        _dispatch_kernel,
        grid=(n // tm,),
        in_specs=[smem, smem, pl.BlockSpec((tm, HALF), lambda i: (i, 0)), pl.BlockSpec(memory_space=pl.ANY)],
        out_specs=pl.BlockSpec(memory_space=pl.ANY),
        out_shape=jax.ShapeDtypeStruct(xz.shape, u32),
        scratch_shapes=[pltpu.SemaphoreType.DMA(())],
        input_output_aliases={3: 0},
        compiler_params=_cparams(("arbitrary",)),
        name="dispatch",
    )(d1, d2, up, xz)


def _expert_kernel(be_ref, nu_ref, xs_ref, wg_ref, wu_ref, wd_ref, y_ref):
    j = pl.program_id(0)

    @pl.when(j < nu_ref[0])
    def _():
        xa, xb = _unpack_rows(xs_ref[...])
        xa = xa.astype(bf16)
        xb = xb.astype(bf16)
        g = _dot(xa, wg_ref[0, :HALF, :]) + _dot(xb, wg_ref[0, HALF:, :])
        u = _dot(xa, wu_ref[0, :HALF, :]) + _dot(xb, wu_ref[0, HALF:, :])
        hb = (g * jax.nn.sigmoid(g) * u).astype(bf16)
        y_ref[...] = _pack_rows(_dot(hb, wd_ref[0]))

    @pl.when(j >= nu_ref[0])
    def _():
        y_ref[...] = jnp.zeros_like(y_ref)


def _experts(block_e, nused, xs, wg, wu, wd, blk):
    nb = xs.shape[0] // blk
    return pl.pallas_call(
        _expert_kernel,
        grid_spec=pltpu.PrefetchScalarGridSpec(
            num_scalar_prefetch=2,
            grid=(nb,),
            in_specs=[pl.BlockSpec((blk, HALF), lambda j, be, nu: (j, 0)),
                      pl.BlockSpec((1, D_MODEL, D_EXPERT), lambda j, be, nu: (be[j], 0, 0)),
                      pl.BlockSpec((1, D_MODEL, D_EXPERT), lambda j, be, nu: (be[j], 0, 0)),
                      pl.BlockSpec((1, D_EXPERT, D_MODEL), lambda j, be, nu: (be[j], 0, 0))],
            out_specs=pl.BlockSpec((blk, HALF), lambda j, be, nu: (j, 0)),
        ),
        out_shape=jax.ShapeDtypeStruct(xs.shape, u32),
        compiler_params=_cparams(("arbitrary",)),
        name="experts",
    )(block_e, nused, xs, wg, wu, wd)


def _combine_kernel(d1_ref, d2_ref, h_ref, wts_ref, gfin_ref, y_ref, o_ref, y1_buf, y2_buf, sem):
    tm = h_ref.shape[0]

    def body(n, carry):
        pltpu.make_async_copy(y_ref.at[pl.ds(d1_ref[n], 1)], y1_buf.at[pl.ds(n, 1)], sem).start()
        pltpu.make_async_copy(y_ref.at[pl.ds(d2_ref[n], 1)], y2_buf.at[pl.ds(n, 1)], sem).start()
        return carry

    lax.fori_loop(0, tm, body, 0)
    w_rows = jnp.concatenate([wts_ref[...], jnp.zeros((LANES - SUBLANES, tm), f32)], axis=0)
    w_cols = jnp.transpose(w_rows)
    w1 = w_cols[:, 0:1]
    w2 = w_cols[:, 1:2]
    pltpu.make_async_copy(y_ref.at[pl.ds(0, tm)], y1_buf, sem).wait()
    pltpu.make_async_copy(y_ref.at[pl.ds(0, tm)], y2_buf, sem).wait()
    a1, b1 = _unpack_rows(y1_buf[...])
    a2, b2 = _unpack_rows(y2_buf[...])
    h = h_ref[...]
    ha = h[:, :HALF] + a1 * w1 + a2 * w2
    hb = h[:, HALF:] + b1 * w1 + b2 * w2
    ms = (jnp.sum(ha * ha, axis=-1, keepdims=True) + jnp.sum(hb * hb, axis=-1, keepdims=True)) * (1.0 / D_MODEL)
    inv = lax.rsqrt(ms + EPS)
    o_ref[:, :HALF] = ha * inv * gfin_ref[:, :HALF]
    o_ref[:, HALF:] = hb * inv * gfin_ref[:, HALF:]


def _combine(d1, d2, h, wts, g_final, ybuf, tm):
    n = h.shape[0]
    smem = pl.BlockSpec((tm,), lambda i: (i,), memory_space=pltpu.SMEM)
    return pl.pallas_call(
        _combine_kernel,
        grid=(n // tm,),
        in_specs=[smem, smem, pl.BlockSpec((tm, D_MODEL), lambda i: (i, 0)),
                  pl.BlockSpec((SUBLANES, tm), lambda i: (0, i)),
                  pl.BlockSpec(g_final.shape, lambda i: (0, 0)),
                  pl.BlockSpec(memory_space=pl.ANY)],
        out_specs=pl.BlockSpec((tm, D_MODEL), lambda i: (i, 0)),
        out_shape=jax.ShapeDtypeStruct((n, D_MODEL), f32),
        scratch_shapes=[pltpu.VMEM((tm, HALF), u32), pltpu.VMEM((tm, HALF), u32), pltpu.SemaphoreType.DMA(())],
        compiler_params=_cparams(("arbitrary",)),
        name="combine",
    )(d1, d2, h, wts, g_final, ybuf)


def _tile(n, pref):
    t = pref
    while n % t:
        t //= 2
    return t


def _prep_weights(g_mix, w_in, w_af2, b_af, w_ab2, b_ab, g_head, w_dw, b_dw, g_ln, b_ln, w_pw2, w_out,
                  g_ffn, w_rg, b_rg, w_re, b_re, w_e_gate, w_e_up, w_e_down, g_final):
    n_qkvo = 2 * GLA_KEY + 2 * GLA_VAL
    n_a = 2 * GATE_RANK
    w_in = w_in[0]
    zg = jnp.zeros((GATE_RANK, GLA_KEY), f32)
    wgate = jnp.concatenate([jnp.concatenate([w_af2[0], zg], axis=1), jnp.concatenate([zg, w_ab2[0]], axis=1)], axis=0)
    wdw = jnp.concatenate([w_dw[0], jnp.zeros((1, CONV_CH), f32)], axis=0)
    pad = jnp.zeros((SUBLANES - N_GROUPS, D_MODEL), f32)
    wr = jnp.concatenate([w_re[0].T, w_rg[0].T, pad], axis=0)
    br = jnp.concatenate([b_re[0], b_rg[0], jnp.zeros((SUBLANES - N_GROUPS,), f32)])[:, None]
    return dict(
        g_mix=g_mix[0][None, :],
        wqkvo=w_in[:, :n_qkvo].astype(bf16),
        wa=w_in[:, n_qkvo:n_qkvo + n_a].astype(bf16),
        wc=w_in[:, n_qkvo + n_a:].astype(bf16),
        wgate=wgate.astype(bf16),
        bgate=jnp.concatenate([b_af[0], b_ab[0]])[None, :],
        g_head=g_head[0][None, :],
        wdw=wdw, bdw=b_dw[0][None, :], gln=g_ln[0][None, :], bln=b_ln[0][None, :],
        wpw=w_pw2[0].astype(bf16),
        woa=w_out[0, :GLA_VAL].astype(bf16), wob=w_out[0, GLA_VAL:].astype(bf16),
        g_ffn=g_ffn[0][None, :], wr=wr, br=br,
        wg=w_e_gate[0].astype(bf16), wu=w_e_up[0].astype(bf16), wd=w_e_down[0].astype(bf16),
        g_final=g_final[None, :],
    )


def _trunk(x3, w, moe_block):
    n_seq, seq_len, _ = x3.shape
    n = n_seq * seq_len
    x = x3.reshape(n, D_MODEL)
    tm = _tile(n, 512)
    q, k, v, og, la, glu = _inproj(x, w["g_mix"], w["wqkvo"], w["wa"], w["wc"], w["wgate"], w["bgate"], tm)
    o_f, o_b = _gla(q, k, v, la, n_seq, seq_len)
    oc = _conv(glu, w["wdw"], w["bdw"], w["gln"], w["bln"], w["wpw"], seq_len, _tile(seq_len, 512))
    h, up, eid, wts = _mix(x, o_f, o_b, og, oc, w["g_head"], w["woa"], w["wob"], w["g_ffn"], w["wr"], w["br"], tm)

    nb = (n * TOP_K) // moe_block + N_EXPERTS
    nbp = -(-nb // LANES) * LANES
    dest, blkinfo = _rank(eid, moe_block, nbp, tm)
    d1, d2 = dest[1, 0], dest[1, 1]
    xs = _dispatch(d1, d2, up, jnp.zeros((nb * moe_block, HALF), u32), _tile(n, 512))
    ybuf = _experts(blkinfo[0], blkinfo[1, :1], xs, w["wg"], w["wu"], w["wd"], moe_block)
    y = _combine(d1, d2, h, wts, w["g_final"], ybuf, _tile(n, 256))
    return y.reshape(x3.shape)


def kernel(x_prompt, x_sample, g_mix, w_in, w_af2, b_af, w_ab2, b_ab, g_head, w_dw, b_dw, g_ln, b_ln, w_pw2,
           w_out, g_ffn, w_rg, b_rg, w_re, b_re, w_e_gate, w_e_up, w_e_down, g_final):
    w = _prep_weights(g_mix, w_in, w_af2, b_af, w_ab2, b_ab, g_head, w_dw, b_dw, g_ln, b_ln, w_pw2, w_out,
                      g_ffn, w_rg, b_rg, w_re, b_re, w_e_gate, w_e_up, w_e_down, g_final)
    moe_block = 512 if x_prompt.shape[0] * x_prompt.shape[1] >= 8192 else 128
    return (_trunk(x_prompt, w, moe_block), _trunk(x_sample, w, moe_block))
```

```python
import functools

import jax
import jax.numpy as jnp
from jax import lax
from jax.experimental import pallas as pl
from jax.experimental.pallas import tpu as pltpu

f32 = jnp.float32
bf16 = jnp.bfloat16
i32 = jnp.int32
u32 = jnp.uint32

D_MODEL = 1024
GLA_HEADS = 4
GLA_DK = 64
GLA_DV = 128
GLA_KEY = GLA_HEADS * GLA_DK
GLA_VAL = GLA_HEADS * GLA_DV
GATE_RANK = 16
GATE_NORMALIZER = 16
CONV_CH = 512
CONV_K = 31
N_GROUPS = 4
EXPERTS_PER_GROUP = 8
N_EXPERTS = N_GROUPS * EXPERTS_PER_GROUP
TOP_K = 2
D_EXPERT = 512
EPS = 1e-6

LANES = 128
SUBLANES = 8
GLA_CHUNK = 64
GLA_TILE = 256
PAIR_DK = 2 * GLA_DK
PAIR_DV = 2 * GLA_DV
CONV_HALO = 16
CONV_ROWS = 16
HALF = D_MODEL // 2
MOE_TILE = 512
MOE_BLOCK = 512
RUN_ALIGN = SUBLANES
SORT_ROWS = 256
VMEM_LIMIT = 56 * 1024 * 1024


def _cparams(sem, **kw):
    return pltpu.CompilerParams(dimension_semantics=sem, vmem_limit_bytes=VMEM_LIMIT, **kw)


def _dot(a, b):
    return jnp.dot(a, b, preferred_element_type=f32)


def _dot_nt(a, b):
    return lax.dot_general(a, b, (((1,), (1,)), ((), ())), preferred_element_type=f32)


def _dot_tn(a, b):
    return lax.dot_general(a, b, (((0,), (0,)), ((), ())), preferred_element_type=f32)


def _split_bf16(x):
    hi = x.astype(bf16)
    lo = (x - hi.astype(f32)).astype(bf16)
    return hi, lo


def _pack_rows(x):
    a = lax.bitcast_convert_type(x[:, :HALF].astype(bf16).astype(f32), u32)
    b = lax.bitcast_convert_type(x[:, HALF:].astype(bf16).astype(f32), u32)
    return a | (b >> 16)


def _unpack_rows(w):
    a = lax.bitcast_convert_type(w & jnp.uint32(0xFFFF0000), f32)
    b = lax.bitcast_convert_type(w << 16, f32)
    return a, b


def _rms(x):
    return x * lax.rsqrt(jnp.mean(x * x, axis=-1, keepdims=True) + EPS)


def _inproj_kernel(x_ref, g_ref, wqkvo_ref, wa_ref, wc_ref, wgate_ref, bgate_ref,
                   q_ref, k_ref, v_ref, og_ref, la_ref, glu_ref):
    u = (_rms(x_ref[...]) * g_ref[...]).astype(bf16)
    p = _dot(u, wqkvo_ref[...])
    q_ref[...] = (p[:, :GLA_KEY] * (GLA_DK ** -0.5)).astype(bf16)
    k_ref[...] = p[:, GLA_KEY:2 * GLA_KEY].astype(bf16)
    v_ref[...] = p[:, 2 * GLA_KEY:2 * GLA_KEY + GLA_VAL].astype(bf16)
    og_ref[...] = p[:, 2 * GLA_KEY + GLA_VAL:].astype(bf16)
    a = _dot(u, wa_ref[...])
    z = _dot(a.astype(bf16), wgate_ref[...]) + bgate_ref[...]
    la_ref[...] = (jnp.minimum(z, 0.0) - jnp.log1p(jnp.exp(-jnp.abs(z)))) * (1.0 / GATE_NORMALIZER)
    c = _dot(u, wc_ref[...])
    glu_ref[...] = (c[:, :CONV_CH] * jax.nn.sigmoid(c[:, CONV_CH:])).astype(bf16)


def _inproj(x, g_mix, wqkvo, wa, wc, wgate, bgate, tm):
    n = x.shape[0]
    row = lambda w: pl.BlockSpec((tm, w), lambda i: (i, 0))
    full = lambda a: pl.BlockSpec(a.shape, lambda i: (0, 0))
    return pl.pallas_call(
        _inproj_kernel,
        grid=(n // tm,),
        in_specs=[row(D_MODEL), full(g_mix), full(wqkvo), full(wa), full(wc), full(wgate), full(bgate)],
        out_specs=[row(GLA_KEY), row(GLA_KEY), row(GLA_VAL), row(GLA_VAL), row(2 * GLA_KEY), row(CONV_CH)],
        out_shape=[jax.ShapeDtypeStruct((n, GLA_KEY), bf16), jax.ShapeDtypeStruct((n, GLA_KEY), bf16),
                   jax.ShapeDtypeStruct((n, GLA_VAL), bf16), jax.ShapeDtypeStruct((n, GLA_VAL), bf16),
                   jax.ShapeDtypeStruct((n, 2 * GLA_KEY), f32), jax.ShapeDtypeStruct((n, CONV_CH), bf16)],
        compiler_params=_cparams(("parallel",)),
        name="inproj",
    )(x, g_mix, wqkvo, wa, wc, wgate, bgate)


def _gla_unit(q_ref, k_ref, v_ref, o_ref, st_ref, cs, la, pair, reverse, masks):
    head0_k, head0_v, keep = masks
    tl = cs.shape[0]
    kl = slice(pair * PAIR_DK, (pair + 1) * PAIR_DK)
    vl = slice(pair * PAIR_DV, (pair + 1) * PAIR_DV)
    n_chunks = tl // GLA_CHUNK
    order = list(range(n_chunks - 1, -1, -1) if reverse else range(n_chunks))

    def stack_heads(x, head0):
        z = jnp.zeros_like(x)
        return jnp.concatenate([jnp.where(head0, x, z), jnp.where(head0, z, x)], axis=0)

    local = {}
    for j in order:
        rows = slice(j * GLA_CHUNK, (j + 1) * GLA_CHUNK)
        cj = cs[rows, kl]
        q = q_ref[rows, kl].astype(f32)
        k = k_ref[rows, kl].astype(f32)
        c_last = cj[GLA_CHUNK - 1:GLA_CHUNK, :]
        mid = 0.5 * c_last
        if reverse:
            e = cj - la[rows, kl]
            qd, kd, qi, ku = q * jnp.exp(mid - e), k * jnp.exp(e - mid), q * jnp.exp(c_last - e), k * jnp.exp(e)
        else:
            qd, kd, qi, ku = q * jnp.exp(cj - mid), k * jnp.exp(mid - cj), q * jnp.exp(cj), k * jnp.exp(c_last - cj)
        vbd = stack_heads(v_ref[rows, vl], head0_v)
        sc = _dot_nt(qd.astype(bf16), stack_heads(kd.astype(bf16), head0_k))
        p = jnp.where(keep, sc, 0.0).astype(bf16)
        kv = _dot_tn(vbd, stack_heads(ku.astype(bf16), head0_k))
        local[j] = (p, vbd, qi.astype(bf16), kv, jnp.exp(c_last))

    st = st_ref[...]
    st_before = {}
    for j in order:
        st_before[j] = st.astype(bf16)
        st = local[j][4] * st + local[j][3]
    st_ref[...] = st
    for j in order:
        p, vbd, qi, _, _ = local[j]
        o = _dot(p, vbd) + _dot_nt(qi, st_before[j])
        o_ref[j * GLA_CHUNK:(j + 1) * GLA_CHUNK, vl] = o.astype(o_ref.dtype)


def _gla_kernel(qf, kf, vf, laf, qb, kb, vb, lab, of, ob, st_ref):
    @pl.when(pl.program_id(1) == 0)
    def _():
        st_ref[...] = jnp.zeros_like(st_ref)

    tl = qf.shape[0]
    r = lax.broadcasted_iota(i32, (tl, tl), 0)
    c = lax.broadcasted_iota(i32, (tl, tl), 1)
    tri = jnp.where((r // GLA_CHUNK == c // GLA_CHUNK) & (c <= r), 1.0, 0.0).astype(bf16)
    lane_k = lax.broadcasted_iota(i32, (GLA_CHUNK, PAIR_DK), 1)
    head0_k = lane_k < GLA_DK
    head0_v = lax.broadcasted_iota(i32, (GLA_CHUNK, PAIR_DV), 1) < GLA_DV
    t_idx = lax.broadcasted_iota(i32, (GLA_CHUNK, PAIR_DK), 0)
    s_idx = lane_k % GLA_CHUNK
    for d, (q_ref, k_ref, v_ref, la_ref, o_ref) in enumerate(((qf, kf, vf, laf, of), (qb, kb, vb, lab, ob))):
        reverse = d == 1
        la = la_ref[...]
        la_hi, la_lo = _split_bf16(la)
        cs = _dot(tri, la_hi) + _dot(tri, la_lo)
        keep = (s_idx > t_idx) if reverse else (s_idx <= t_idx)
        for pair in range(GLA_HEADS // 2):
            _gla_unit(q_ref, k_ref, v_ref, o_ref, st_ref.at[d, pair], cs, la, pair, reverse,
                      (head0_k, head0_v, keep))


def _gla(q, k, v, la, n_seq, seq_len):
    n = q.shape[0]
    nblk = seq_len // GLA_TILE
    fwd = lambda b, i: (b * nblk + i, 0)
    bwd = lambda b, i: (b * nblk + nblk - 1 - i, 0)
    bwd_la = lambda b, i: (b * nblk + nblk - 1 - i, 1)
    kq = lambda im: pl.BlockSpec((GLA_TILE, GLA_KEY), im)
    vv = lambda im: pl.BlockSpec((GLA_TILE, GLA_VAL), im)
    return pl.pallas_call(
        _gla_kernel,
        grid=(n_seq, nblk),
        in_specs=[kq(fwd), kq(fwd), vv(fwd), kq(fwd), kq(bwd), kq(bwd), vv(bwd), kq(bwd_la)],
        out_specs=[vv(fwd), vv(bwd)],
        out_shape=[jax.ShapeDtypeStruct((n, GLA_VAL), bf16), jax.ShapeDtypeStruct((n, GLA_VAL), bf16)],
        scratch_shapes=[pltpu.VMEM((2, GLA_HEADS // 2, PAIR_DV, PAIR_DK), f32)],
        compiler_params=_cparams(("parallel", "arbitrary")),
        name="gla",
    )(q, k, v, la, q, k, v, la)


def _conv_kernel(prev_ref, cur_ref, next_ref, wdw_ref, bdw_ref, gln_ref, bln_ref, wpw_ref,
                 o_ref, ext_ref, y_ref, *, tiles_per_seq):
    tl = cur_ref.shape[0]
    pos = pl.program_id(0) % tiles_per_seq
    prev = jnp.where(pos == 0, 0.0, prev_ref[...].astype(f32))
    nxt = jnp.where(pos == tiles_per_seq - 1, 0.0, next_ref[...].astype(f32))
    cur = cur_ref[...].astype(f32)
    n_slabs = CONV_CH // LANES
    for c in range(n_slabs):
        lanes = slice(c * LANES, (c + 1) * LANES)
        ext_ref[c, 0:CONV_HALO, :] = prev[:, lanes]
        ext_ref[c, CONV_HALO:CONV_HALO + tl, :] = cur[:, lanes]
        ext_ref[c, CONV_HALO + tl:, :] = nxt[:, lanes]
    off = CONV_HALO - CONV_K // 2

    def body(rb, carry):
        base = pl.multiple_of(rb * CONV_ROWS, CONV_ROWS)
        for c in range(n_slabs):
            lanes = slice(c * LANES, (c + 1) * LANES)
            a = jnp.zeros((CONV_ROWS, LANES), f32) + bdw_ref[:, lanes]
            for j in range(CONV_K):
                taps = ext_ref.at[c, pl.ds(off + j, tl)]
                a = a + taps[pl.ds(base, CONV_ROWS), :] * wdw_ref[j:j + 1, lanes]
            y_ref[pl.ds(base, CONV_ROWS), lanes] = a
        return carry

    lax.fori_loop(0, tl // CONV_ROWS, body, 0)
    acc = y_ref[...]
    mu = jnp.mean(acc, axis=-1, keepdims=True)
    xc = acc - mu
    yn = xc * lax.rsqrt(jnp.mean(xc * xc, axis=-1, keepdims=True) + EPS) * gln_ref[...] + bln_ref[...]
    o_ref[...] = _dot((yn * jax.nn.sigmoid(yn)).astype(bf16), wpw_ref[...]).astype(bf16)


def _conv(glu, wdw, bdw, gln, bln, wpw, seq_len, tl):
    n = glu.shape[0]
    hb = tl // CONV_HALO
    n_halo = n // CONV_HALO
    full = lambda a: pl.BlockSpec(a.shape, lambda i: (0, 0))
    return pl.pallas_call(
        functools.partial(_conv_kernel, tiles_per_seq=seq_len // tl),
        grid=(n // tl,),
        in_specs=[pl.BlockSpec((CONV_HALO, CONV_CH), lambda i: (jnp.maximum(i * hb - 1, 0), 0)),
                  pl.BlockSpec((tl, CONV_CH), lambda i: (i, 0)),
                  pl.BlockSpec((CONV_HALO, CONV_CH), lambda i: (jnp.minimum((i + 1) * hb, n_halo - 1), 0)),
                  full(wdw), full(bdw), full(gln), full(bln), full(wpw)],
        out_specs=pl.BlockSpec((tl, CONV_CH), lambda i: (i, 0)),
        out_shape=jax.ShapeDtypeStruct((n, CONV_CH), bf16),
        scratch_shapes=[pltpu.VMEM((CONV_CH // LANES, tl + 2 * CONV_HALO, LANES), f32),
                        pltpu.VMEM((tl, CONV_CH), f32)],
        compiler_params=_cparams(("parallel",)),
        name="conv",
    )(glu, glu, glu, wdw, bdw, gln, bln, wpw)


def _mix_kernel(x_ref, of_ref, ob_ref, og_ref, oc_ref, gh_ref, woa_ref, wob_ref, gf_ref, wr_ref, br_ref,
                h_ref, up_ref, eid_ref, wts_ref):
    tm = x_ref.shape[0]
    o = of_ref[...].astype(f32) + ob_ref[...].astype(f32)
    o = jnp.concatenate([_rms(o[:, h * GLA_DV:(h + 1) * GLA_DV]) for h in range(GLA_HEADS)], axis=1)
    og = og_ref[...].astype(f32)
    o = o * gh_ref[...] * (og * jax.nn.sigmoid(og))
    h = x_ref[...] + _dot(o.astype(bf16), woa_ref[...]) + _dot(oc_ref[...], wob_ref[...])
    h_ref[...] = h
    u = _rms(h) * gf_ref[...]
    u_hi, u_lo = _split_bf16(u)
    up_ref[...] = u_hi

    w_hi, w_lo = _split_bf16(wr_ref[...])
    logits = _dot_nt(w_hi, u_hi) + _dot_nt(w_lo, u_hi) + _dot_nt(w_hi, u_lo) + br_ref[...]
    le = logits[0:N_EXPERTS]
    lg = logits[N_EXPERTS:N_EXPERTS + N_GROUPS]
    gmax = jnp.max(lg, axis=0, keepdims=True)
    gi = lax.broadcasted_iota(i32, lg.shape, 0)
    gsel = jnp.min(jnp.where(lg == gmax, gi, N_GROUPS), axis=0, keepdims=True)
    gate = 1.0 / jnp.sum(jnp.exp(lg - gmax), axis=0, keepdims=True)
    ri = lax.broadcasted_iota(i32, le.shape, 0)
    lm = jnp.where(ri // EXPERTS_PER_GROUP == gsel, le, -jnp.inf)
    m1 = jnp.max(lm, axis=0, keepdims=True)
    i1 = jnp.min(jnp.where(lm == m1, ri, N_EXPERTS), axis=0, keepdims=True)
    lm2 = jnp.where(ri == i1, -jnp.inf, lm)
    m2 = jnp.max(lm2, axis=0, keepdims=True)
    i2 = jnp.min(jnp.where(lm2 == m2, ri, N_EXPERTS), axis=0, keepdims=True)
    t = jnp.exp(m2 - m1)
    den = 1.0 / (1.0 + t)
    r8 = lax.broadcasted_iota(i32, (SUBLANES, tm), 0)
    eid_ref[...] = jnp.where(r8 == 0, i1, jnp.where(r8 == 1, i2, 0))
    wts_ref[...] = jnp.where(r8 == 0, gate * den, jnp.where(r8 == 1, gate * t * den, 0.0))


def _mix(x, o_f, o_b, og, oc, g_head, woa, wob, g_ffn, wr, br, tm):
    n = x.shape[0]
    row = lambda w: pl.BlockSpec((tm, w), lambda i: (i, 0))
    col = pl.BlockSpec((SUBLANES, tm), lambda i: (0, i))
    full = lambda a: pl.BlockSpec(a.shape, lambda i: (0, 0))
    return pl.pallas_call(
        _mix_kernel,
        grid=(n // tm,),
        in_specs=[row(D_MODEL), row(GLA_VAL), row(GLA_VAL), row(GLA_VAL), row(CONV_CH),
                  full(g_head), full(woa), full(wob), full(g_ffn), full(wr), full(br)],
        out_specs=[row(D_MODEL), row(D_MODEL), col, col],
        out_shape=[jax.ShapeDtypeStruct((n, D_MODEL), f32), jax.ShapeDtypeStruct((n, D_MODEL), bf16),
                   jax.ShapeDtypeStruct((SUBLANES, n), i32), jax.ShapeDtypeStruct((SUBLANES, n), f32)],
        compiler_params=_cparams(("parallel",)),
        name="mix",
    )(x, o_f, o_b, og, oc, g_head, woa, wob, g_ffn, wr, br)


def _rank_kernel(eid_ref, pos_ref, tbl_ref, blk_ref, cnt_ref, base_ref, *, blk, nbp):
    ps = pl.program_id(0)
    i = pl.program_id(1)
    tm = eid_ref.shape[1]
    ri = lax.broadcasted_iota(i32, (N_EXPERTS, tm), 0)
    oh1 = ri == eid_ref[0:1, :]
    oh2 = ri == eid_ref[1:2, :]
    ohf = jnp.where(oh1 | oh2, 1.0, 0.0)
    tile_cnt = jnp.sum(ohf, axis=1, keepdims=True)
    run_len = jnp.floor((tile_cnt + (RUN_ALIGN - 1)) * (1.0 / RUN_ALIGN)) * RUN_ALIGN + jnp.zeros((N_EXPERTS, LANES), f32)
    rr = lax.broadcasted_iota(i32, (N_EXPERTS, LANES), 0)

    def cumsum_experts(x):
        for s in (1, 2, 4, 8, 16):
            x = x + jnp.where(rr >= s, pltpu.roll(x, s, axis=0), 0.0)
        return x

    @pl.when((ps == 0) & (i == 0))
    def _():
        cnt_ref[...] = jnp.zeros_like(cnt_ref)

    @pl.when(ps == 0)
    def _():
        cnt_ref[...] += run_len
        pos_ref[...] = jnp.zeros_like(pos_ref)
        tbl_ref[...] = jnp.zeros_like(tbl_ref)

    @pl.when((ps == 1) & (i == 0))
    def _():
        cnt = cnt_ref[...]
        pc = jnp.floor((cnt + (blk - 1)) * (1.0 / blk)) * blk
        inc = cumsum_experts(pc)
        base_ref[...] = inc - pc
        pend = jnp.concatenate([inc] * (nbp // LANES), axis=1)
        jl = lax.broadcasted_iota(i32, (N_EXPERTS, nbp), 1).astype(f32) * blk
        be = jnp.minimum(jnp.sum(jnp.where(pend <= jl, 1.0, 0.0), axis=0, keepdims=True), N_EXPERTS - 1.0)
        nused = jnp.concatenate([inc[N_EXPERTS - 1:N_EXPERTS, :]] * (nbp // LANES), axis=1) * (1.0 / blk)
        r8 = lax.broadcasted_iota(i32, (SUBLANES, nbp), 0)
        blk_ref[...] = jnp.where(r8 == 0, be, jnp.where(r8 == 1, nused, 0.0)).astype(i32)

    @pl.when(ps == 1)
    def _():
        a = lax.broadcasted_iota(i32, (tm, tm), 0)
        b = lax.broadcasted_iota(i32, (tm, tm), 1)
        upper = jnp.where(a < b, 1.0, 0.0).astype(bf16)
        before = _dot(ohf.astype(bf16), upper)
        run_start = cumsum_experts(run_len) - run_len
        row = run_start[:, 0:1] + before
        p1 = jnp.sum(jnp.where(oh1, row, 0.0), axis=0, keepdims=True)
        p2 = jnp.sum(jnp.where(oh2, row, 0.0), axis=0, keepdims=True)
        r8 = lax.broadcasted_iota(i32, (SUBLANES, tm), 0)
        pos_ref[0] = jnp.where(r8 == 0, p1, jnp.where(r8 == 1, p2, 0.0)).astype(i32)
        diag = lax.broadcasted_iota(i32, (N_EXPERTS, LANES), 1) == rr
        on_lanes = lambda x: jnp.sum(jnp.where(diag, x, 0.0), axis=0, keepdims=True)
        chunks = run_len * (1.0 / RUN_ALIGN)
        rows = (on_lanes(run_start), on_lanes(base_ref[...]), on_lanes(chunks), jnp.sum(chunks, axis=0, keepdims=True))
        t8 = lax.broadcasted_iota(i32, (SUBLANES, LANES), 0)
        tbl = jnp.zeros((SUBLANES, LANES), f32)
        for t, x in enumerate(rows):
            tbl = jnp.where(t8 == t, x, tbl)
        tbl_ref[0, 0] = tbl.astype(i32)
        base_ref[...] += run_len


def _rank(eid, blk, nbp, tm):
    n = eid.shape[1]
    return pl.pallas_call(
        functools.partial(_rank_kernel, blk=blk, nbp=nbp),
        grid=(2, n // tm),
        in_specs=[pl.BlockSpec((SUBLANES, tm), lambda p, i: (0, i))],
        out_specs=[pl.BlockSpec((1, SUBLANES, tm), lambda p, i: (p, 0, i)),
                   pl.BlockSpec((1, 1, SUBLANES, LANES), lambda p, i: (p, i, 0, 0)),
                   pl.BlockSpec((SUBLANES, nbp), lambda p, i: (0, 0))],
        out_shape=[jax.ShapeDtypeStruct((2, SUBLANES, n), i32),
                   jax.ShapeDtypeStruct((2, n // tm, SUBLANES, LANES), i32),
                   jax.ShapeDtypeStruct((SUBLANES, nbp), i32)],
        scratch_shapes=[pltpu.VMEM((N_EXPERTS, LANES), f32), pltpu.VMEM((N_EXPERTS, LANES), f32)],
        compiler_params=_cparams(("arbitrary", "arbitrary")),
        name="rank",
    )(eid)


def _for_each_chunk(tbl_ref, fn):
    def per_expert(e, carry):
        sorted0 = tbl_ref[0, e]
        slot0 = tbl_ref[1, e]

        def per_chunk(c, carry2):
            off = c * RUN_ALIGN
            fn(pl.multiple_of(sorted0 + off, RUN_ALIGN), pl.multiple_of(slot0 + off, RUN_ALIGN))
            return carry2

        lax.fori_loop(0, tbl_ref[2, e], per_chunk, 0)
        return carry

    lax.fori_loop(0, N_EXPERTS, per_expert, 0)


def _wait_chunks(tbl_ref, copy):
    def body(c, carry):
        copy(0, 0).wait()
        return carry

    lax.fori_loop(0, tbl_ref[3, 0], body, 0)


def _dispatch_kernel(tbl_ref, pos_ref, u_ref, xz_ref, xs_ref, sorted_ref, sem):
    del xz_ref
    tm = u_ref.shape[0]
    p1 = pos_ref[0:1, :]
    p2 = pos_ref[1:2, :]
    u = u_ref[...]
    for r0 in range(0, sorted_ref.shape[0], SORT_ROWS):
        ri = lax.broadcasted_iota(i32, (SORT_ROWS, tm), 0) + r0
        sel = jnp.where((ri == p1) | (ri == p2), 1.0, 0.0).astype(bf16)
        sorted_ref[r0:r0 + SORT_ROWS, :] = _pack_rows(_dot(sel, u))

    def copy(sorted_row, slot_row):
        return pltpu.make_async_copy(sorted_ref.at[pl.ds(sorted_row, RUN_ALIGN)],
                                     xs_ref.at[pl.ds(slot_row, RUN_ALIGN)], sem)

    _for_each_chunk(tbl_ref, lambda s, d: copy(s, d).start())
    _wait_chunks(tbl_ref, copy)


def _dispatch(tbl, pos, up, xz, tm, rmax):
    n = up.shape[0]
    return pl.pallas_call(
        _dispatch_kernel,
        grid=(n // tm,),
        in_specs=[pl.BlockSpec((SUBLANES, LANES), lambda i: (i, 0), memory_space=pltpu.SMEM),
                  pl.BlockSpec((SUBLANES, tm), lambda i: (0, i)),
                  pl.BlockSpec((tm, D_MODEL), lambda i: (i, 0)),
                  pl.BlockSpec(memory_space=pl.ANY)],
        out_specs=pl.BlockSpec(memory_space=pl.ANY),
        out_shape=jax.ShapeDtypeStruct(xz.shape, u32),
        scratch_shapes=[pltpu.VMEM((rmax, HALF), u32), pltpu.SemaphoreType.DMA(())],
        input_output_aliases={3: 0},
        compiler_params=_cparams(("arbitrary",)),
        name="dispatch",
    )(tbl, pos, up, xz)


def _expert_kernel(be_ref, nu_ref, xs_ref, wg_ref, wu_ref, wd_ref, y_ref):
    j = pl.program_id(0)

    @pl.when(j < nu_ref[0])
    def _():
        xa, xb = _unpack_rows(xs_ref[...])
        xa = xa.astype(bf16)
        xb = xb.astype(bf16)
        g = _dot(xa, wg_ref[0, :HALF, :]) + _dot(xb, wg_ref[0, HALF:, :])
        u = _dot(xa, wu_ref[0, :HALF, :]) + _dot(xb, wu_ref[0, HALF:, :])
        hb = (g * jax.nn.sigmoid(g) * u).astype(bf16)
        y_ref[...] = _pack_rows(_dot(hb, wd_ref[0]))

    @pl.when(j >= nu_ref[0])
    def _():
        y_ref[...] = jnp.zeros_like(y_ref)


def _experts(block_e, nused, xs, wg, wu, wd, blk):
    nb = xs.shape[0] // blk
    return pl.pallas_call(
        _expert_kernel,
        grid_spec=pltpu.PrefetchScalarGridSpec(
            num_scalar_prefetch=2,
            grid=(nb,),
            in_specs=[pl.BlockSpec((blk, HALF), lambda j, be, nu: (j, 0)),
                      pl.BlockSpec((1, D_MODEL, D_EXPERT), lambda j, be, nu: (be[j], 0, 0)),
                      pl.BlockSpec((1, D_MODEL, D_EXPERT), lambda j, be, nu: (be[j], 0, 0)),
                      pl.BlockSpec((1, D_EXPERT, D_MODEL), lambda j, be, nu: (be[j], 0, 0))],
            out_specs=pl.BlockSpec((blk, HALF), lambda j, be, nu: (j, 0)),
        ),
        out_shape=jax.ShapeDtypeStruct(xs.shape, u32),
        compiler_params=_cparams(("arbitrary",)),
        name="experts",
    )(block_e, nused, xs, wg, wu, wd)


def _combine_kernel(tbl_ref, pos_ref, wts_ref, h_ref, gfin_ref, y_ref, o_ref, ys_ref, sem):
    tm = h_ref.shape[0]

    @pl.when(pl.program_id(0) == 0)
    def _():
        ys_ref[...] = jnp.zeros_like(ys_ref)

    def copy(sorted_row, slot_row):
        return pltpu.make_async_copy(y_ref.at[pl.ds(slot_row, RUN_ALIGN)],
                                     ys_ref.at[pl.ds(sorted_row, RUN_ALIGN)], sem)

    _for_each_chunk(tbl_ref, lambda s, d: copy(s, d).start())
    pad = jnp.zeros((LANES - SUBLANES, tm), f32)
    pos_cols = jnp.transpose(jnp.concatenate([pos_ref[...].astype(f32), pad], axis=0))
    w_cols = jnp.transpose(jnp.concatenate([wts_ref[...], pad], axis=0))
    p1, p2 = pos_cols[:, 0:1], pos_cols[:, 1:2]
    w1, w2 = w_cols[:, 0:1], w_cols[:, 1:2]
    h = h_ref[...]
    ha = h[:, :HALF]
    hb = h[:, HALF:]
    _wait_chunks(tbl_ref, copy)
    for r0 in range(0, ys_ref.shape[0], SORT_ROWS):
        li = (lax.broadcasted_iota(i32, (tm, SORT_ROWS), 1) + r0).astype(f32)
        sel = (jnp.where(li == p1, w1, 0.0) + jnp.where(li == p2, w2, 0.0)).astype(bf16)
        ya, yb = _unpack_rows(ys_ref[r0:r0 + SORT_ROWS, :])
        ha = ha + _dot(sel, ya.astype(bf16))
        hb = hb + _dot(sel, yb.astype(bf16))
    ms = (jnp.sum(ha * ha, axis=-1, keepdims=True) + jnp.sum(hb * hb, axis=-1, keepdims=True)) * (1.0 / D_MODEL)
    inv = lax.rsqrt(ms + EPS)
    o_ref[:, :HALF] = ha * inv * gfin_ref[:, :HALF]
    o_ref[:, HALF:] = hb * inv * gfin_ref[:, HALF:]


def _combine(tbl, pos, wts, h, g_final, ybuf, tm, rmax):
    n = h.shape[0]
    col = pl.BlockSpec((SUBLANES, tm), lambda i: (0, i))
    return pl.pallas_call(
        _combine_kernel,
        grid=(n // tm,),
        in_specs=[pl.BlockSpec((SUBLANES, LANES), lambda i: (i, 0), memory_space=pltpu.SMEM),
                  col, col,
                  pl.BlockSpec((tm, D_MODEL), lambda i: (i, 0)),
                  pl.BlockSpec(g_final.shape, lambda i: (0, 0)),
                  pl.BlockSpec(memory_space=pl.ANY)],
        out_specs=pl.BlockSpec((tm, D_MODEL), lambda i: (i, 0)),
        out_shape=jax.ShapeDtypeStruct((n, D_MODEL), f32),
        scratch_shapes=[pltpu.VMEM((rmax, HALF), u32), pltpu.SemaphoreType.DMA(())],
        compiler_params=_cparams(("arbitrary",)),
        name="combine",
    )(tbl, pos, wts, h, g_final, ybuf)


def _tile(n, pref):
    t = pref
    while n % t:
        t //= 2
    return t


def _prep_weights(g_mix, w_in, w_af2, b_af, w_ab2, b_ab, g_head, w_dw, b_dw, g_ln, b_ln, w_pw2, w_out,
                  g_ffn, w_rg, b_rg, w_re, b_re, w_e_gate, w_e_up, w_e_down, g_final):
    n_qkvo = 2 * GLA_KEY + 2 * GLA_VAL
    n_a = 2 * GATE_RANK
    w_in = w_in[0]
    zg = jnp.zeros((GATE_RANK, GLA_KEY), f32)
    wgate = jnp.concatenate([jnp.concatenate([w_af2[0], zg], axis=1), jnp.concatenate([zg, w_ab2[0]], axis=1)], axis=0)
    wdw = jnp.concatenate([w_dw[0], jnp.zeros((1, CONV_CH), f32)], axis=0)
    pad = jnp.zeros((SUBLANES - N_GROUPS, D_MODEL), f32)
    wr = jnp.concatenate([w_re[0].T, w_rg[0].T, pad], axis=0)
    br = jnp.concatenate([b_re[0], b_rg[0], jnp.zeros((SUBLANES - N_GROUPS,), f32)])[:, None]
    return dict(
        g_mix=g_mix[0][None, :],
        wqkvo=w_in[:, :n_qkvo].astype(bf16),
        wa=w_in[:, n_qkvo:n_qkvo + n_a].astype(bf16),
        wc=w_in[:, n_qkvo + n_a:].astype(bf16),
        wgate=wgate.astype(bf16),
        bgate=jnp.concatenate([b_af[0], b_ab[0]])[None, :],
        g_head=g_head[0][None, :],
        wdw=wdw, bdw=b_dw[0][None, :], gln=g_ln[0][None, :], bln=b_ln[0][None, :],
        wpw=w_pw2[0].astype(bf16),
        woa=w_out[0, :GLA_VAL].astype(bf16), wob=w_out[0, GLA_VAL:].astype(bf16),
        g_ffn=g_ffn[0][None, :], wr=wr, br=br,
        wg=w_e_gate[0].astype(bf16), wu=w_e_up[0].astype(bf16), wd=w_e_down[0].astype(bf16),
        g_final=g_final[None, :],
    )


def _trunk(x3, w):
    n_seq, seq_len, _ = x3.shape
    n = n_seq * seq_len
    x = x3.reshape(n, D_MODEL)
    tm = _tile(n, MOE_TILE)
    q, k, v, og, la, glu = _inproj(x, w["g_mix"], w["wqkvo"], w["wa"], w["wc"], w["wgate"], w["bgate"], tm)
    o_f, o_b = _gla(q, k, v, la, n_seq, seq_len)
    oc = _conv(glu, w["wdw"], w["bdw"], w["gln"], w["bln"], w["wpw"], seq_len, _tile(seq_len, 512))
    h, up, eid, wts = _mix(x, o_f, o_b, og, oc, w["g_head"], w["woa"], w["wob"], w["g_ffn"], w["wr"], w["br"], tm)

    n_tiles = n // tm
    max_slots = n * TOP_K + n_tiles * N_EXPERTS * (RUN_ALIGN - 1)
    nb = -(-max_slots // MOE_BLOCK) + N_EXPERTS
    nbp = -(-nb // LANES) * LANES
    rmax = -(-(tm * TOP_K + N_EXPERTS * (RUN_ALIGN - 1)) // SORT_ROWS) * SORT_ROWS
    pos, tbl, blkinfo = _rank(eid, MOE_BLOCK, nbp, tm)
    pos = pos[1]
    tbl = tbl[1].reshape(n_tiles * SUBLANES, LANES)
    xs = _dispatch(tbl, pos, up, jnp.zeros((nb * MOE_BLOCK, HALF), u32), tm, rmax)
    ybuf = _experts(blkinfo[0], blkinfo[1, :1], xs, w["wg"], w["wu"], w["wd"], MOE_BLOCK)
    y = _combine(tbl, pos, wts, h, w["g_final"], ybuf, tm, rmax)
    return y.reshape(x3.shape)


def kernel(x_prompt, x_sample, g_mix, w_in, w_af2, b_af, w_ab2, b_ab, g_head, w_dw, b_dw, g_ln, b_ln, w_pw2,
           w_out, g_ffn, w_rg, b_rg, w_re, b_re, w_e_gate, w_e_up, w_e_down, g_final):
    w = _prep_weights(g_mix, w_in, w_af2, b_af, w_ab2, b_ab, g_head, w_dw, b_dw, g_ln, b_ln, w_pw2, w_out,
                      g_ffn, w_rg, b_rg, w_re, b_re, w_e_gate, w_e_up, w_e_down, g_final)
    return (_trunk(x_prompt, w), _trunk(x_sample, w))
```

```python
import functools

import jax
import jax.numpy as jnp
from jax import lax
from jax.experimental import pallas as pl
from jax.experimental.pallas import tpu as pltpu

f32 = jnp.float32
bf16 = jnp.bfloat16
i32 = jnp.int32
u32 = jnp.uint32

D_MODEL = 1024
GLA_HEADS = 4
GLA_DK = 64
GLA_DV = 128
GLA_KEY = GLA_HEADS * GLA_DK
GLA_VAL = GLA_HEADS * GLA_DV
GATE_RANK = 16
GATE_NORMALIZER = 16
CONV_CH = 512
CONV_K = 31
N_GROUPS = 4
EXPERTS_PER_GROUP = 8
N_EXPERTS = N_GROUPS * EXPERTS_PER_GROUP
TOP_K = 2
D_EXPERT = 512
EPS = 1e-6

LANES = 128
SUBLANES = 8
GLA_CHUNK = 64
GLA_TILE = 256
PAIR_DK = 2 * GLA_DK
PAIR_DV = 2 * GLA_DV
CONV_HALO = 16
CONV_ROWS = 16
HALF = D_MODEL // 2
MOE_TILE = 512
MOE_BLOCK = 512
RUN_ALIGN = SUBLANES
SORT_ROWS = 256
RANK_GROUP = 8
VMEM_LIMIT = 56 * 1024 * 1024


def _cparams(sem, **kw):
    return pltpu.CompilerParams(dimension_semantics=sem, vmem_limit_bytes=VMEM_LIMIT, **kw)


def _dot(a, b):
    return jnp.dot(a, b, preferred_element_type=f32)


def _dot_nt(a, b):
    return lax.dot_general(a, b, (((1,), (1,)), ((), ())), preferred_element_type=f32)


def _dot_tn(a, b):
    return lax.dot_general(a, b, (((0,), (0,)), ((), ())), preferred_element_type=f32)


def _split_bf16(x):
    hi = x.astype(bf16)
    lo = (x - hi.astype(f32)).astype(bf16)
    return hi, lo


def _pack_rows(x):
    a = lax.bitcast_convert_type(x[:, :HALF].astype(bf16).astype(f32), u32)
    b = lax.bitcast_convert_type(x[:, HALF:].astype(bf16).astype(f32), u32)
    return a | (b >> 16)


def _unpack_rows(w):
    a = lax.bitcast_convert_type(w & jnp.uint32(0xFFFF0000), f32)
    b = lax.bitcast_convert_type(w << 16, f32)
    return a, b


def _rms(x):
    return x * lax.rsqrt(jnp.mean(x * x, axis=-1, keepdims=True) + EPS)


def _inproj_kernel(x_ref, g_ref, wqkvo_ref, wa_ref, wc_ref, wgate_ref, bgate_ref,
                   q_ref, k_ref, v_ref, og_ref, la_ref, glu_ref):
    u = (_rms(x_ref[...]) * g_ref[...]).astype(bf16)
    p = _dot(u, wqkvo_ref[...])
    q_ref[...] = (p[:, :GLA_KEY] * (GLA_DK ** -0.5)).astype(bf16)
    k_ref[...] = p[:, GLA_KEY:2 * GLA_KEY].astype(bf16)
    v_ref[...] = p[:, 2 * GLA_KEY:2 * GLA_KEY + GLA_VAL].astype(bf16)
    og_ref[...] = p[:, 2 * GLA_KEY + GLA_VAL:].astype(bf16)
    a = _dot(u, wa_ref[...])
    z = _dot(a.astype(bf16), wgate_ref[...]) + bgate_ref[...]
    la_ref[...] = (jnp.minimum(z, 0.0) - jnp.log1p(jnp.exp(-jnp.abs(z)))) * (1.0 / GATE_NORMALIZER)
    c = _dot(u, wc_ref[...])
    glu_ref[...] = (c[:, :CONV_CH] * jax.nn.sigmoid(c[:, CONV_CH:])).astype(bf16)


def _inproj(x, g_mix, wqkvo, wa, wc, wgate, bgate, tm):
    n = x.shape[0]
    row = lambda w: pl.BlockSpec((tm, w), lambda i: (i, 0))
    full = lambda a: pl.BlockSpec(a.shape, lambda i: (0, 0))
    return pl.pallas_call(
        _inproj_kernel,
        grid=(n // tm,),
        in_specs=[row(D_MODEL), full(g_mix), full(wqkvo), full(wa), full(wc), full(wgate), full(bgate)],
        out_specs=[row(GLA_KEY), row(GLA_KEY), row(GLA_VAL), row(GLA_VAL), row(2 * GLA_KEY), row(CONV_CH)],
        out_shape=[jax.ShapeDtypeStruct((n, GLA_KEY), bf16), jax.ShapeDtypeStruct((n, GLA_KEY), bf16),
                   jax.ShapeDtypeStruct((n, GLA_VAL), bf16), jax.ShapeDtypeStruct((n, GLA_VAL), bf16),
                   jax.ShapeDtypeStruct((n, 2 * GLA_KEY), f32), jax.ShapeDtypeStruct((n, CONV_CH), bf16)],
        compiler_params=_cparams(("parallel",)),
        name="inproj",
    )(x, g_mix, wqkvo, wa, wc, wgate, bgate)


def _gla_unit(q_ref, k_ref, v_ref, o_ref, st_ref, cs, la, pair, reverse, masks):
    head0_k, head0_v, keep = masks
    tl = cs.shape[0]
    kl = slice(pair * PAIR_DK, (pair + 1) * PAIR_DK)
    vl = slice(pair * PAIR_DV, (pair + 1) * PAIR_DV)
    n_chunks = tl // GLA_CHUNK
    order = list(range(n_chunks - 1, -1, -1) if reverse else range(n_chunks))

    def stack_heads(x, head0):
        z = jnp.zeros_like(x)
        return jnp.concatenate([jnp.where(head0, x, z), jnp.where(head0, z, x)], axis=0)

    local = {}
    for j in order:
        rows = slice(j * GLA_CHUNK, (j + 1) * GLA_CHUNK)
        cj = cs[rows, kl]
        q = q_ref[rows, kl].astype(f32)
        k = k_ref[rows, kl].astype(f32)
        c_last = cj[GLA_CHUNK - 1:GLA_CHUNK, :]
        mid = 0.5 * c_last
        if reverse:
            e = cj - la[rows, kl]
            qd, kd, qi, ku = q * jnp.exp(mid - e), k * jnp.exp(e - mid), q * jnp.exp(c_last - e), k * jnp.exp(e)
        else:
            qd, kd, qi, ku = q * jnp.exp(cj - mid), k * jnp.exp(mid - cj), q * jnp.exp(cj), k * jnp.exp(c_last - cj)
        vbd = stack_heads(v_ref[rows, vl], head0_v)
        sc = _dot_nt(qd.astype(bf16), stack_heads(kd.astype(bf16), head0_k))
        p = jnp.where(keep, sc, 0.0).astype(bf16)
        kv = _dot_tn(vbd, stack_heads(ku.astype(bf16), head0_k))
        local[j] = (p, vbd, qi.astype(bf16), kv, jnp.exp(c_last))

    st = st_ref[...]
    st_before = {}
    for j in order:
        st_before[j] = st.astype(bf16)
        st = local[j][4] * st + local[j][3]
    st_ref[...] = st
    for j in order:
        p, vbd, qi, _, _ = local[j]
        o = _dot(p, vbd) + _dot_nt(qi, st_before[j])
        o_ref[j * GLA_CHUNK:(j + 1) * GLA_CHUNK, vl] = o.astype(o_ref.dtype)


def _gla_kernel(qf, kf, vf, laf, qb, kb, vb, lab, of, ob, st_ref):
    @pl.when(pl.program_id(1) == 0)
    def _():
        st_ref[...] = jnp.zeros_like(st_ref)

    tl = qf.shape[0]
    r = lax.broadcasted_iota(i32, (tl, tl), 0)
    c = lax.broadcasted_iota(i32, (tl, tl), 1)
    tri = jnp.where((r // GLA_CHUNK == c // GLA_CHUNK) & (c <= r), 1.0, 0.0).astype(bf16)
    lane_k = lax.broadcasted_iota(i32, (GLA_CHUNK, PAIR_DK), 1)
    head0_k = lane_k < GLA_DK
    head0_v = lax.broadcasted_iota(i32, (GLA_CHUNK, PAIR_DV), 1) < GLA_DV
    t_idx = lax.broadcasted_iota(i32, (GLA_CHUNK, PAIR_DK), 0)
    s_idx = lane_k % GLA_CHUNK
    for d, (q_ref, k_ref, v_ref, la_ref, o_ref) in enumerate(((qf, kf, vf, laf, of), (qb, kb, vb, lab, ob))):
        reverse = d == 1
        la = la_ref[...]
        la_hi, la_lo = _split_bf16(la)
        cs = _dot(tri, la_hi) + _dot(tri, la_lo)
        keep = (s_idx > t_idx) if reverse else (s_idx <= t_idx)
        for pair in range(GLA_HEADS // 2):
            _gla_unit(q_ref, k_ref, v_ref, o_ref, st_ref.at[d, pair], cs, la, pair, reverse,
                      (head0_k, head0_v, keep))


def _gla(q, k, v, la, n_seq, seq_len):
    n = q.shape[0]
    nblk = seq_len // GLA_TILE
    fwd = lambda b, i: (b * nblk + i, 0)
    bwd = lambda b, i: (b * nblk + nblk - 1 - i, 0)
    bwd_la = lambda b, i: (b * nblk + nblk - 1 - i, 1)
    kq = lambda im: pl.BlockSpec((GLA_TILE, GLA_KEY), im)
    vv = lambda im: pl.BlockSpec((GLA_TILE, GLA_VAL), im)
    return pl.pallas_call(
        _gla_kernel,
        grid=(n_seq, nblk),
        in_specs=[kq(fwd), kq(fwd), vv(fwd), kq(fwd), kq(bwd), kq(bwd), vv(bwd), kq(bwd_la)],
        out_specs=[vv(fwd), vv(bwd)],
        out_shape=[jax.ShapeDtypeStruct((n, GLA_VAL), bf16), jax.ShapeDtypeStruct((n, GLA_VAL), bf16)],
        scratch_shapes=[pltpu.VMEM((2, GLA_HEADS // 2, PAIR_DV, PAIR_DK), f32)],
        compiler_params=_cparams(("parallel", "arbitrary")),
        name="gla",
    )(q, k, v, la, q, k, v, la)


def _conv_kernel(prev_ref, cur_ref, next_ref, wdw_ref, bdw_ref, gln_ref, bln_ref, wpw_ref,
                 o_ref, ext_ref, y_ref, *, tiles_per_seq):
    tl = cur_ref.shape[0]
    pos = pl.program_id(0) % tiles_per_seq
    prev = jnp.where(pos == 0, 0.0, prev_ref[...].astype(f32))
    nxt = jnp.where(pos == tiles_per_seq - 1, 0.0, next_ref[...].astype(f32))
    cur = cur_ref[...].astype(f32)
    n_slabs = CONV_CH // LANES
    for c in range(n_slabs):
        lanes = slice(c * LANES, (c + 1) * LANES)
        ext_ref[c, 0:CONV_HALO, :] = prev[:, lanes]
        ext_ref[c, CONV_HALO:CONV_HALO + tl, :] = cur[:, lanes]
        ext_ref[c, CONV_HALO + tl:, :] = nxt[:, lanes]
    off = CONV_HALO - CONV_K // 2

    def body(rb, carry):
        base = pl.multiple_of(rb * CONV_ROWS, CONV_ROWS)
        for c in range(n_slabs):
            lanes = slice(c * LANES, (c + 1) * LANES)
            a = jnp.zeros((CONV_ROWS, LANES), f32) + bdw_ref[:, lanes]
            for j in range(CONV_K):
                taps = ext_ref.at[c, pl.ds(off + j, tl)]
                a = a + taps[pl.ds(base, CONV_ROWS), :] * wdw_ref[j:j + 1, lanes]
            y_ref[pl.ds(base, CONV_ROWS), lanes] = a
        return carry

    lax.fori_loop(0, tl // CONV_ROWS, body, 0)
    acc = y_ref[...]
    mu = jnp.mean(acc, axis=-1, keepdims=True)
    xc = acc - mu
    yn = xc * lax.rsqrt(jnp.mean(xc * xc, axis=-1, keepdims=True) + EPS) * gln_ref[...] + bln_ref[...]
    o_ref[...] = _dot((yn * jax.nn.sigmoid(yn)).astype(bf16), wpw_ref[...]).astype(bf16)


def _conv(glu, wdw, bdw, gln, bln, wpw, seq_len, tl):
    n = glu.shape[0]
    hb = tl // CONV_HALO
    n_halo = n // CONV_HALO
    full = lambda a: pl.BlockSpec(a.shape, lambda i: (0, 0))
    return pl.pallas_call(
        functools.partial(_conv_kernel, tiles_per_seq=seq_len // tl),
        grid=(n // tl,),
        in_specs=[pl.BlockSpec((CONV_HALO, CONV_CH), lambda i: (jnp.maximum(i * hb - 1, 0), 0)),
                  pl.BlockSpec((tl, CONV_CH), lambda i: (i, 0)),
                  pl.BlockSpec((CONV_HALO, CONV_CH), lambda i: (jnp.minimum((i + 1) * hb, n_halo - 1), 0)),
                  full(wdw), full(bdw), full(gln), full(bln), full(wpw)],
        out_specs=pl.BlockSpec((tl, CONV_CH), lambda i: (i, 0)),
        out_shape=jax.ShapeDtypeStruct((n, CONV_CH), bf16),
        scratch_shapes=[pltpu.VMEM((CONV_CH // LANES, tl + 2 * CONV_HALO, LANES), f32),
                        pltpu.VMEM((tl, CONV_CH), f32)],
        compiler_params=_cparams(("parallel",)),
        name="conv",
    )(glu, glu, glu, wdw, bdw, gln, bln, wpw)


def _mix_kernel(x_ref, of_ref, ob_ref, og_ref, oc_ref, gh_ref, woa_ref, wob_ref, gf_ref, wr_ref, br_ref,
                h_ref, up_ref, eid_ref, wts_ref):
    tm = x_ref.shape[0]
    o = of_ref[...].astype(f32) + ob_ref[...].astype(f32)
    o = jnp.concatenate([_rms(o[:, h * GLA_DV:(h + 1) * GLA_DV]) for h in range(GLA_HEADS)], axis=1)
    og = og_ref[...].astype(f32)
    o = o * gh_ref[...] * (og * jax.nn.sigmoid(og))
    h = x_ref[...] + _dot(o.astype(bf16), woa_ref[...]) + _dot(oc_ref[...], wob_ref[...])
    h_ref[...] = h
    u = _rms(h) * gf_ref[...]
    u_hi, u_lo = _split_bf16(u)
    up_ref[...] = u_hi

    w_hi, w_lo = _split_bf16(wr_ref[...])
    logits = _dot_nt(w_hi, u_hi) + _dot_nt(w_lo, u_hi) + _dot_nt(w_hi, u_lo) + br_ref[...]
    le = logits[0:N_EXPERTS]
    lg = logits[N_EXPERTS:N_EXPERTS + N_GROUPS]
    gmax = jnp.max(lg, axis=0, keepdims=True)
    gi = lax.broadcasted_iota(i32, lg.shape, 0)
    gsel = jnp.min(jnp.where(lg == gmax, gi, N_GROUPS), axis=0, keepdims=True)
    gate = 1.0 / jnp.sum(jnp.exp(lg - gmax), axis=0, keepdims=True)
    ri = lax.broadcasted_iota(i32, le.shape, 0)
    lm = jnp.where(ri // EXPERTS_PER_GROUP == gsel, le, -jnp.inf)
    m1 = jnp.max(lm, axis=0, keepdims=True)
    i1 = jnp.min(jnp.where(lm == m1, ri, N_EXPERTS), axis=0, keepdims=True)
    lm2 = jnp.where(ri == i1, -jnp.inf, lm)
    m2 = jnp.max(lm2, axis=0, keepdims=True)
    i2 = jnp.min(jnp.where(lm2 == m2, ri, N_EXPERTS), axis=0, keepdims=True)
    t = jnp.exp(m2 - m1)
    den = 1.0 / (1.0 + t)
    r8 = lax.broadcasted_iota(i32, (SUBLANES, tm), 0)
    eid_ref[...] = jnp.where(r8 == 0, i1, jnp.where(r8 == 1, i2, 0))
    wts_ref[...] = jnp.where(r8 == 0, gate * den, jnp.where(r8 == 1, gate * t * den, 0.0))


def _mix(x, o_f, o_b, og, oc, g_head, woa, wob, g_ffn, wr, br, tm):
    n = x.shape[0]
    row = lambda w: pl.BlockSpec((tm, w), lambda i: (i, 0))
    col = pl.BlockSpec((SUBLANES, tm), lambda i: (0, i))
    full = lambda a: pl.BlockSpec(a.shape, lambda i: (0, 0))
    return pl.pallas_call(
        _mix_kernel,
        grid=(n // tm,),
        in_specs=[row(D_MODEL), row(GLA_VAL), row(GLA_VAL), row(GLA_VAL), row(CONV_CH),
                  full(g_head), full(woa), full(wob), full(g_ffn), full(wr), full(br)],
        out_specs=[row(D_MODEL), row(D_MODEL), col, col],
        out_shape=[jax.ShapeDtypeStruct((n, D_MODEL), f32), jax.ShapeDtypeStruct((n, D_MODEL), bf16),
                   jax.ShapeDtypeStruct((SUBLANES, n), i32), jax.ShapeDtypeStruct((SUBLANES, n), f32)],
        compiler_params=_cparams(("parallel",)),
        name="mix",
    )(x, o_f, o_b, og, oc, g_head, woa, wob, g_ffn, wr, br)


def _rank_kernel(eid_ref, pos_ref, tbl_ref, blk_ref, cnt_ref, base_ref, *, blk, nbp, tm):
    ps = pl.program_id(0)
    i = pl.program_id(1)
    rr = lax.broadcasted_iota(i32, (N_EXPERTS, LANES), 0)

    def cumsum_experts(x):
        for s in (1, 2, 4, 8, 16):
            x = x + jnp.where(rr >= s, pltpu.roll(x, s, axis=0), 0.0)
        return x

    @pl.when((ps == 0) & (i == 0))
    def _():
        cnt_ref[...] = jnp.zeros_like(cnt_ref)

    @pl.when(ps == 0)
    def _():
        pos_ref[...] = jnp.zeros_like(pos_ref)
        tbl_ref[...] = jnp.zeros_like(tbl_ref)

    @pl.when((ps == 1) & (i == 0))
    def _():
        cnt = cnt_ref[...]
        pc = jnp.floor((cnt + (blk - 1)) * (1.0 / blk)) * blk
        inc = cumsum_experts(pc)
        base_ref[...] = inc - pc
        pend = jnp.concatenate([inc] * (nbp // LANES), axis=1)
        jl = lax.broadcasted_iota(i32, (N_EXPERTS, nbp), 1).astype(f32) * blk
        be = jnp.minimum(jnp.sum(jnp.where(pend <= jl, 1.0, 0.0), axis=0, keepdims=True), N_EXPERTS - 1.0)
        nused = jnp.concatenate([inc[N_EXPERTS - 1:N_EXPERTS, :]] * (nbp // LANES), axis=1) * (1.0 / blk)
        r8 = lax.broadcasted_iota(i32, (SUBLANES, nbp), 0)
        blk_ref[...] = jnp.where(r8 == 0, be, jnp.where(r8 == 1, nused, 0.0)).astype(i32)

    ri = lax.broadcasted_iota(i32, (N_EXPERTS, tm), 0)
    for t in range(eid_ref.shape[1] // tm):
        lanes = slice(t * tm, (t + 1) * tm)
        oh1 = ri == eid_ref[0:1, lanes]
        oh2 = ri == eid_ref[1:2, lanes]
        ohf = jnp.where(oh1 | oh2, 1.0, 0.0)
        tile_cnt = jnp.sum(ohf, axis=1, keepdims=True)
        run_len = (jnp.floor((tile_cnt + (RUN_ALIGN - 1)) * (1.0 / RUN_ALIGN)) * RUN_ALIGN
                   + jnp.zeros((N_EXPERTS, LANES), f32))

        @pl.when(ps == 0)
        def _():
            cnt_ref[...] += run_len

        @pl.when(ps == 1)
        def _():
            a = lax.broadcasted_iota(i32, (tm, tm), 0)
            b = lax.broadcasted_iota(i32, (tm, tm), 1)
            upper = jnp.where(a < b, 1.0, 0.0).astype(bf16)
            before = _dot(ohf.astype(bf16), upper)
            run_start = cumsum_experts(run_len) - run_len
            row = run_start[:, 0:1] + before
            p1 = jnp.sum(jnp.where(oh1, row, 0.0), axis=0, keepdims=True)
            p2 = jnp.sum(jnp.where(oh2, row, 0.0), axis=0, keepdims=True)
            r8 = lax.broadcasted_iota(i32, (SUBLANES, tm), 0)
            pos_ref[0, :, lanes] = jnp.where(r8 == 0, p1, jnp.where(r8 == 1, p2, 0.0)).astype(i32)
            diag = lax.broadcasted_iota(i32, (N_EXPERTS, LANES), 1) == rr
            on_lanes = lambda x: jnp.sum(jnp.where(diag, x, 0.0), axis=0, keepdims=True)
            chunks = run_len * (1.0 / RUN_ALIGN)
            rows = (on_lanes(run_start), on_lanes(base_ref[...]), on_lanes(chunks),
                    jnp.sum(chunks, axis=0, keepdims=True))
            t8 = lax.broadcasted_iota(i32, (SUBLANES, LANES), 0)
            tbl = jnp.zeros((SUBLANES, LANES), f32)
            for k, x in enumerate(rows):
                tbl = jnp.where(t8 == k, x, tbl)
            tbl_ref[0, t] = tbl.astype(i32)
            base_ref[...] += run_len


def _rank(eid, blk, nbp, tm):
    n = eid.shape[1]
    group = _tile(n // tm, RANK_GROUP)
    return pl.pallas_call(
        functools.partial(_rank_kernel, blk=blk, nbp=nbp, tm=tm),
        grid=(2, n // (tm * group)),
        in_specs=[pl.BlockSpec((SUBLANES, tm * group), lambda p, i: (0, i))],
        out_specs=[pl.BlockSpec((1, SUBLANES, tm * group), lambda p, i: (p, 0, i)),
                   pl.BlockSpec((1, group, SUBLANES, LANES), lambda p, i: (p, i, 0, 0)),
                   pl.BlockSpec((SUBLANES, nbp), lambda p, i: (0, 0))],
        out_shape=[jax.ShapeDtypeStruct((2, SUBLANES, n), i32),
                   jax.ShapeDtypeStruct((2, n // tm, SUBLANES, LANES), i32),
                   jax.ShapeDtypeStruct((SUBLANES, nbp), i32)],
        scratch_shapes=[pltpu.VMEM((N_EXPERTS, LANES), f32), pltpu.VMEM((N_EXPERTS, LANES), f32)],
        compiler_params=_cparams(("arbitrary", "arbitrary")),
        name="rank",
    )(eid)


def _for_each_chunk(tbl_ref, fn):
    def per_expert(e, carry):
        sorted0 = tbl_ref[0, e]
        slot0 = tbl_ref[1, e]

        def per_chunk(c, carry2):
            off = c * RUN_ALIGN
            fn(pl.multiple_of(sorted0 + off, RUN_ALIGN), pl.multiple_of(slot0 + off, RUN_ALIGN))
            return carry2

        lax.fori_loop(0, tbl_ref[2, e], per_chunk, 0)
        return carry

    lax.fori_loop(0, N_EXPERTS, per_expert, 0)


def _wait_chunks(count, copy):
    def body(c, carry):
        copy(0, 0).wait()
        return carry

    lax.fori_loop(0, count, body, 0)


def _dispatch_kernel(tbl_ref, pos_ref, u_ref, xz_ref, xs_ref, sorted_ref, sems, pending_ref):
    del xz_ref
    i = pl.program_id(0)
    tm = u_ref.shape[0]
    buf = i % 2

    def copy_from(b):
        def copy(sorted_row, slot_row):
            return pltpu.make_async_copy(sorted_ref.at[b, pl.ds(sorted_row, RUN_ALIGN)],
                                         xs_ref.at[pl.ds(slot_row, RUN_ALIGN)], sems.at[b])
        return copy

    @pl.when(i == 0)
    def _():
        pending_ref[0] = 0
        pending_ref[1] = 0

    _wait_chunks(pending_ref[buf], copy_from(buf))
    p1 = pos_ref[0:1, :]
    p2 = pos_ref[1:2, :]
    u = u_ref[...]
    for r0 in range(0, sorted_ref.shape[1], SORT_ROWS):
        ri = lax.broadcasted_iota(i32, (SORT_ROWS, tm), 0) + r0
        sel = jnp.where((ri == p1) | (ri == p2), 1.0, 0.0).astype(bf16)
        sorted_ref[buf, r0:r0 + SORT_ROWS, :] = _pack_rows(_dot(sel, u))
    _for_each_chunk(tbl_ref, lambda s, d: copy_from(buf)(s, d).start())
    pending_ref[buf] = tbl_ref[3, 0]

    @pl.when(i == pl.num_programs(0) - 1)
    def _():
        for b in range(2):
            _wait_chunks(pending_ref[b], copy_from(b))


def _dispatch(tbl, pos, up, xz, tm, rmax):
    n = up.shape[0]
    return pl.pallas_call(
        _dispatch_kernel,
        grid=(n // tm,),
        in_specs=[pl.BlockSpec((SUBLANES, LANES), lambda i: (i, 0), memory_space=pltpu.SMEM),
                  pl.BlockSpec((SUBLANES, tm), lambda i: (0, i)),
                  pl.BlockSpec((tm, D_MODEL), lambda i: (i, 0)),
                  pl.BlockSpec(memory_space=pl.ANY)],
        out_specs=pl.BlockSpec(memory_space=pl.ANY),
        out_shape=jax.ShapeDtypeStruct(xz.shape, u32),
        scratch_shapes=[pltpu.VMEM((2, rmax, HALF), u32), pltpu.SemaphoreType.DMA((2,)), pltpu.SMEM((2,), i32)],
        input_output_aliases={3: 0},
        compiler_params=_cparams(("arbitrary",)),
        name="dispatch",
    )(tbl, pos, up, xz)


def _expert_kernel(be_ref, nu_ref, xs_ref, wg_ref, wu_ref, wd_ref, y_ref):
    j = pl.program_id(0)

    @pl.when(j < nu_ref[0])
    def _():
        xa, xb = _unpack_rows(xs_ref[...])
        xa = xa.astype(bf16)
        xb = xb.astype(bf16)
        g = _dot(xa, wg_ref[0, :HALF, :]) + _dot(xb, wg_ref[0, HALF:, :])
        u = _dot(xa, wu_ref[0, :HALF, :]) + _dot(xb, wu_ref[0, HALF:, :])
        hb = (g * jax.nn.sigmoid(g) * u).astype(bf16)
        y_ref[...] = _pack_rows(_dot(hb, wd_ref[0]))

    @pl.when(j >= nu_ref[0])
    def _():
        y_ref[...] = jnp.zeros_like(y_ref)


def _experts(block_e, nused, xs, wg, wu, wd, blk):
    nb = xs.shape[0] // blk
    used = lambda j, be, nu: (jnp.minimum(j, nu[0] - 1), 0)
    return pl.pallas_call(
        _expert_kernel,
        grid_spec=pltpu.PrefetchScalarGridSpec(
            num_scalar_prefetch=2,
            grid=(nb,),
            in_specs=[pl.BlockSpec((blk, HALF), used),
                      pl.BlockSpec((1, D_MODEL, D_EXPERT), lambda j, be, nu: (be[j], 0, 0)),
                      pl.BlockSpec((1, D_MODEL, D_EXPERT), lambda j, be, nu: (be[j], 0, 0)),
                      pl.BlockSpec((1, D_EXPERT, D_MODEL), lambda j, be, nu: (be[j], 0, 0))],
            out_specs=pl.BlockSpec((blk, HALF), lambda j, be, nu: (j, 0)),
        ),
        out_shape=jax.ShapeDtypeStruct(xs.shape, u32),
        compiler_params=_cparams(("arbitrary",)),
        name="experts",
    )(block_e, nused, xs, wg, wu, wd)


def _combine_kernel(tbl_ref, tbl_next_ref, pos_ref, wts_ref, h_ref, gfin_ref, y_ref, o_ref, ys_ref, sems):
    i = pl.program_id(0)
    tm = h_ref.shape[0]
    buf = i % 2

    def copy_into(b):
        def copy(sorted_row, slot_row):
            return pltpu.make_async_copy(y_ref.at[pl.ds(slot_row, RUN_ALIGN)],
                                         ys_ref.at[b, pl.ds(sorted_row, RUN_ALIGN)], sems.at[b])
        return copy

    @pl.when(i == 0)
    def _():
        ys_ref[...] = jnp.zeros_like(ys_ref)
        _for_each_chunk(tbl_ref, lambda s, d: copy_into(0)(s, d).start())

    @pl.when(i + 1 < pl.num_programs(0))
    def _():
        _for_each_chunk(tbl_next_ref, lambda s, d: copy_into(1 - buf)(s, d).start())

    pad = jnp.zeros((LANES - SUBLANES, tm), f32)
    pos_cols = jnp.transpose(jnp.concatenate([pos_ref[...].astype(f32), pad], axis=0))
    w_cols = jnp.transpose(jnp.concatenate([wts_ref[...], pad], axis=0))
    p1, p2 = pos_cols[:, 0:1], pos_cols[:, 1:2]
    w1, w2 = w_cols[:, 0:1], w_cols[:, 1:2]
    h = h_ref[...]
    ha = h[:, :HALF]
    hb = h[:, HALF:]
    _wait_chunks(tbl_ref[3, 0], copy_into(buf))
    for r0 in range(0, ys_ref.shape[1], SORT_ROWS):
        li = (lax.broadcasted_iota(i32, (tm, SORT_ROWS), 1) + r0).astype(f32)
        sel = (jnp.where(li == p1, w1, 0.0) + jnp.where(li == p2, w2, 0.0)).astype(bf16)
        ya, yb = _unpack_rows(ys_ref[buf, r0:r0 + SORT_ROWS, :])
        ha = ha + _dot(sel, ya.astype(bf16))
        hb = hb + _dot(sel, yb.astype(bf16))
    ms = (jnp.sum(ha * ha, axis=-1, keepdims=True) + jnp.sum(hb * hb, axis=-1, keepdims=True)) * (1.0 / D_MODEL)
    inv = lax.rsqrt(ms + EPS)
    o_ref[:, :HALF] = ha * inv * gfin_ref[:, :HALF]
    o_ref[:, HALF:] = hb * inv * gfin_ref[:, HALF:]


def _combine(tbl, pos, wts, h, g_final, ybuf, tm, rmax):
    n = h.shape[0]
    n_tiles = n // tm
    col = pl.BlockSpec((SUBLANES, tm), lambda i: (0, i))
    return pl.pallas_call(
        _combine_kernel,
        grid=(n_tiles,),
        in_specs=[pl.BlockSpec((SUBLANES, LANES), lambda i: (i, 0), memory_space=pltpu.SMEM),
                  pl.BlockSpec((SUBLANES, LANES), lambda i: (jnp.minimum(i + 1, n_tiles - 1), 0),
                               memory_space=pltpu.SMEM),
                  col, col,
                  pl.BlockSpec((tm, D_MODEL), lambda i: (i, 0)),
                  pl.BlockSpec(g_final.shape, lambda i: (0, 0)),
                  pl.BlockSpec(memory_space=pl.ANY)],
        out_specs=pl.BlockSpec((tm, D_MODEL), lambda i: (i, 0)),
        out_shape=jax.ShapeDtypeStruct((n, D_MODEL), f32),
        scratch_shapes=[pltpu.VMEM((2, rmax, HALF), u32), pltpu.SemaphoreType.DMA((2,))],
        compiler_params=_cparams(("arbitrary",)),
        name="combine",
    )(tbl, tbl, pos, wts, h, g_final, ybuf)


def _tile(n, pref):
    t = pref
    while n % t:
        t //= 2
    return t


def _prep_weights(g_mix, w_in, w_af2, b_af, w_ab2, b_ab, g_head, w_dw, b_dw, g_ln, b_ln, w_pw2, w_out,
                  g_ffn, w_rg, b_rg, w_re, b_re, w_e_gate, w_e_up, w_e_down, g_final):
    n_qkvo = 2 * GLA_KEY + 2 * GLA_VAL
    n_a = 2 * GATE_RANK
    w_in = w_in[0]
    zg = jnp.zeros((GATE_RANK, GLA_KEY), f32)
    wgate = jnp.concatenate([jnp.concatenate([w_af2[0], zg], axis=1), jnp.concatenate([zg, w_ab2[0]], axis=1)], axis=0)
    wdw = jnp.concatenate([w_dw[0], jnp.zeros((1, CONV_CH), f32)], axis=0)
    pad = jnp.zeros((SUBLANES - N_GROUPS, D_MODEL), f32)
    wr = jnp.concatenate([w_re[0].T, w_rg[0].T, pad], axis=0)
    br = jnp.concatenate([b_re[0], b_rg[0], jnp.zeros((SUBLANES - N_GROUPS,), f32)])[:, None]
    return dict(
        g_mix=g_mix[0][None, :],
        wqkvo=w_in[:, :n_qkvo].astype(bf16),
        wa=w_in[:, n_qkvo:n_qkvo + n_a].astype(bf16),
        wc=w_in[:, n_qkvo + n_a:].astype(bf16),
        wgate=wgate.astype(bf16),
        bgate=jnp.concatenate([b_af[0], b_ab[0]])[None, :],
        g_head=g_head[0][None, :],
        wdw=wdw, bdw=b_dw[0][None, :], gln=g_ln[0][None, :], bln=b_ln[0][None, :],
        wpw=w_pw2[0].astype(bf16),
        woa=w_out[0, :GLA_VAL].astype(bf16), wob=w_out[0, GLA_VAL:].astype(bf16),
        g_ffn=g_ffn[0][None, :], wr=wr, br=br,
        wg=w_e_gate[0].astype(bf16), wu=w_e_up[0].astype(bf16), wd=w_e_down[0].astype(bf16),
        g_final=g_final[None, :],
    )


def _trunk(x3, w):
    n_seq, seq_len, _ = x3.shape
    n = n_seq * seq_len
    x = x3.reshape(n, D_MODEL)
    tm = _tile(n, MOE_TILE)
    q, k, v, og, la, glu = _inproj(x, w["g_mix"], w["wqkvo"], w["wa"], w["wc"], w["wgate"], w["bgate"], tm)
    o_f, o_b = _gla(q, k, v, la, n_seq, seq_len)
    oc = _conv(glu, w["wdw"], w["bdw"], w["gln"], w["bln"], w["wpw"], seq_len, _tile(seq_len, 512))
    h, up, eid, wts = _mix(x, o_f, o_b, og, oc, w["g_head"], w["woa"], w["wob"], w["g_ffn"], w["wr"], w["br"], tm)

    n_tiles = n // tm
    max_slots = n * TOP_K + n_tiles * N_EXPERTS * (RUN_ALIGN - 1)
    nb = -(-max_slots // MOE_BLOCK) + N_EXPERTS
    nbp = -(-nb // LANES) * LANES
    rmax = -(-(tm * TOP_K + N_EXPERTS * (RUN_ALIGN - 1)) // SORT_ROWS) * SORT_ROWS
    pos, tbl, blkinfo = _rank(eid, MOE_BLOCK, nbp, tm)
    pos = pos[1]
    tbl = tbl[1].reshape(n_tiles * SUBLANES, LANES)
    xs = _dispatch(tbl, pos, up, jnp.zeros((nb * MOE_BLOCK, HALF), u32), tm, rmax)
    ybuf = _experts(blkinfo[0], blkinfo[1, :1], xs, w["wg"], w["wu"], w["wd"], MOE_BLOCK)
    y = _combine(tbl, pos, wts, h, w["g_final"], ybuf, tm, rmax)
    return y.reshape(x3.shape)


def kernel(x_prompt, x_sample, g_mix, w_in, w_af2, b_af, w_ab2, b_ab, g_head, w_dw, b_dw, g_ln, b_ln, w_pw2,
           w_out, g_ffn, w_rg, b_rg, w_re, b_re, w_e_gate, w_e_up, w_e_down, g_final):
    w = _prep_weights(g_mix, w_in, w_af2, b_af, w_ab2, b_ab, g_head, w_dw, b_dw, g_ln, b_ln, w_pw2, w_out,
                      g_ffn, w_rg, b_rg, w_re, b_re, w_e_gate, w_e_up, w_e_down, g_final)
    return (_trunk(x_prompt, w), _trunk(x_sample, w))
```

```python
import functools

import jax
import jax.numpy as jnp
from jax import lax
from jax.experimental import pallas as pl
from jax.experimental.pallas import tpu as pltpu

f32 = jnp.float32
bf16 = jnp.bfloat16
i32 = jnp.int32
u32 = jnp.uint32

D_MODEL = 1024
GLA_HEADS = 4
GLA_DK = 64
GLA_DV = 128
GLA_KEY = GLA_HEADS * GLA_DK
GLA_VAL = GLA_HEADS * GLA_DV
GATE_RANK = 16
GATE_NORMALIZER = 16
CONV_CH = 512
CONV_K = 31
N_GROUPS = 4
EXPERTS_PER_GROUP = 8
N_EXPERTS = N_GROUPS * EXPERTS_PER_GROUP
TOP_K = 2
D_EXPERT = 512
EPS = 1e-6

LANES = 128
SUBLANES = 8
GLA_CHUNK = 64
GLA_TILE = 256
PAIR_DK = 2 * GLA_DK
PAIR_DV = 2 * GLA_DV
CONV_HALO = 16
CONV_ROWS = 16
HALF = D_MODEL // 2
MOE_TILE = 512
MOE_BLOCK = 512
RUN_ALIGN = SUBLANES
SORT_ROWS = 256
RANK_GROUP = 8
BIG_CHUNK = 32
WAIT_ROWS = 256
VMEM_LIMIT = 56 * 1024 * 1024


def _cparams(sem, **kw):
    return pltpu.CompilerParams(dimension_semantics=sem, vmem_limit_bytes=VMEM_LIMIT, **kw)


def _dot(a, b):
    return jnp.dot(a, b, preferred_element_type=f32)


def _dot_nt(a, b):
    return lax.dot_general(a, b, (((1,), (1,)), ((), ())), preferred_element_type=f32)


def _dot_tn(a, b):
    return lax.dot_general(a, b, (((0,), (0,)), ((), ())), preferred_element_type=f32)


def _split_bf16(x):
    hi = x.astype(bf16)
    lo = (x - hi.astype(f32)).astype(bf16)
    return hi, lo


def _pack_rows(x):
    a = lax.bitcast_convert_type(x[:, :HALF].astype(bf16).astype(f32), u32)
    b = lax.bitcast_convert_type(x[:, HALF:].astype(bf16).astype(f32), u32)
    return a | (b >> 16)


def _unpack_rows(w):
    a = lax.bitcast_convert_type(w & jnp.uint32(0xFFFF0000), f32)
    b = lax.bitcast_convert_type(w << 16, f32)
    return a, b


def _rms(x):
    return x * lax.rsqrt(jnp.mean(x * x, axis=-1, keepdims=True) + EPS)


def _inproj_kernel(x_ref, g_ref, wqkvo_ref, wa_ref, wc_ref, wgate_ref, bgate_ref,
                   q_ref, k_ref, v_ref, og_ref, la_ref, glu_ref):
    u = (_rms(x_ref[...]) * g_ref[...]).astype(bf16)
    p = _dot(u, wqkvo_ref[...])
    q_ref[...] = (p[:, :GLA_KEY] * (GLA_DK ** -0.5)).astype(bf16)
    k_ref[...] = p[:, GLA_KEY:2 * GLA_KEY].astype(bf16)
    v_ref[...] = p[:, 2 * GLA_KEY:2 * GLA_KEY + GLA_VAL].astype(bf16)
    og_ref[...] = p[:, 2 * GLA_KEY + GLA_VAL:].astype(bf16)
    a = _dot(u, wa_ref[...])
    z = _dot(a.astype(bf16), wgate_ref[...]) + bgate_ref[...]
    la_ref[...] = (jnp.minimum(z, 0.0) - jnp.log1p(jnp.exp(-jnp.abs(z)))) * (1.0 / GATE_NORMALIZER)
    c = _dot(u, wc_ref[...])
    glu_ref[...] = (c[:, :CONV_CH] * jax.nn.sigmoid(c[:, CONV_CH:])).astype(bf16)


def _inproj(x, g_mix, wqkvo, wa, wc, wgate, bgate, tm):
    n = x.shape[0]
    row = lambda w: pl.BlockSpec((tm, w), lambda i: (i, 0))
    full = lambda a: pl.BlockSpec(a.shape, lambda i: (0, 0))
    return pl.pallas_call(
        _inproj_kernel,
        grid=(n // tm,),
        in_specs=[row(D_MODEL), full(g_mix), full(wqkvo), full(wa), full(wc), full(wgate), full(bgate)],
        out_specs=[row(GLA_KEY), row(GLA_KEY), row(GLA_VAL), row(GLA_VAL), row(2 * GLA_KEY), row(CONV_CH)],
        out_shape=[jax.ShapeDtypeStruct((n, GLA_KEY), bf16), jax.ShapeDtypeStruct((n, GLA_KEY), bf16),
                   jax.ShapeDtypeStruct((n, GLA_VAL), bf16), jax.ShapeDtypeStruct((n, GLA_VAL), bf16),
                   jax.ShapeDtypeStruct((n, 2 * GLA_KEY), f32), jax.ShapeDtypeStruct((n, CONV_CH), bf16)],
        compiler_params=_cparams(("parallel",)),
        name="inproj",
    )(x, g_mix, wqkvo, wa, wc, wgate, bgate)


def _gla_unit(q_ref, k_ref, v_ref, o_ref, st_ref, cs, la, pair, reverse, masks):
    head0_k, head0_v, keep = masks
    tl = cs.shape[0]
    kl = slice(pair * PAIR_DK, (pair + 1) * PAIR_DK)
    vl = slice(pair * PAIR_DV, (pair + 1) * PAIR_DV)
    n_chunks = tl // GLA_CHUNK
    order = list(range(n_chunks - 1, -1, -1) if reverse else range(n_chunks))

    def stack_heads(x, head0):
        z = jnp.zeros_like(x)
        return jnp.concatenate([jnp.where(head0, x, z), jnp.where(head0, z, x)], axis=0)

    local = {}
    for j in order:
        rows = slice(j * GLA_CHUNK, (j + 1) * GLA_CHUNK)
        cj = cs[rows, kl]
        q = q_ref[rows, kl].astype(f32)
        k = k_ref[rows, kl].astype(f32)
        c_last = cj[GLA_CHUNK - 1:GLA_CHUNK, :]
        mid = 0.5 * c_last
        rel = (mid - (cj - la[rows, kl])) if reverse else (cj - mid)
        qd = q * jnp.exp(rel)
        kd = k * jnp.exp(-rel)
        edge = jnp.exp(mid)
        qi = qd * edge
        ku = kd * edge
        vbd = stack_heads(v_ref[rows, vl], head0_v)
        sc = _dot_nt(qd.astype(bf16), stack_heads(kd.astype(bf16), head0_k))
        p = jnp.where(keep, sc, 0.0).astype(bf16)
        kv = _dot_tn(stack_heads(ku.astype(bf16), head0_k), vbd)
        local[j] = (p, vbd, qi.astype(bf16), kv, c_last)

    row_id = lax.broadcasted_iota(i32, (PAIR_DK, PAIR_DK), 0)
    c_rows = jnp.zeros((PAIR_DK, PAIR_DK), f32)
    for j in order:
        c_rows = jnp.where(row_id == j, local[j][4], c_rows)
    decay_cols = jnp.exp(jnp.transpose(c_rows))
    st = st_ref[...]
    st_before = {}
    for j in order:
        st_before[j] = st.astype(bf16)
        st = decay_cols[:, j:j + 1] * st + local[j][3]
    st_ref[...] = st
    for j in order:
        p, vbd, qi, _, _ = local[j]
        o = _dot(p, vbd) + _dot(qi, st_before[j])
        o_ref[j * GLA_CHUNK:(j + 1) * GLA_CHUNK, vl] = o.astype(o_ref.dtype)


def _gla_kernel(qf, kf, vf, laf, qb, kb, vb, lab, of, ob, st_ref):
    @pl.when(pl.program_id(1) == 0)
    def _():
        st_ref[...] = jnp.zeros_like(st_ref)

    tl = qf.shape[0]
    r = lax.broadcasted_iota(i32, (tl, tl), 0)
    c = lax.broadcasted_iota(i32, (tl, tl), 1)
    tri = jnp.where((r // GLA_CHUNK == c // GLA_CHUNK) & (c <= r), 1.0, 0.0).astype(bf16)
    lane_k = lax.broadcasted_iota(i32, (GLA_CHUNK, PAIR_DK), 1)
    head0_k = lane_k < GLA_DK
    head0_v = lax.broadcasted_iota(i32, (GLA_CHUNK, PAIR_DV), 1) < GLA_DV
    t_idx = lax.broadcasted_iota(i32, (GLA_CHUNK, PAIR_DK), 0)
    s_idx = lane_k % GLA_CHUNK
    for d, (q_ref, k_ref, v_ref, la_ref, o_ref) in enumerate(((qf, kf, vf, laf, of), (qb, kb, vb, lab, ob))):
        reverse = d == 1
        la = la_ref[...]
        la_hi, la_lo = _split_bf16(la)
        cs = _dot(tri, la_hi) + _dot(tri, la_lo)
        keep = (s_idx > t_idx) if reverse else (s_idx <= t_idx)
        for pair in range(GLA_HEADS // 2):
            _gla_unit(q_ref, k_ref, v_ref, o_ref, st_ref.at[d, pair], cs, la, pair, reverse,
                      (head0_k, head0_v, keep))


def _gla(q, k, v, la, n_seq, seq_len):
    n = q.shape[0]
    nblk = seq_len // GLA_TILE
    fwd = lambda b, i: (b * nblk + i, 0)
    bwd = lambda b, i: (b * nblk + nblk - 1 - i, 0)
    bwd_la = lambda b, i: (b * nblk + nblk - 1 - i, 1)
    kq = lambda im: pl.BlockSpec((GLA_TILE, GLA_KEY), im)
    vv = lambda im: pl.BlockSpec((GLA_TILE, GLA_VAL), im)
    return pl.pallas_call(
        _gla_kernel,
        grid=(n_seq, nblk),
        in_specs=[kq(fwd), kq(fwd), vv(fwd), kq(fwd), kq(bwd), kq(bwd), vv(bwd), kq(bwd_la)],
        out_specs=[vv(fwd), vv(bwd)],
        out_shape=[jax.ShapeDtypeStruct((n, GLA_VAL), bf16), jax.ShapeDtypeStruct((n, GLA_VAL), bf16)],
        scratch_shapes=[pltpu.VMEM((2, GLA_HEADS // 2, PAIR_DK, PAIR_DV), f32)],
        compiler_params=_cparams(("parallel", "arbitrary")),
        name="gla",
    )(q, k, v, la, q, k, v, la)


def _conv_kernel(prev_ref, cur_ref, next_ref, wdw_ref, bdw_ref, gln_ref, bln_ref, wpw_ref,
                 o_ref, ext_ref, y_ref, *, tiles_per_seq):
    tl = cur_ref.shape[0]
    pos = pl.program_id(0) % tiles_per_seq
    prev = jnp.where(pos == 0, 0.0, prev_ref[...].astype(f32))
    nxt = jnp.where(pos == tiles_per_seq - 1, 0.0, next_ref[...].astype(f32))
    cur = cur_ref[...].astype(f32)
    n_slabs = CONV_CH // LANES
    for c in range(n_slabs):
        lanes = slice(c * LANES, (c + 1) * LANES)
        ext_ref[c, 0:CONV_HALO, :] = prev[:, lanes]
        ext_ref[c, CONV_HALO:CONV_HALO + tl, :] = cur[:, lanes]
        ext_ref[c, CONV_HALO + tl:, :] = nxt[:, lanes]
    off = CONV_HALO - CONV_K // 2

    def body(rb, carry):
        base = pl.multiple_of(rb * CONV_ROWS, CONV_ROWS)
        for c in range(n_slabs):
            lanes = slice(c * LANES, (c + 1) * LANES)
            a = jnp.zeros((CONV_ROWS, LANES), f32) + bdw_ref[:, lanes]
            for j in range(CONV_K):
                taps = ext_ref.at[c, pl.ds(off + j, tl)]
                a = a + taps[pl.ds(base, CONV_ROWS), :] * wdw_ref[j:j + 1, lanes]
            y_ref[pl.ds(base, CONV_ROWS), lanes] = a
        return carry

    lax.fori_loop(0, tl // CONV_ROWS, body, 0)
    acc = y_ref[...]
    mu = jnp.mean(acc, axis=-1, keepdims=True)
    xc = acc - mu
    yn = xc * lax.rsqrt(jnp.mean(xc * xc, axis=-1, keepdims=True) + EPS) * gln_ref[...] + bln_ref[...]
    o_ref[...] = _dot((yn * jax.nn.sigmoid(yn)).astype(bf16), wpw_ref[...]).astype(bf16)


def _conv(glu, wdw, bdw, gln, bln, wpw, seq_len, tl):
    n = glu.shape[0]
    hb = tl // CONV_HALO
    n_halo = n // CONV_HALO
    full = lambda a: pl.BlockSpec(a.shape, lambda i: (0, 0))
    return pl.pallas_call(
        functools.partial(_conv_kernel, tiles_per_seq=seq_len // tl),
        grid=(n // tl,),
        in_specs=[pl.BlockSpec((CONV_HALO, CONV_CH), lambda i: (jnp.maximum(i * hb - 1, 0), 0)),
                  pl.BlockSpec((tl, CONV_CH), lambda i: (i, 0)),
                  pl.BlockSpec((CONV_HALO, CONV_CH), lambda i: (jnp.minimum((i + 1) * hb, n_halo - 1), 0)),
                  full(wdw), full(bdw), full(gln), full(bln), full(wpw)],
        out_specs=pl.BlockSpec((tl, CONV_CH), lambda i: (i, 0)),
        out_shape=jax.ShapeDtypeStruct((n, CONV_CH), bf16),
        scratch_shapes=[pltpu.VMEM((CONV_CH // LANES, tl + 2 * CONV_HALO, LANES), f32),
                        pltpu.VMEM((tl, CONV_CH), f32)],
        compiler_params=_cparams(("parallel",)),
        name="conv",
    )(glu, glu, glu, wdw, bdw, gln, bln, wpw)


def _mix_kernel(x_ref, of_ref, ob_ref, og_ref, oc_ref, gh_ref, woa_ref, wob_ref, gf_ref, wr_ref, br_ref,
                h_ref, up_ref, eid_ref, wts_ref):
    tm = x_ref.shape[0]
    o = of_ref[...].astype(f32) + ob_ref[...].astype(f32)
    o = jnp.concatenate([_rms(o[:, h * GLA_DV:(h + 1) * GLA_DV]) for h in range(GLA_HEADS)], axis=1)
    og = og_ref[...].astype(f32)
    o = o * gh_ref[...] * (og * jax.nn.sigmoid(og))
    h = x_ref[...] + _dot(o.astype(bf16), woa_ref[...]) + _dot(oc_ref[...], wob_ref[...])
    h_ref[...] = h
    u = _rms(h) * gf_ref[...]
    u_hi, u_lo = _split_bf16(u)
    up_ref[...] = u_hi

    w_hi, w_lo = _split_bf16(wr_ref[...])
    logits = _dot_nt(w_hi, u_hi) + _dot_nt(w_lo, u_hi) + _dot_nt(w_hi, u_lo) + br_ref[...]
    le = logits[0:N_EXPERTS]
    lg = logits[N_EXPERTS:N_EXPERTS + N_GROUPS]
    gmax = jnp.max(lg, axis=0, keepdims=True)
    gi = lax.broadcasted_iota(i32, lg.shape, 0)
    gsel = jnp.min(jnp.where(lg == gmax, gi, N_GROUPS), axis=0, keepdims=True)
    gate = 1.0 / jnp.sum(jnp.exp(lg - gmax), axis=0, keepdims=True)
    ri = lax.broadcasted_iota(i32, le.shape, 0)
    lm = jnp.where(ri // EXPERTS_PER_GROUP == gsel, le, -jnp.inf)
    m1 = jnp.max(lm, axis=0, keepdims=True)
    i1 = jnp.min(jnp.where(lm == m1, ri, N_EXPERTS), axis=0, keepdims=True)
    lm2 = jnp.where(ri == i1, -jnp.inf, lm)
    m2 = jnp.max(lm2, axis=0, keepdims=True)
    i2 = jnp.min(jnp.where(lm2 == m2, ri, N_EXPERTS), axis=0, keepdims=True)
    t = jnp.exp(m2 - m1)
    den = 1.0 / (1.0 + t)
    r8 = lax.broadcasted_iota(i32, (SUBLANES, tm), 0)
    eid_ref[...] = jnp.where(r8 == 0, i1, jnp.where(r8 == 1, i2, 0))
    wts_ref[...] = jnp.where(r8 == 0, gate * den, jnp.where(r8 == 1, gate * t * den, 0.0))


def _mix(x, o_f, o_b, og, oc, g_head, woa, wob, g_ffn, wr, br, tm):
    n = x.shape[0]
    row = lambda w: pl.BlockSpec((tm, w), lambda i: (i, 0))
    col = pl.BlockSpec((SUBLANES, tm), lambda i: (0, i))
    full = lambda a: pl.BlockSpec(a.shape, lambda i: (0, 0))
    return pl.pallas_call(
        _mix_kernel,
        grid=(n // tm,),
        in_specs=[row(D_MODEL), row(GLA_VAL), row(GLA_VAL), row(GLA_VAL), row(CONV_CH),
                  full(g_head), full(woa), full(wob), full(g_ffn), full(wr), full(br)],
        out_specs=[row(D_MODEL), row(D_MODEL), col, col],
        out_shape=[jax.ShapeDtypeStruct((n, D_MODEL), f32), jax.ShapeDtypeStruct((n, D_MODEL), bf16),
                   jax.ShapeDtypeStruct((SUBLANES, n), i32), jax.ShapeDtypeStruct((SUBLANES, n), f32)],
        compiler_params=_cparams(("parallel",)),
        name="mix",
    )(x, o_f, o_b, og, oc, g_head, woa, wob, g_ffn, wr, br)


def _rank_kernel(eid_ref, pos_ref, tbl_ref, blk_ref, cnt_ref, base_ref, *, blk, nbp, tm):
    ps = pl.program_id(0)
    i = pl.program_id(1)
    rr = lax.broadcasted_iota(i32, (N_EXPERTS, LANES), 0)

    def cumsum_experts(x):
        for s in (1, 2, 4, 8, 16):
            x = x + jnp.where(rr >= s, pltpu.roll(x, s, axis=0), 0.0)
        return x

    @pl.when((ps == 0) & (i == 0))
    def _():
        cnt_ref[...] = jnp.zeros_like(cnt_ref)

    @pl.when(ps == 0)
    def _():
        pos_ref[...] = jnp.zeros_like(pos_ref)
        tbl_ref[...] = jnp.zeros_like(tbl_ref)

    @pl.when((ps == 1) & (i == 0))
    def _():
        cnt = cnt_ref[...]
        pc = jnp.floor((cnt + (blk - 1)) * (1.0 / blk)) * blk
        inc = cumsum_experts(pc)
        base_ref[...] = inc - pc
        pend = jnp.concatenate([inc] * (nbp // LANES), axis=1)
        jl = lax.broadcasted_iota(i32, (N_EXPERTS, nbp), 1).astype(f32) * blk
        be = jnp.minimum(jnp.sum(jnp.where(pend <= jl, 1.0, 0.0), axis=0, keepdims=True), N_EXPERTS - 1.0)
        nused = jnp.concatenate([inc[N_EXPERTS - 1:N_EXPERTS, :]] * (nbp // LANES), axis=1) * (1.0 / blk)
        r8 = lax.broadcasted_iota(i32, (SUBLANES, nbp), 0)
        blk_ref[...] = jnp.where(r8 == 0, be, jnp.where(r8 == 1, nused, 0.0)).astype(i32)

    ri = lax.broadcasted_iota(i32, (N_EXPERTS, tm), 0)
    for t in range(eid_ref.shape[1] // tm):
        lanes = slice(t * tm, (t + 1) * tm)
        oh1 = ri == eid_ref[0:1, lanes]
        oh2 = ri == eid_ref[1:2, lanes]
        ohf = jnp.where(oh1 | oh2, 1.0, 0.0)
        tile_cnt = jnp.sum(ohf, axis=1, keepdims=True)
        run_len = (jnp.floor((tile_cnt + (RUN_ALIGN - 1)) * (1.0 / RUN_ALIGN)) * RUN_ALIGN
                   + jnp.zeros((N_EXPERTS, LANES), f32))

        @pl.when(ps == 0)
        def _():
            cnt_ref[...] += run_len

        @pl.when(ps == 1)
        def _():
            a = lax.broadcasted_iota(i32, (tm, tm), 0)
            b = lax.broadcasted_iota(i32, (tm, tm), 1)
            upper = jnp.where(a < b, 1.0, 0.0).astype(bf16)
            before = _dot(ohf.astype(bf16), upper)
            run_start = cumsum_experts(run_len) - run_len
            row = run_start[:, 0:1] + before
            p1 = jnp.sum(jnp.where(oh1, row, 0.0), axis=0, keepdims=True)
            p2 = jnp.sum(jnp.where(oh2, row, 0.0), axis=0, keepdims=True)
            r8 = lax.broadcasted_iota(i32, (SUBLANES, tm), 0)
            pos_ref[0, :, lanes] = jnp.where(r8 == 0, p1, jnp.where(r8 == 1, p2, 0.0)).astype(i32)
            diag = lax.broadcasted_iota(i32, (N_EXPERTS, LANES), 1) == rr
            on_lanes = lambda x: jnp.sum(jnp.where(diag, x, 0.0), axis=0, keepdims=True)
            big = jnp.floor(run_len * (1.0 / BIG_CHUNK))
            small = (run_len - big * BIG_CHUNK) * (1.0 / RUN_ALIGN)
            rows = (on_lanes(run_start), on_lanes(base_ref[...]), on_lanes(big), on_lanes(small),
                    jnp.sum(run_len, axis=0, keepdims=True))
            t8 = lax.broadcasted_iota(i32, (SUBLANES, LANES), 0)
            tbl = jnp.zeros((SUBLANES, LANES), f32)
            for k, x in enumerate(rows):
                tbl = jnp.where(t8 == k, x, tbl)
            tbl_ref[0, t] = tbl.astype(i32)
            base_ref[...] += run_len


def _rank(eid, blk, nbp, tm):
    n = eid.shape[1]
    group = _tile(n // tm, RANK_GROUP)
    return pl.pallas_call(
        functools.partial(_rank_kernel, blk=blk, nbp=nbp, tm=tm),
        grid=(2, n // (tm * group)),
        in_specs=[pl.BlockSpec((SUBLANES, tm * group), lambda p, i: (0, i))],
        out_specs=[pl.BlockSpec((1, SUBLANES, tm * group), lambda p, i: (p, 0, i)),
                   pl.BlockSpec((1, group, SUBLANES, LANES), lambda p, i: (p, i, 0, 0)),
                   pl.BlockSpec((SUBLANES, nbp), lambda p, i: (0, 0))],
        out_shape=[jax.ShapeDtypeStruct((2, SUBLANES, n), i32),
                   jax.ShapeDtypeStruct((2, n // tm, SUBLANES, LANES), i32),
                   jax.ShapeDtypeStruct((SUBLANES, nbp), i32)],
        scratch_shapes=[pltpu.VMEM((N_EXPERTS, LANES), f32), pltpu.VMEM((N_EXPERTS, LANES), f32)],
        compiler_params=_cparams(("arbitrary", "arbitrary")),
        name="rank",
    )(eid)


def _start_run_copies(tbl_ref, copy):
    def per_expert(e, carry):
        sorted0 = tbl_ref[0, e]
        slot0 = tbl_ref[1, e]
        n_big = tbl_ref[2, e]

        def pieces(rows, first):
            def body(c, carry2):
                off = first + c * rows
                copy(pl.multiple_of(sorted0 + off, RUN_ALIGN), pl.multiple_of(slot0 + off, RUN_ALIGN), rows).start()
                return carry2
            return body

        lax.fori_loop(0, n_big, pieces(BIG_CHUNK, 0), 0)
        lax.fori_loop(0, tbl_ref[3, e], pieces(RUN_ALIGN, n_big * BIG_CHUNK), 0)
        return carry

    lax.fori_loop(0, N_EXPERTS, per_expert, 0)


def _wait_rows(total_rows, copy):
    def waits(rows):
        def body(c, carry):
            copy(0, 0, rows).wait()
            return carry
        return body

    n_wide = total_rows // WAIT_ROWS
    lax.fori_loop(0, n_wide, waits(WAIT_ROWS), 0)
    lax.fori_loop(0, (total_rows - n_wide * WAIT_ROWS) // RUN_ALIGN, waits(RUN_ALIGN), 0)


def _dispatch_kernel(tbl_ref, pos_ref, u_ref, xz_ref, xs_ref, sorted_ref, sems, pending_ref):
    del xz_ref
    i = pl.program_id(0)
    tm = u_ref.shape[0]
    buf = i % 2

    def copy_from(b):
        def copy(sorted_row, slot_row, rows):
            return pltpu.make_async_copy(sorted_ref.at[b, pl.ds(sorted_row, rows)],
                                         xs_ref.at[pl.ds(slot_row, rows)], sems.at[b])
        return copy

    @pl.when(i == 0)
    def _():
        pending_ref[0] = 0
        pending_ref[1] = 0

    _wait_rows(pending_ref[buf], copy_from(buf))
    p1 = pos_ref[0:1, :]
    p2 = pos_ref[1:2, :]
    u = u_ref[...]
    for r0 in range(0, sorted_ref.shape[1], SORT_ROWS):
        ri = lax.broadcasted_iota(i32, (SORT_ROWS, tm), 0) + r0
        sel = jnp.where((ri == p1) | (ri == p2), 1.0, 0.0).astype(bf16)
        sorted_ref[buf, r0:r0 + SORT_ROWS, :] = _pack_rows(_dot(sel, u))
    _start_run_copies(tbl_ref, copy_from(buf))
    pending_ref[buf] = tbl_ref[4, 0]

    @pl.when(i == pl.num_programs(0) - 1)
    def _():
        for b in range(2):
            _wait_rows(pending_ref[b], copy_from(b))


def _dispatch(tbl, pos, up, xz, tm, rmax):
    n = up.shape[0]
    return pl.pallas_call(
        _dispatch_kernel,
        grid=(n // tm,),
        in_specs=[pl.BlockSpec((SUBLANES, LANES), lambda i: (i, 0), memory_space=pltpu.SMEM),
                  pl.BlockSpec((SUBLANES, tm), lambda i: (0, i)),
                  pl.BlockSpec((tm, D_MODEL), lambda i: (i, 0)),
                  pl.BlockSpec(memory_space=pl.ANY)],
        out_specs=pl.BlockSpec(memory_space=pl.ANY),
        out_shape=jax.ShapeDtypeStruct(xz.shape, u32),
        scratch_shapes=[pltpu.VMEM((2, rmax, HALF), u32), pltpu.SemaphoreType.DMA((2,)), pltpu.SMEM((2,), i32)],
        input_output_aliases={3: 0},
        compiler_params=_cparams(("arbitrary",)),
        name="dispatch",
    )(tbl, pos, up, xz)


def _expert_kernel(be_ref, nu_ref, xs_ref, wg_ref, wu_ref, wd_ref, y_ref):
    j = pl.program_id(0)

    @pl.when(j < nu_ref[0])
    def _():
        xa, xb = _unpack_rows(xs_ref[...])
        xa = xa.astype(bf16)
        xb = xb.astype(bf16)
        g = _dot(xa, wg_ref[0, :HALF, :]) + _dot(xb, wg_ref[0, HALF:, :])
        u = _dot(xa, wu_ref[0, :HALF, :]) + _dot(xb, wu_ref[0, HALF:, :])
        hb = (g * jax.nn.sigmoid(g) * u).astype(bf16)
        y_ref[...] = _pack_rows(_dot(hb, wd_ref[0]))

    @pl.when(j >= nu_ref[0])
    def _():
        y_ref[...] = jnp.zeros_like(y_ref)


def _experts(block_e, nused, xs, wg, wu, wd, blk):
    nb = xs.shape[0] // blk
    used = lambda j, be, nu: (jnp.minimum(j, nu[0] - 1), 0)
    return pl.pallas_call(
        _expert_kernel,
        grid_spec=pltpu.PrefetchScalarGridSpec(
            num_scalar_prefetch=2,
            grid=(nb,),
            in_specs=[pl.BlockSpec((blk, HALF), used),
                      pl.BlockSpec((1, D_MODEL, D_EXPERT), lambda j, be, nu: (be[j], 0, 0)),
                      pl.BlockSpec((1, D_MODEL, D_EXPERT), lambda j, be, nu: (be[j], 0, 0)),
                      pl.BlockSpec((1, D_EXPERT, D_MODEL), lambda j, be, nu: (be[j], 0, 0))],
            out_specs=pl.BlockSpec((blk, HALF), lambda j, be, nu: (j, 0)),
        ),
        out_shape=jax.ShapeDtypeStruct(xs.shape, u32),
        compiler_params=_cparams(("arbitrary",)),
        name="experts",
    )(block_e, nused, xs, wg, wu, wd)


def _combine_kernel(tbl_ref, tbl_next_ref, pos_ref, wts_ref, h_ref, gfin_ref, y_ref, o_ref, ys_ref, sems):
    i = pl.program_id(0)
    tm = h_ref.shape[0]
    buf = i % 2

    def copy_into(b):
        def copy(sorted_row, slot_row, rows):
            return pltpu.make_async_copy(y_ref.at[pl.ds(slot_row, rows)],
                                         ys_ref.at[b, pl.ds(sorted_row, rows)], sems.at[b])
        return copy

    @pl.when(i == 0)
    def _():
        ys_ref[...] = jnp.zeros_like(ys_ref)
        _start_run_copies(tbl_ref, copy_into(0))

    @pl.when(i + 1 < pl.num_programs(0))
    def _():
        _start_run_copies(tbl_next_ref, copy_into(1 - buf))

    pad = jnp.zeros((LANES - SUBLANES, tm), f32)
    pos_cols = jnp.transpose(jnp.concatenate([pos_ref[...].astype(f32), pad], axis=0))
    w_cols = jnp.transpose(jnp.concatenate([wts_ref[...], pad], axis=0))
    p1, p2 = pos_cols[:, 0:1], pos_cols[:, 1:2]
    w1, w2 = w_cols[:, 0:1], w_cols[:, 1:2]
    h = h_ref[...]
    ha = h[:, :HALF]
    hb = h[:, HALF:]
    _wait_rows(tbl_ref[4, 0], copy_into(buf))
    for r0 in range(0, ys_ref.shape[1], SORT_ROWS):
        li = (lax.broadcasted_iota(i32, (tm, SORT_ROWS), 1) + r0).astype(f32)
        sel = (jnp.where(li == p1, w1, 0.0) + jnp.where(li == p2, w2, 0.0)).astype(bf16)
        ya, yb = _unpack_rows(ys_ref[buf, r0:r0 + SORT_ROWS, :])
        ha = ha + _dot(sel, ya.astype(bf16))
        hb = hb + _dot(sel, yb.astype(bf16))
    ms = (jnp.sum(ha * ha, axis=-1, keepdims=True) + jnp.sum(hb * hb, axis=-1, keepdims=True)) * (1.0 / D_MODEL)
    inv = lax.rsqrt(ms + EPS)
    o_ref[:, :HALF] = ha * inv * gfin_ref[:, :HALF]
    o_ref[:, HALF:] = hb * inv * gfin_ref[:, HALF:]


def _combine(tbl, pos, wts, h, g_final, ybuf, tm, rmax):
    n = h.shape[0]
    n_tiles = n // tm
    col = pl.BlockSpec((SUBLANES, tm), lambda i: (0, i))
    return pl.pallas_call(
        _combine_kernel,
        grid=(n_tiles,),
        in_specs=[pl.BlockSpec((SUBLANES, LANES), lambda i: (i, 0), memory_space=pltpu.SMEM),
                  pl.BlockSpec((SUBLANES, LANES), lambda i: (jnp.minimum(i + 1, n_tiles - 1), 0),
                               memory_space=pltpu.SMEM),
                  col, col,
                  pl.BlockSpec((tm, D_MODEL), lambda i: (i, 0)),
                  pl.BlockSpec(g_final.shape, lambda i: (0, 0)),
                  pl.BlockSpec(memory_space=pl.ANY)],
        out_specs=pl.BlockSpec((tm, D_MODEL), lambda i: (i, 0)),
        out_shape=jax.ShapeDtypeStruct((n, D_MODEL), f32),
        scratch_shapes=[pltpu.VMEM((2, rmax, HALF), u32), pltpu.SemaphoreType.DMA((2,))],
        compiler_params=_cparams(("arbitrary",)),
        name="combine",
    )(tbl, tbl, pos, wts, h, g_final, ybuf)


def _tile(n, pref):
    t = pref
    while n % t:
        t //= 2
    return t


def _prep_weights(g_mix, w_in, w_af2, b_af, w_ab2, b_ab, g_head, w_dw, b_dw, g_ln, b_ln, w_pw2, w_out,
                  g_ffn, w_rg, b_rg, w_re, b_re, w_e_gate, w_e_up, w_e_down, g_final):
    n_qkvo = 2 * GLA_KEY + 2 * GLA_VAL
    n_a = 2 * GATE_RANK
    w_in = w_in[0]
    zg = jnp.zeros((GATE_RANK, GLA_KEY), f32)
    wgate = jnp.concatenate([jnp.concatenate([w_af2[0], zg], axis=1), jnp.concatenate([zg, w_ab2[0]], axis=1)], axis=0)
    wdw = jnp.concatenate([w_dw[0], jnp.zeros((1, CONV_CH), f32)], axis=0)
    pad = jnp.zeros((SUBLANES - N_GROUPS, D_MODEL), f32)
    wr = jnp.concatenate([w_re[0].T, w_rg[0].T, pad], axis=0)
    br = jnp.concatenate([b_re[0], b_rg[0], jnp.zeros((SUBLANES - N_GROUPS,), f32)])[:, None]
    return dict(
        g_mix=g_mix[0][None, :],
        wqkvo=w_in[:, :n_qkvo].astype(bf16),
        wa=w_in[:, n_qkvo:n_qkvo + n_a].astype(bf16),
        wc=w_in[:, n_qkvo + n_a:].astype(bf16),
        wgate=wgate.astype(bf16),
        bgate=jnp.concatenate([b_af[0], b_ab[0]])[None, :],
        g_head=g_head[0][None, :],
        wdw=wdw, bdw=b_dw[0][None, :], gln=g_ln[0][None, :], bln=b_ln[0][None, :],
        wpw=w_pw2[0].astype(bf16),
        woa=w_out[0, :GLA_VAL].astype(bf16), wob=w_out[0, GLA_VAL:].astype(bf16),
        g_ffn=g_ffn[0][None, :], wr=wr, br=br,
        wg=w_e_gate[0].astype(bf16), wu=w_e_up[0].astype(bf16), wd=w_e_down[0].astype(bf16),
        g_final=g_final[None, :],
    )


def _trunk(x3, w):
    n_seq, seq_len, _ = x3.shape
    n = n_seq * seq_len
    x = x3.reshape(n, D_MODEL)
    tm = _tile(n, MOE_TILE)
    q, k, v, og, la, glu = _inproj(x, w["g_mix"], w["wqkvo"], w["wa"], w["wc"], w["wgate"], w["bgate"], tm)
    o_f, o_b = _gla(q, k, v, la, n_seq, seq_len)
    oc = _conv(glu, w["wdw"], w["bdw"], w["gln"], w["bln"], w["wpw"], seq_len, _tile(seq_len, 512))
    h, up, eid, wts = _mix(x, o_f, o_b, og, oc, w["g_head"], w["woa"], w["wob"], w["g_ffn"], w["wr"], w["br"], tm)

    n_tiles = n // tm
    max_slots = n * TOP_K + n_tiles * N_EXPERTS * (RUN_ALIGN - 1)
    nb = -(-max_slots // MOE_BLOCK) + N_EXPERTS
    nbp = -(-nb // LANES) * LANES
    rmax = -(-(tm * TOP_K + N_EXPERTS * (RUN_ALIGN - 1)) // SORT_ROWS) * SORT_ROWS
    pos, tbl, blkinfo = _rank(eid, MOE_BLOCK, nbp, tm)
    pos = pos[1]
    tbl = tbl[1].reshape(n_tiles * SUBLANES, LANES)
    xs = _dispatch(tbl, pos, up, jnp.zeros((nb * MOE_BLOCK, HALF), u32), tm, rmax)
    ybuf = _experts(blkinfo[0], blkinfo[1, :1], xs, w["wg"], w["wu"], w["wd"], MOE_BLOCK)
    y = _combine(tbl, pos, wts, h, w["g_final"], ybuf, tm, rmax)
    return y.reshape(x3.shape)


def kernel(x_prompt, x_sample, g_mix, w_in, w_af2, b_af, w_ab2, b_ab, g_head, w_dw, b_dw, g_ln, b_ln, w_pw2,
           w_out, g_ffn, w_rg, b_rg, w_re, b_re, w_e_gate, w_e_up, w_e_down, g_final):
    w = _prep_weights(g_mix, w_in, w_af2, b_af, w_ab2, b_ab, g_head, w_dw, b_dw, g_ln, b_ln, w_pw2, w_out,
                      g_ffn, w_rg, b_rg, w_re, b_re, w_e_gate, w_e_up, w_e_down, g_final)
    return (_trunk(x_prompt, w), _trunk(x_sample, w))
```

```python
import functools

import jax
import jax.numpy as jnp
from jax import lax
from jax.experimental import pallas as pl
from jax.experimental.pallas import tpu as pltpu

f32 = jnp.float32
bf16 = jnp.bfloat16
i32 = jnp.int32
u32 = jnp.uint32

D_MODEL = 1024
GLA_HEADS = 4
GLA_DK = 64
GLA_DV = 128
GLA_KEY = GLA_HEADS * GLA_DK
GLA_VAL = GLA_HEADS * GLA_DV
GATE_RANK = 16
GATE_NORMALIZER = 16
CONV_CH = 512
CONV_K = 31
N_GROUPS = 4
EXPERTS_PER_GROUP = 8
N_EXPERTS = N_GROUPS * EXPERTS_PER_GROUP
TOP_K = 2
D_EXPERT = 512
EPS = 1e-6

LANES = 128
SUBLANES = 8
GLA_CHUNK = 128
GLA_TILE = 256
PAIR_DK = 2 * GLA_DK
PAIR_DV = 2 * GLA_DV
CONV_HALO = 16
CONV_ROWS = 32
HALF = D_MODEL // 2
MOE_TILE = 512
MOE_BLOCK = 512
RUN_ALIGN = SUBLANES
SORT_ROWS = 256
RANK_GROUP = 8
BIG_CHUNK = 32
WAIT_ROWS = 256
VMEM_LIMIT = 56 * 1024 * 1024


def _cparams(sem, **kw):
    return pltpu.CompilerParams(dimension_semantics=sem, vmem_limit_bytes=VMEM_LIMIT, **kw)


def _dot(a, b):
    return jnp.dot(a, b, preferred_element_type=f32)


def _dot_nt(a, b):
    return lax.dot_general(a, b, (((1,), (1,)), ((), ())), preferred_element_type=f32)


def _dot_tn(a, b):
    return lax.dot_general(a, b, (((0,), (0,)), ((), ())), preferred_element_type=f32)


def _split_bf16(x):
    hi = x.astype(bf16)
    lo = (x - hi.astype(f32)).astype(bf16)
    return hi, lo


def _pack_rows(x):
    a = lax.bitcast_convert_type(x[:, :HALF].astype(bf16).astype(f32), u32)
    b = lax.bitcast_convert_type(x[:, HALF:].astype(bf16).astype(f32), u32)
    return a | (b >> 16)


def _unpack_rows(w):
    a = lax.bitcast_convert_type(w & jnp.uint32(0xFFFF0000), f32)
    b = lax.bitcast_convert_type(w << 16, f32)
    return a, b


def _rms(x):
    return x * lax.rsqrt(jnp.mean(x * x, axis=-1, keepdims=True) + EPS)


def _inproj_kernel(x_ref, g_ref, wqkvo_ref, wa_ref, wc_ref, wgate_ref, bgate_ref,
                   q_ref, k_ref, v_ref, og_ref, la_ref, glu_ref):
    u = (_rms(x_ref[...]) * g_ref[...]).astype(bf16)
    p = _dot(u, wqkvo_ref[...])
    q_ref[...] = (p[:, :GLA_KEY] * (GLA_DK ** -0.5)).astype(bf16)
    k_ref[...] = p[:, GLA_KEY:2 * GLA_KEY].astype(bf16)
    v_ref[...] = p[:, 2 * GLA_KEY:2 * GLA_KEY + GLA_VAL].astype(bf16)
    og_ref[...] = p[:, 2 * GLA_KEY + GLA_VAL:].astype(bf16)
    a = _dot(u, wa_ref[...])
    z = _dot(a.astype(bf16), wgate_ref[...]) + bgate_ref[...]
    la_ref[...] = (jnp.minimum(z, 0.0) - jnp.log1p(jnp.exp(-jnp.abs(z)))) * (1.0 / GATE_NORMALIZER)
    c = _dot(u, wc_ref[...])
    glu_ref[...] = (c[:, :CONV_CH] * jax.nn.sigmoid(c[:, CONV_CH:])).astype(bf16)


def _inproj(x, g_mix, wqkvo, wa, wc, wgate, bgate, tm):
    n = x.shape[0]
    row = lambda w: pl.BlockSpec((tm, w), lambda i: (i, 0))
    full = lambda a: pl.BlockSpec(a.shape, lambda i: (0, 0))
    return pl.pallas_call(
        _inproj_kernel,
        grid=(n // tm,),
        in_specs=[row(D_MODEL), full(g_mix), full(wqkvo), full(wa), full(wc), full(wgate), full(bgate)],
        out_specs=[row(GLA_KEY), row(GLA_KEY), row(GLA_VAL), row(GLA_VAL), row(2 * GLA_KEY), row(CONV_CH)],
        out_shape=[jax.ShapeDtypeStruct((n, GLA_KEY), bf16), jax.ShapeDtypeStruct((n, GLA_KEY), bf16),
                   jax.ShapeDtypeStruct((n, GLA_VAL), bf16), jax.ShapeDtypeStruct((n, GLA_VAL), bf16),
                   jax.ShapeDtypeStruct((n, 2 * GLA_KEY), f32), jax.ShapeDtypeStruct((n, CONV_CH), bf16)],
        compiler_params=_cparams(("parallel",)),
        name="inproj",
    )(x, g_mix, wqkvo, wa, wc, wgate, bgate)


def _gla_unit(q_ref, k_ref, v_ref, cs, la, pair, reverse, masks):
    head0_k, head0_v, keep = masks
    tl = cs.shape[0]
    kl = slice(pair * PAIR_DK, (pair + 1) * PAIR_DK)
    vl = slice(pair * PAIR_DV, (pair + 1) * PAIR_DV)
    n_chunks = tl // GLA_CHUNK
    order = list(range(n_chunks - 1, -1, -1) if reverse else range(n_chunks))

    def stack_heads(x, head0):
        z = jnp.zeros_like(x)
        return jnp.concatenate([jnp.where(head0, x, z), jnp.where(head0, z, x)], axis=0)

    local = {}
    for j in order:
        rows = slice(j * GLA_CHUNK, (j + 1) * GLA_CHUNK)
        cj = cs[rows, kl]
        q = q_ref[rows, kl].astype(f32)
        k = k_ref[rows, kl].astype(f32)
        c_last = cj[GLA_CHUNK - 1:GLA_CHUNK, :]
        mid = 0.5 * c_last
        rel = (mid - (cj - la[rows, kl])) if reverse else (cj - mid)
        qd = q * jnp.exp(rel)
        kd = k * jnp.exp(-rel)
        edge = jnp.exp(mid)
        qi = qd * edge
        ku = kd * edge
        vbd = stack_heads(v_ref[rows, vl], head0_v)
        sc = _dot_nt(qd.astype(bf16), stack_heads(kd.astype(bf16), head0_k))
        p = jnp.where(keep, sc, 0.0).astype(bf16)
        kv = _dot_tn(stack_heads(ku.astype(bf16), head0_k), vbd)
        local[j] = (p, vbd, qi.astype(bf16), kv, c_last)
    return order, vl, local


def _gla_finish(o_ref, st_ref, order, vl, local):
    row_id = lax.broadcasted_iota(i32, (PAIR_DK, PAIR_DK), 0)
    c_rows = jnp.zeros((PAIR_DK, PAIR_DK), f32)
    for j in order:
        c_rows = jnp.where(row_id == j, local[j][4], c_rows)
    decay_cols = jnp.exp(jnp.transpose(c_rows))
    st = st_ref[...]
    st_before = {}
    for j in order:
        st_before[j] = st.astype(bf16)
        st = decay_cols[:, j:j + 1] * st + local[j][3]
    st_ref[...] = st
    for j in order:
        p, vbd, qi, _, _ = local[j]
        o = _dot(p, vbd) + _dot(qi, st_before[j])
        o_ref[j * GLA_CHUNK:(j + 1) * GLA_CHUNK, vl] = o.astype(o_ref.dtype)


def _gla_kernel(qf, kf, vf, laf, qb, kb, vb, lab, of, ob, st_ref, tri_ref):
    tl = qf.shape[0]

    @pl.when(pl.program_id(1) == 0)
    def _():
        st_ref[...] = jnp.zeros_like(st_ref)
        r = lax.broadcasted_iota(i32, (tl, tl), 0)
        c = lax.broadcasted_iota(i32, (tl, tl), 1)
        tri_ref[...] = jnp.where((r // GLA_CHUNK == c // GLA_CHUNK) & (c <= r), 1.0, 0.0).astype(bf16)

    tri = tri_ref[...]
    head0_k = lax.broadcasted_iota(i32, (GLA_CHUNK, PAIR_DK), 1) < GLA_DK
    head0_v = lax.broadcasted_iota(i32, (GLA_CHUNK, PAIR_DV), 1) < GLA_DV
    t_idx = lax.broadcasted_iota(i32, (GLA_CHUNK, 2 * GLA_CHUNK), 0)
    s_idx = lax.broadcasted_iota(i32, (GLA_CHUNK, 2 * GLA_CHUNK), 1) % GLA_CHUNK
    units = []
    for d, (q_ref, k_ref, v_ref, la_ref, o_ref) in enumerate(((qf, kf, vf, laf, of), (qb, kb, vb, lab, ob))):
        reverse = d == 1
        la = la_ref[...]
        la_hi, la_lo = _split_bf16(la)
        cs = _dot(tri, la_hi) + _dot(tri, la_lo)
        keep = (s_idx > t_idx) if reverse else (s_idx <= t_idx)
        for pair in range(GLA_HEADS // 2):
            units.append((o_ref, st_ref.at[d, pair]) + _gla_unit(q_ref, k_ref, v_ref, cs, la, pair, reverse,
                                                                  (head0_k, head0_v, keep)))
    for unit in units:
        _gla_finish(*unit)


def _gla(q, k, v, la, n_seq, seq_len):
    n = q.shape[0]
    nblk = seq_len // GLA_TILE
    fwd = lambda b, i: (b * nblk + i, 0)
    bwd = lambda b, i: (b * nblk + nblk - 1 - i, 0)
    bwd_la = lambda b, i: (b * nblk + nblk - 1 - i, 1)
    kq = lambda im: pl.BlockSpec((GLA_TILE, GLA_KEY), im)
    vv = lambda im: pl.BlockSpec((GLA_TILE, GLA_VAL), im)
    return pl.pallas_call(
        _gla_kernel,
        grid=(n_seq, nblk),
        in_specs=[kq(fwd), kq(fwd), vv(fwd), kq(fwd), kq(bwd), kq(bwd), vv(bwd), kq(bwd_la)],
        out_specs=[vv(fwd), vv(bwd)],
        out_shape=[jax.ShapeDtypeStruct((n, GLA_VAL), bf16), jax.ShapeDtypeStruct((n, GLA_VAL), bf16)],
        scratch_shapes=[pltpu.VMEM((2, GLA_HEADS // 2, PAIR_DK, PAIR_DV), f32), pltpu.VMEM((GLA_TILE, GLA_TILE), bf16)],
        compiler_params=_cparams(("parallel", "arbitrary")),
        name="gla",
    )(q, k, v, la, q, k, v, la)


def _conv_kernel(prev_ref, cur_ref, next_ref, wdw_ref, bdw_ref, gln_ref, bln_ref, wpw_ref,
                 o_ref, ext_ref, y_ref, *, tiles_per_seq):
    tl = cur_ref.shape[0]
    pos = pl.program_id(0) % tiles_per_seq
    prev = jnp.where(pos == 0, 0.0, prev_ref[...].astype(f32))
    nxt = jnp.where(pos == tiles_per_seq - 1, 0.0, next_ref[...].astype(f32))
    cur = cur_ref[...].astype(f32)
    n_slabs = CONV_CH // LANES
    for c in range(n_slabs):
        lanes = slice(c * LANES, (c + 1) * LANES)
        ext_ref[c, 0:CONV_HALO, :] = prev[:, lanes]
        ext_ref[c, CONV_HALO:CONV_HALO + tl, :] = cur[:, lanes]
        ext_ref[c, CONV_HALO + tl:, :] = nxt[:, lanes]
    off = CONV_HALO - CONV_K // 2

    def body(rb, carry):
        base = pl.multiple_of(rb * CONV_ROWS, CONV_ROWS)
        for c in range(n_slabs):
            lanes = slice(c * LANES, (c + 1) * LANES)
            a = jnp.zeros((CONV_ROWS, LANES), f32) + bdw_ref[:, lanes]
            for j in range(CONV_K):
                taps = ext_ref.at[c, pl.ds(off + j, tl)]
                a = a + taps[pl.ds(base, CONV_ROWS), :] * wdw_ref[j:j + 1, lanes]
            y_ref[pl.ds(base, CONV_ROWS), lanes] = a
        return carry

    lax.fori_loop(0, tl // CONV_ROWS, body, 0)
    acc = y_ref[...]
    mu = jnp.mean(acc, axis=-1, keepdims=True)
    xc = acc - mu
    yn = xc * lax.rsqrt(jnp.mean(xc * xc, axis=-1, keepdims=True) + EPS) * gln_ref[...] + bln_ref[...]
    o_ref[...] = _dot((yn * jax.nn.sigmoid(yn)).astype(bf16), wpw_ref[...]).astype(bf16)


def _conv(glu, wdw, bdw, gln, bln, wpw, seq_len, tl):
    n = glu.shape[0]
    hb = tl // CONV_HALO
    n_halo = n // CONV_HALO
    full = lambda a: pl.BlockSpec(a.shape, lambda i: (0, 0))
    return pl.pallas_call(
        functools.partial(_conv_kernel, tiles_per_seq=seq_len // tl),
        grid=(n // tl,),
        in_specs=[pl.BlockSpec((CONV_HALO, CONV_CH), lambda i: (jnp.maximum(i * hb - 1, 0), 0)),
                  pl.BlockSpec((tl, CONV_CH), lambda i: (i, 0)),
                  pl.BlockSpec((CONV_HALO, CONV_CH), lambda i: (jnp.minimum((i + 1) * hb, n_halo - 1), 0)),
                  full(wdw), full(bdw), full(gln), full(bln), full(wpw)],
        out_specs=pl.BlockSpec((tl, CONV_CH), lambda i: (i, 0)),
        out_shape=jax.ShapeDtypeStruct((n, CONV_CH), bf16),
        scratch_shapes=[pltpu.VMEM((CONV_CH // LANES, tl + 2 * CONV_HALO, LANES), f32),
                        pltpu.VMEM((tl, CONV_CH), f32)],
        compiler_params=_cparams(("parallel",)),
        name="conv",
    )(glu, glu, glu, wdw, bdw, gln, bln, wpw)


def _mix_kernel(x_ref, of_ref, ob_ref, og_ref, oc_ref, gh_ref, woa_ref, wob_ref, gf_ref, wr_ref, br_ref,
                h_ref, up_ref, eid_ref, wts_ref):
    tm = x_ref.shape[0]
    o = of_ref[...].astype(f32) + ob_ref[...].astype(f32)
    o = jnp.concatenate([_rms(o[:, h * GLA_DV:(h + 1) * GLA_DV]) for h in range(GLA_HEADS)], axis=1)
    og = og_ref[...].astype(f32)
    o = o * gh_ref[...] * (og * jax.nn.sigmoid(og))
    h = x_ref[...] + _dot(o.astype(bf16), woa_ref[...]) + _dot(oc_ref[...], wob_ref[...])
    h_ref[...] = h
    u = _rms(h) * gf_ref[...]
    u_hi, u_lo = _split_bf16(u)
    up_ref[...] = u_hi

    w_hi, w_lo = _split_bf16(wr_ref[...])
    logits = _dot_nt(w_hi, u_hi) + _dot_nt(w_lo, u_hi) + _dot_nt(w_hi, u_lo) + br_ref[...]
    le = logits[0:N_EXPERTS]
    lg = logits[N_EXPERTS:N_EXPERTS + N_GROUPS]
    gmax = jnp.max(lg, axis=0, keepdims=True)
    gi = lax.broadcasted_iota(i32, lg.shape, 0)
    gsel = jnp.min(jnp.where(lg == gmax, gi, N_GROUPS), axis=0, keepdims=True)
    gate = 1.0 / jnp.sum(jnp.exp(lg - gmax), axis=0, keepdims=True)
    ri = lax.broadcasted_iota(i32, le.shape, 0)
    lm = jnp.where(ri // EXPERTS_PER_GROUP == gsel, le, -jnp.inf)
    m1 = jnp.max(lm, axis=0, keepdims=True)
    i1 = jnp.min(jnp.where(lm == m1, ri, N_EXPERTS), axis=0, keepdims=True)
    lm2 = jnp.where(ri == i1, -jnp.inf, lm)
    m2 = jnp.max(lm2, axis=0, keepdims=True)
    i2 = jnp.min(jnp.where(lm2 == m2, ri, N_EXPERTS), axis=0, keepdims=True)
    t = jnp.exp(m2 - m1)
    den = 1.0 / (1.0 + t)
    r8 = lax.broadcasted_iota(i32, (SUBLANES, tm), 0)
    eid_ref[...] = jnp.where(r8 == 0, i1, jnp.where(r8 == 1, i2, 0))
    wts_ref[...] = jnp.where(r8 == 0, gate * den, jnp.where(r8 == 1, gate * t * den, 0.0))


def _mix(x, o_f, o_b, og, oc, g_head, woa, wob, g_ffn, wr, br, tm):
    n = x.shape[0]
    row = lambda w: pl.BlockSpec((tm, w), lambda i: (i, 0))
    col = pl.BlockSpec((SUBLANES, tm), lambda i: (0, i))
    full = lambda a: pl.BlockSpec(a.shape, lambda i: (0, 0))
    return pl.pallas_call(
        _mix_kernel,
        grid=(n // tm,),
        in_specs=[row(D_MODEL), row(GLA_VAL), row(GLA_VAL), row(GLA_VAL), row(CONV_CH),
                  full(g_head), full(woa), full(wob), full(g_ffn), full(wr), full(br)],
        out_specs=[row(D_MODEL), row(D_MODEL), col, col],
        out_shape=[jax.ShapeDtypeStruct((n, D_MODEL), f32), jax.ShapeDtypeStruct((n, D_MODEL), bf16),
                   jax.ShapeDtypeStruct((SUBLANES, n), i32), jax.ShapeDtypeStruct((SUBLANES, n), f32)],
        compiler_params=_cparams(("parallel",)),
        name="mix",
    )(x, o_f, o_b, og, oc, g_head, woa, wob, g_ffn, wr, br)


def _rank_kernel(eid_ref, pos_ref, tbl_ref, blk_ref, tail_ref, cnt_ref, base_ref, end_ref, *, blk, nbp, tm):
    ps = pl.program_id(0)
    i = pl.program_id(1)
    rr = lax.broadcasted_iota(i32, (N_EXPERTS, LANES), 0)

    def cumsum_experts(x):
        for s in (1, 2, 4, 8, 16):
            x = x + jnp.where(rr >= s, pltpu.roll(x, s, axis=0), 0.0)
        return x

    @pl.when((ps == 0) & (i == 0))
    def _():
        cnt_ref[...] = jnp.zeros_like(cnt_ref)

    @pl.when(ps == 0)
    def _():
        pos_ref[...] = jnp.zeros_like(pos_ref)
        tbl_ref[...] = jnp.zeros_like(tbl_ref)

    @pl.when((ps == 1) & (i == 0))
    def _():
        cnt = cnt_ref[...]
        pc = jnp.floor((cnt + (blk - 1)) * (1.0 / blk)) * blk
        inc = cumsum_experts(pc)
        base_ref[...] = inc - pc
        end_ref[...] = inc
        pend = jnp.concatenate([inc] * (nbp // LANES), axis=1)
        jl = lax.broadcasted_iota(i32, (N_EXPERTS, nbp), 1).astype(f32) * blk
        be = jnp.minimum(jnp.sum(jnp.where(pend <= jl, 1.0, 0.0), axis=0, keepdims=True), N_EXPERTS - 1.0)
        nused = jnp.concatenate([inc[N_EXPERTS - 1:N_EXPERTS, :]] * (nbp // LANES), axis=1) * (1.0 / blk)
        r8 = lax.broadcasted_iota(i32, (SUBLANES, nbp), 0)
        blk_ref[...] = jnp.where(r8 == 0, be, jnp.where(r8 == 1, nused, 0.0)).astype(i32)

    ri = lax.broadcasted_iota(i32, (N_EXPERTS, tm), 0)
    for t in range(eid_ref.shape[1] // tm):
        lanes = slice(t * tm, (t + 1) * tm)
        oh1 = ri == eid_ref[0:1, lanes]
        oh2 = ri == eid_ref[1:2, lanes]
        ohf = jnp.where(oh1 | oh2, 1.0, 0.0)
        tile_cnt = jnp.sum(ohf, axis=1, keepdims=True)
        run_len = (jnp.floor((tile_cnt + (RUN_ALIGN - 1)) * (1.0 / RUN_ALIGN)) * RUN_ALIGN
                   + jnp.zeros((N_EXPERTS, LANES), f32))

        @pl.when(ps == 0)
        def _():
            cnt_ref[...] += run_len

        @pl.when(ps == 1)
        def _():
            a = lax.broadcasted_iota(i32, (tm, tm), 0)
            b = lax.broadcasted_iota(i32, (tm, tm), 1)
            upper = jnp.where(a < b, 1.0, 0.0).astype(bf16)
            before = _dot(ohf.astype(bf16), upper)
            run_start = cumsum_experts(run_len) - run_len
            row = run_start[:, 0:1] + before
            p1 = jnp.sum(jnp.where(oh1, row, 0.0), axis=0, keepdims=True)
            p2 = jnp.sum(jnp.where(oh2, row, 0.0), axis=0, keepdims=True)
            r8 = lax.broadcasted_iota(i32, (SUBLANES, tm), 0)
            pos_ref[0, :, lanes] = jnp.where(r8 == 0, p1, jnp.where(r8 == 1, p2, 0.0)).astype(i32)
            diag = lax.broadcasted_iota(i32, (N_EXPERTS, LANES), 1) == rr
            on_lanes = lambda x: jnp.sum(jnp.where(diag, x, 0.0), axis=0, keepdims=True)
            big = jnp.floor(run_len * (1.0 / BIG_CHUNK))
            small = (run_len - big * BIG_CHUNK) * (1.0 / RUN_ALIGN)
            rows = (on_lanes(run_start), on_lanes(base_ref[...]), on_lanes(big), on_lanes(small),
                    jnp.sum(run_len, axis=0, keepdims=True))
            t8 = lax.broadcasted_iota(i32, (SUBLANES, LANES), 0)
            tbl = jnp.zeros((SUBLANES, LANES), f32)
            for k, x in enumerate(rows):
                tbl = jnp.where(t8 == k, x, tbl)
            tbl_ref[0, t] = tbl.astype(i32)
            base_ref[...] += run_len

    @pl.when((ps == 1) & (i == pl.num_programs(1) - 1))
    def _():
        diag = lax.broadcasted_iota(i32, (N_EXPERTS, LANES), 1) == rr
        on_lanes = lambda x: jnp.sum(jnp.where(diag, x, 0.0), axis=0, keepdims=True)
        first = base_ref[...]
        count = end_ref[...] - first
        rows = (on_lanes(first), on_lanes(count), jnp.sum(count, axis=0, keepdims=True),
                end_ref[N_EXPERTS - 1:N_EXPERTS, :] * (1.0 / blk))
        t8 = lax.broadcasted_iota(i32, (SUBLANES, LANES), 0)
        tail = jnp.zeros((SUBLANES, LANES), f32)
        for k, x in enumerate(rows):
            tail = jnp.where(t8 == k, x, tail)
        tail_ref[...] = tail.astype(i32)


def _rank(eid, blk, nbp, tm):
    n = eid.shape[1]
    group = _tile(n // tm, RANK_GROUP)
    return pl.pallas_call(
        functools.partial(_rank_kernel, blk=blk, nbp=nbp, tm=tm),
        grid=(2, n // (tm * group)),
        in_specs=[pl.BlockSpec((SUBLANES, tm * group), lambda p, i: (0, i))],
        out_specs=[pl.BlockSpec((1, SUBLANES, tm * group), lambda p, i: (p, 0, i)),
                   pl.BlockSpec((1, group, SUBLANES, LANES), lambda p, i: (p, i, 0, 0)),
                   pl.BlockSpec((SUBLANES, nbp), lambda p, i: (0, 0)),
                   pl.BlockSpec((SUBLANES, LANES), lambda p, i: (0, 0))],
        out_shape=[jax.ShapeDtypeStruct((2, SUBLANES, n), i32),
                   jax.ShapeDtypeStruct((2, n // tm, SUBLANES, LANES), i32),
                   jax.ShapeDtypeStruct((SUBLANES, nbp), i32),
                   jax.ShapeDtypeStruct((SUBLANES, LANES), i32)],
        scratch_shapes=[pltpu.VMEM((N_EXPERTS, LANES), f32)] * 3,
        compiler_params=_cparams(("arbitrary", "arbitrary")),
        name="rank",
    )(eid)


def _start_run_copies(tbl_ref, copy):
    def per_expert(e, carry):
        sorted0 = tbl_ref[0, e]
        slot0 = tbl_ref[1, e]
        n_big = tbl_ref[2, e]

        def pieces(rows, first):
            def body(c, carry2):
                off = first + c * rows
                copy(pl.multiple_of(sorted0 + off, RUN_ALIGN), pl.multiple_of(slot0 + off, RUN_ALIGN), rows).start()
                return carry2
            return body

        lax.fori_loop(0, n_big, pieces(BIG_CHUNK, 0), 0)
        lax.fori_loop(0, tbl_ref[3, e], pieces(RUN_ALIGN, n_big * BIG_CHUNK), 0)
        return carry

    lax.fori_loop(0, N_EXPERTS, per_expert, 0)


def _wait_rows(total_rows, copy):
    def waits(rows):
        def body(c, carry):
            copy(0, 0, rows).wait()
            return carry
        return body

    n_wide = total_rows // WAIT_ROWS
    lax.fori_loop(0, n_wide, waits(WAIT_ROWS), 0)
    lax.fori_loop(0, (total_rows - n_wide * WAIT_ROWS) // RUN_ALIGN, waits(RUN_ALIGN), 0)


def _dispatch_kernel(tbl_ref, tail_ref, pos_ref, u_ref, xs_ref, sorted_ref, zero_ref, sems, pending_ref):
    i = pl.program_id(0)
    tm = u_ref.shape[0]
    buf = i % 2

    def copy_from(b):
        def copy(sorted_row, slot_row, rows):
            return pltpu.make_async_copy(sorted_ref.at[b, pl.ds(sorted_row, rows)],
                                         xs_ref.at[pl.ds(slot_row, rows)], sems.at[b])
        return copy

    @pl.when(i == 0)
    def _():
        pending_ref[0] = 0
        pending_ref[1] = 0

    _wait_rows(pending_ref[buf], copy_from(buf))
    p1 = pos_ref[0:1, :]
    p2 = pos_ref[1:2, :]
    u = u_ref[...]
    for r0 in range(0, sorted_ref.shape[1], SORT_ROWS):
        ri = lax.broadcasted_iota(i32, (SORT_ROWS, tm), 0) + r0
        sel = jnp.where((ri == p1) | (ri == p2), 1.0, 0.0).astype(bf16)
        sorted_ref[buf, r0:r0 + SORT_ROWS, :] = _pack_rows(_dot(sel, u))
    _start_run_copies(tbl_ref, copy_from(buf))
    pending_ref[buf] = tbl_ref[4, 0]

    @pl.when(i == pl.num_programs(0) - 1)
    def _():
        for b in range(2):
            _wait_rows(pending_ref[b], copy_from(b))
        blk = zero_ref.shape[0]
        n_blocks = xs_ref.shape[0] // blk
        zero_ref[...] = jnp.zeros_like(zero_ref)

        def zero_copy(unused_row, slot_row, rows):
            return pltpu.make_async_copy(zero_ref.at[pl.ds(0, rows)], xs_ref.at[pl.ds(slot_row, rows)], sems.at[2])

        def per_expert(e, carry):
            first = tail_ref[0, e]
            n_big = tail_ref[1, e] // BIG_CHUNK

            def pieces(rows, start):
                def body(c, carry2):
                    zero_copy(0, pl.multiple_of(first + start + c * rows, RUN_ALIGN), rows).start()
                    return carry2
                return body

            lax.fori_loop(0, n_big, pieces(BIG_CHUNK, 0), 0)
            lax.fori_loop(0, (tail_ref[1, e] - n_big * BIG_CHUNK) // RUN_ALIGN, pieces(RUN_ALIGN, n_big * BIG_CHUNK), 0)
            return carry

        lax.fori_loop(0, N_EXPERTS, per_expert, 0)
        n_used = tail_ref[3, 0]

        def per_block(j, carry):
            zero_copy(0, pl.multiple_of(j * blk, blk), blk).start()
            return carry

        lax.fori_loop(n_used, n_blocks, per_block, 0)
        _wait_rows(tail_ref[2, 0] + (n_blocks - n_used) * blk, zero_copy)


def _dispatch(tbl, tail, pos, up, n_slots, tm, rmax):
    n = up.shape[0]
    return pl.pallas_call(
        _dispatch_kernel,
        grid=(n // tm,),
        in_specs=[pl.BlockSpec((SUBLANES, LANES), lambda i: (i, 0), memory_space=pltpu.SMEM),
                  pl.BlockSpec((SUBLANES, LANES), lambda i: (0, 0), memory_space=pltpu.SMEM),
                  pl.BlockSpec((SUBLANES, tm), lambda i: (0, i)),
                  pl.BlockSpec((tm, D_MODEL), lambda i: (i, 0))],
        out_specs=pl.BlockSpec(memory_space=pl.ANY),
        out_shape=jax.ShapeDtypeStruct((n_slots, HALF), u32),
        scratch_shapes=[pltpu.VMEM((2, rmax, HALF), u32), pltpu.VMEM((MOE_BLOCK, HALF), u32),
                        pltpu.SemaphoreType.DMA((3,)), pltpu.SMEM((2,), i32)],
        compiler_params=_cparams(("arbitrary",)),
        name="dispatch",
    )(tbl, tail, pos, up)


def _expert_kernel(be_ref, nu_ref, xs_ref, wg_ref, wu_ref, wd_ref, y_ref):
    j = pl.program_id(0)

    @pl.when(j < nu_ref[0])
    def _():
        xa, xb = _unpack_rows(xs_ref[...])
        xa = xa.astype(bf16)
        xb = xb.astype(bf16)
        g = _dot(xa, wg_ref[0, :HALF, :]) + _dot(xb, wg_ref[0, HALF:, :])
        u = _dot(xa, wu_ref[0, :HALF, :]) + _dot(xb, wu_ref[0, HALF:, :])
        hb = (g * jax.nn.sigmoid(g) * u).astype(bf16)
        y_ref[...] = _pack_rows(_dot(hb, wd_ref[0]))

    @pl.when(j >= nu_ref[0])
    def _():
        y_ref[...] = jnp.zeros_like(y_ref)


def _experts(block_e, nused, xs, wg, wu, wd, blk):
    nb = xs.shape[0] // blk
    used = lambda j, be, nu: (jnp.minimum(j, nu[0] - 1), 0)
    return pl.pallas_call(
        _expert_kernel,
        grid_spec=pltpu.PrefetchScalarGridSpec(
            num_scalar_prefetch=2,
            grid=(nb,),
            in_specs=[pl.BlockSpec((blk, HALF), used),
                      pl.BlockSpec((1, D_MODEL, D_EXPERT), lambda j, be, nu: (be[j], 0, 0)),
                      pl.BlockSpec((1, D_MODEL, D_EXPERT), lambda j, be, nu: (be[j], 0, 0)),
                      pl.BlockSpec((1, D_EXPERT, D_MODEL), lambda j, be, nu: (be[j], 0, 0))],
            out_specs=pl.BlockSpec((blk, HALF), lambda j, be, nu: (j, 0)),
        ),
        out_shape=jax.ShapeDtypeStruct(xs.shape, u32),
        compiler_params=_cparams(("arbitrary",)),
        name="experts",
    )(block_e, nused, xs, wg, wu, wd)


def _combine_kernel(tbl_ref, tbl_next_ref, pos_ref, wts_ref, h_ref, gfin_ref, y_ref, o_ref, ys_ref, sems):
    i = pl.program_id(0)
    tm = h_ref.shape[0]
    buf = i % 2

    def copy_into(b):
        def copy(sorted_row, slot_row, rows):
            return pltpu.make_async_copy(y_ref.at[pl.ds(slot_row, rows)],
                                         ys_ref.at[b, pl.ds(sorted_row, rows)], sems.at[b])
        return copy

    @pl.when(i == 0)
    def _():
        ys_ref[...] = jnp.zeros_like(ys_ref)
        _start_run_copies(tbl_ref, copy_into(0))

    @pl.when(i + 1 < pl.num_programs(0))
    def _():
        _start_run_copies(tbl_next_ref, copy_into(1 - buf))

    pad = jnp.zeros((LANES - SUBLANES, tm), f32)
    pos_cols = jnp.transpose(jnp.concatenate([pos_ref[...].astype(f32), pad], axis=0))
    w_cols = jnp.transpose(jnp.concatenate([wts_ref[...], pad], axis=0))
    p1, p2 = pos_cols[:, 0:1], pos_cols[:, 1:2]
    w1, w2 = w_cols[:, 0:1], w_cols[:, 1:2]
    h = h_ref[...]
    ha = h[:, :HALF]
    hb = h[:, HALF:]
    _wait_rows(tbl_ref[4, 0], copy_into(buf))
    for r0 in range(0, ys_ref.shape[1], SORT_ROWS):
        li = (lax.broadcasted_iota(i32, (tm, SORT_ROWS), 1) + r0).astype(f32)
        sel = (jnp.where(li == p1, w1, 0.0) + jnp.where(li == p2, w2, 0.0)).astype(bf16)
        ya, yb = _unpack_rows(ys_ref[buf, r0:r0 + SORT_ROWS, :])
        ha = ha + _dot(sel, ya.astype(bf16))
        hb = hb + _dot(sel, yb.astype(bf16))
    ms = (jnp.sum(ha * ha, axis=-1, keepdims=True) + jnp.sum(hb * hb, axis=-1, keepdims=True)) * (1.0 / D_MODEL)
    inv = lax.rsqrt(ms + EPS)
    o_ref[:, :HALF] = ha * inv * gfin_ref[:, :HALF]
    o_ref[:, HALF:] = hb * inv * gfin_ref[:, HALF:]


def _combine(tbl, pos, wts, h, g_final, ybuf, tm, rmax):
    n = h.shape[0]
    n_tiles = n // tm
    col = pl.BlockSpec((SUBLANES, tm), lambda i: (0, i))
    return pl.pallas_call(
        _combine_kernel,
        grid=(n_tiles,),
        in_specs=[pl.BlockSpec((SUBLANES, LANES), lambda i: (i, 0), memory_space=pltpu.SMEM),
                  pl.BlockSpec((SUBLANES, LANES), lambda i: (jnp.minimum(i + 1, n_tiles - 1), 0),
                               memory_space=pltpu.SMEM),
                  col, col,
                  pl.BlockSpec((tm, D_MODEL), lambda i: (i, 0)),
                  pl.BlockSpec(g_final.shape, lambda i: (0, 0)),
                  pl.BlockSpec(memory_space=pl.ANY)],
        out_specs=pl.BlockSpec((tm, D_MODEL), lambda i: (i, 0)),
        out_shape=jax.ShapeDtypeStruct((n, D_MODEL), f32),
        scratch_shapes=[pltpu.VMEM((2, rmax, HALF), u32), pltpu.SemaphoreType.DMA((2,))],
        compiler_params=_cparams(("arbitrary",)),
        name="combine",
    )(tbl, tbl, pos, wts, h, g_final, ybuf)


def _tile(n, pref):
    t = pref
    while n % t:
        t //= 2
    return t


def _prep_weights(g_mix, w_in, w_af2, b_af, w_ab2, b_ab, g_head, w_dw, b_dw, g_ln, b_ln, w_pw2, w_out,
                  g_ffn, w_rg, b_rg, w_re, b_re, w_e_gate, w_e_up, w_e_down, g_final):
    n_qkvo = 2 * GLA_KEY + 2 * GLA_VAL
    n_a = 2 * GATE_RANK
    w_in = w_in[0]
    zg = jnp.zeros((GATE_RANK, GLA_KEY), f32)
    wgate = jnp.concatenate([jnp.concatenate([w_af2[0], zg], axis=1), jnp.concatenate([zg, w_ab2[0]], axis=1)], axis=0)
    wdw = jnp.concatenate([w_dw[0], jnp.zeros((1, CONV_CH), f32)], axis=0)
    pad = jnp.zeros((SUBLANES - N_GROUPS, D_MODEL), f32)
    wr = jnp.concatenate([w_re[0].T, w_rg[0].T, pad], axis=0)
    br = jnp.concatenate([b_re[0], b_rg[0], jnp.zeros((SUBLANES - N_GROUPS,), f32)])[:, None]
    return dict(
        g_mix=g_mix[0][None, :],
        wqkvo=w_in[:, :n_qkvo].astype(bf16),
        wa=w_in[:, n_qkvo:n_qkvo + n_a].astype(bf16),
        wc=w_in[:, n_qkvo + n_a:].astype(bf16),
        wgate=wgate.astype(bf16),
        bgate=jnp.concatenate([b_af[0], b_ab[0]])[None, :],
        g_head=g_head[0][None, :],
        wdw=wdw, bdw=b_dw[0][None, :], gln=g_ln[0][None, :], bln=b_ln[0][None, :],
        wpw=w_pw2[0].astype(bf16),
        woa=w_out[0, :GLA_VAL].astype(bf16), wob=w_out[0, GLA_VAL:].astype(bf16),
        g_ffn=g_ffn[0][None, :], wr=wr, br=br,
        wg=w_e_gate[0].astype(bf16), wu=w_e_up[0].astype(bf16), wd=w_e_down[0].astype(bf16),
        g_final=g_final[None, :],
    )


def _trunk(x3, w):
    n_seq, seq_len, _ = x3.shape
    n = n_seq * seq_len
    x = x3.reshape(n, D_MODEL)
    tm = _tile(n, MOE_TILE)
    q, k, v, og, la, glu = _inproj(x, w["g_mix"], w["wqkvo"], w["wa"], w["wc"], w["wgate"], w["bgate"],
                                   _tile(n, 1024))
    o_f, o_b = _gla(q, k, v, la, n_seq, seq_len)
    oc = _conv(glu, w["wdw"], w["bdw"], w["gln"], w["bln"], w["wpw"], seq_len, _tile(seq_len, 512))
    h, up, eid, wts = _mix(x, o_f, o_b, og, oc, w["g_head"], w["woa"], w["wob"], w["g_ffn"], w["wr"], w["br"], tm)

    n_tiles = n // tm
    max_slots = n * TOP_K + n_tiles * N_EXPERTS * (RUN_ALIGN - 1)
    nb = -(-max_slots // MOE_BLOCK) + N_EXPERTS
    nbp = -(-nb // LANES) * LANES
    rmax = -(-(tm * TOP_K + N_EXPERTS * (RUN_ALIGN - 1)) // SORT_ROWS) * SORT_ROWS
    pos, tbl, blkinfo, tail = _rank(eid, MOE_BLOCK, nbp, tm)
    pos = pos[1]
    tbl = tbl[1].reshape(n_tiles * SUBLANES, LANES)
    xs = _dispatch(tbl, tail, pos, up, nb * MOE_BLOCK, tm, rmax)
    ybuf = _experts(blkinfo[0], blkinfo[1, :1], xs, w["wg"], w["wu"], w["wd"], MOE_BLOCK)
    y = _combine(tbl, pos, wts, h, w["g_final"], ybuf, tm, rmax)
    return y.reshape(x3.shape)


def kernel(x_prompt, x_sample, g_mix, w_in, w_af2, b_af, w_ab2, b_ab, g_head, w_dw, b_dw, g_ln, b_ln, w_pw2,
           w_out, g_ffn, w_rg, b_rg, w_re, b_re, w_e_gate, w_e_up, w_e_down, g_final):
    w = _prep_weights(g_mix, w_in, w_af2, b_af, w_ab2, b_ab, g_head, w_dw, b_dw, g_ln, b_ln, w_pw2, w_out,
                      g_ffn, w_rg, b_rg, w_re, b_re, w_e_gate, w_e_up, w_e_down, g_final)
    return (_trunk(x_prompt, w), _trunk(x_sample, w))
```

```python
import functools

import jax
import jax.numpy as jnp
from jax import lax
from jax.experimental import pallas as pl
from jax.experimental.pallas import tpu as pltpu

f32 = jnp.float32
bf16 = jnp.bfloat16
i32 = jnp.int32
u32 = jnp.uint32

D_MODEL = 1024
GLA_HEADS = 4
GLA_DK = 64
GLA_DV = 128
GLA_KEY = GLA_HEADS * GLA_DK
GLA_VAL = GLA_HEADS * GLA_DV
GATE_RANK = 16
GATE_NORMALIZER = 16
CONV_CH = 512
CONV_K = 31
N_GROUPS = 4
EXPERTS_PER_GROUP = 8
N_EXPERTS = N_GROUPS * EXPERTS_PER_GROUP
TOP_K = 2
D_EXPERT = 512
EPS = 1e-6

LANES = 128
SUBLANES = 8
GLA_CHUNK = 128
GLA_TILE = 1024
GLA_CUM_ROWS = 256
PAIR_DK = 2 * GLA_DK
PAIR_DV = 2 * GLA_DV
CONV_HALO = 16
CONV_ROWS = 32
HALF = D_MODEL // 2
MOE_TILE = 512
MOE_BLOCK = 512
RUN_ALIGN = SUBLANES
SORT_ROWS = 256
RANK_GROUP = 8
BIG_CHUNK = 32
WAIT_ROWS = 256
VMEM_LIMIT = 56 * 1024 * 1024


def _cparams(sem, **kw):
    return pltpu.CompilerParams(dimension_semantics=sem, vmem_limit_bytes=VMEM_LIMIT, **kw)


def _dot(a, b):
    return jnp.dot(a, b, preferred_element_type=f32)


def _dot_nt(a, b):
    return lax.dot_general(a, b, (((1,), (1,)), ((), ())), preferred_element_type=f32)


def _dot_tn(a, b):
    return lax.dot_general(a, b, (((0,), (0,)), ((), ())), preferred_element_type=f32)


def _split_bf16(x):
    hi = x.astype(bf16)
    lo = (x - hi.astype(f32)).astype(bf16)
    return hi, lo


def _pack_rows(x):
    a = lax.bitcast_convert_type(x[:, :HALF].astype(bf16).astype(f32), u32)
    b = lax.bitcast_convert_type(x[:, HALF:].astype(bf16).astype(f32), u32)
    return a | (b >> 16)


def _unpack_rows(w):
    a = lax.bitcast_convert_type(w & jnp.uint32(0xFFFF0000), f32)
    b = lax.bitcast_convert_type(w << 16, f32)
    return a, b


def _rms(x):
    return x * lax.rsqrt(jnp.mean(x * x, axis=-1, keepdims=True) + EPS)


def _inproj_kernel(x_ref, g_ref, wqkvo_ref, wa_ref, wc_ref, wgate_ref, bgate_ref,
                   q_ref, k_ref, v_ref, og_ref, la_ref, glu_ref):
    u = (_rms(x_ref[...]) * g_ref[...]).astype(bf16)
    p = _dot(u, wqkvo_ref[...])
    q_ref[...] = (p[:, :GLA_KEY] * (GLA_DK ** -0.5)).astype(bf16)
    k_ref[...] = p[:, GLA_KEY:2 * GLA_KEY].astype(bf16)
    v_ref[...] = p[:, 2 * GLA_KEY:2 * GLA_KEY + GLA_VAL].astype(bf16)
    og_ref[...] = p[:, 2 * GLA_KEY + GLA_VAL:].astype(bf16)
    a = _dot(u, wa_ref[...])
    z = _dot(a.astype(bf16), wgate_ref[...]) + bgate_ref[...]
    la_ref[...] = (jnp.minimum(z, 0.0) - jnp.log1p(jnp.exp(-jnp.abs(z)))) * (1.0 / GATE_NORMALIZER)
    c = _dot(u, wc_ref[...])
    glu_ref[...] = (c[:, :CONV_CH] * jax.nn.sigmoid(c[:, CONV_CH:])).astype(bf16)


def _inproj(x, g_mix, wqkvo, wa, wc, wgate, bgate, tm):
    n = x.shape[0]
    row = lambda w: pl.BlockSpec((tm, w), lambda i: (i, 0))
    full = lambda a: pl.BlockSpec(a.shape, lambda i: (0, 0))
    return pl.pallas_call(
        _inproj_kernel,
        grid=(n // tm,),
        in_specs=[row(D_MODEL), full(g_mix), full(wqkvo), full(wa), full(wc), full(wgate), full(bgate)],
        out_specs=[row(GLA_KEY), row(GLA_KEY), row(GLA_VAL), row(GLA_VAL), row(2 * GLA_KEY), row(CONV_CH)],
        out_shape=[jax.ShapeDtypeStruct((n, GLA_KEY), bf16), jax.ShapeDtypeStruct((n, GLA_KEY), bf16),
                   jax.ShapeDtypeStruct((n, GLA_VAL), bf16), jax.ShapeDtypeStruct((n, GLA_VAL), bf16),
                   jax.ShapeDtypeStruct((n, 2 * GLA_KEY), f32), jax.ShapeDtypeStruct((n, CONV_CH), bf16)],
        compiler_params=_cparams(("parallel",)),
        name="inproj",
    )(x, g_mix, wqkvo, wa, wc, wgate, bgate)


def _gla_unit(q_ref, k_ref, v_ref, cs, la, pair, reverse, masks):
    head0_k, head0_v, keep = masks
    tl = cs.shape[0]
    kl = slice(pair * PAIR_DK, (pair + 1) * PAIR_DK)
    vl = slice(pair * PAIR_DV, (pair + 1) * PAIR_DV)
    n_chunks = tl // GLA_CHUNK
    order = list(range(n_chunks - 1, -1, -1) if reverse else range(n_chunks))

    def stack_heads(x, head0):
        z = jnp.zeros_like(x)
        return jnp.concatenate([jnp.where(head0, x, z), jnp.where(head0, z, x)], axis=0)

    local = {}
    for j in order:
        rows = slice(j * GLA_CHUNK, (j + 1) * GLA_CHUNK)
        cj = cs[rows, kl]
        q = q_ref[rows, kl].astype(f32)
        k = k_ref[rows, kl].astype(f32)
        c_last = cj[GLA_CHUNK - 1:GLA_CHUNK, :]
        mid = 0.5 * c_last
        rel = (mid - (cj - la[rows, kl])) if reverse else (cj - mid)
        qd = q * jnp.exp(rel)
        kd = k * jnp.exp(-rel)
        edge = jnp.exp(mid)
        qi = qd * edge
        ku = kd * edge
        vbd = stack_heads(v_ref[rows, vl], head0_v)
        sc = _dot_nt(qd.astype(bf16), stack_heads(kd.astype(bf16), head0_k))
        p = jnp.where(keep, sc, 0.0).astype(bf16)
        kv = _dot_tn(stack_heads(ku.astype(bf16), head0_k), vbd)
        local[j] = (p, vbd, qi.astype(bf16), kv, c_last)
    return order, vl, local


def _gla_finish(o_ref, st_ref, order, vl, local):
    row_id = lax.broadcasted_iota(i32, (PAIR_DK, PAIR_DK), 0)
    c_rows = jnp.zeros((PAIR_DK, PAIR_DK), f32)
    for j in order:
        c_rows = jnp.where(row_id == j, local[j][4], c_rows)
    decay_cols = jnp.exp(jnp.transpose(c_rows))
    st = st_ref[...]
    st_before = {}
    for j in order:
        st_before[j] = st.astype(bf16)
        st = decay_cols[:, j:j + 1] * st + local[j][3]
    st_ref[...] = st
    for j in order:
        p, vbd, qi, _, _ = local[j]
        o = _dot(p, vbd) + _dot(qi, st_before[j])
        o_ref[j * GLA_CHUNK:(j + 1) * GLA_CHUNK, vl] = o.astype(o_ref.dtype)


def _gla_kernel(qf, kf, vf, laf, qb, kb, vb, lab, of, ob, st_ref, tri_ref):
    tl = qf.shape[0]

    @pl.when(pl.program_id(1) == 0)
    def _():
        st_ref[...] = jnp.zeros_like(st_ref)
        r = lax.broadcasted_iota(i32, tri_ref.shape, 0)
        c = lax.broadcasted_iota(i32, tri_ref.shape, 1)
        tri_ref[...] = jnp.where((r // GLA_CHUNK == c // GLA_CHUNK) & (c <= r), 1.0, 0.0).astype(bf16)

    tri = tri_ref[...]
    cum_rows = tri_ref.shape[0]

    def chunk_cumsum(x):
        blocks = [_dot(tri, x[r0:r0 + cum_rows]) for r0 in range(0, tl, cum_rows)]
        return blocks[0] if len(blocks) == 1 else jnp.concatenate(blocks, axis=0)

    head0_k = lax.broadcasted_iota(i32, (GLA_CHUNK, PAIR_DK), 1) < GLA_DK
    head0_v = lax.broadcasted_iota(i32, (GLA_CHUNK, PAIR_DV), 1) < GLA_DV
    t_idx = lax.broadcasted_iota(i32, (GLA_CHUNK, 2 * GLA_CHUNK), 0)
    s_idx = lax.broadcasted_iota(i32, (GLA_CHUNK, 2 * GLA_CHUNK), 1) % GLA_CHUNK
    units = []
    for d, (q_ref, k_ref, v_ref, la_ref, o_ref) in enumerate(((qf, kf, vf, laf, of), (qb, kb, vb, lab, ob))):
        reverse = d == 1
        la = la_ref[...]
        la_hi, la_lo = _split_bf16(la)
        cs = chunk_cumsum(la_hi) + chunk_cumsum(la_lo)
        keep = (s_idx > t_idx) if reverse else (s_idx <= t_idx)
        for pair in range(GLA_HEADS // 2):
            units.append((o_ref, st_ref.at[d, pair]) + _gla_unit(q_ref, k_ref, v_ref, cs, la, pair, reverse,
                                                                  (head0_k, head0_v, keep)))
    for unit in units:
        _gla_finish(*unit)


def _gla(q, k, v, la, n_seq, seq_len):
    n = q.shape[0]
    tl = _tile(seq_len, GLA_TILE)
    assert tl % GLA_CUM_ROWS == 0 and GLA_CUM_ROWS % GLA_CHUNK == 0
    nblk = seq_len // tl
    fwd = lambda b, i: (b * nblk + i, 0)
    bwd = lambda b, i: (b * nblk + nblk - 1 - i, 0)
    bwd_la = lambda b, i: (b * nblk + nblk - 1 - i, 1)
    kq = lambda im: pl.BlockSpec((tl, GLA_KEY), im)
    vv = lambda im: pl.BlockSpec((tl, GLA_VAL), im)
    return pl.pallas_call(
        _gla_kernel,
        grid=(n_seq, nblk),
        in_specs=[kq(fwd), kq(fwd), vv(fwd), kq(fwd), kq(bwd), kq(bwd), vv(bwd), kq(bwd_la)],
        out_specs=[vv(fwd), vv(bwd)],
        out_shape=[jax.ShapeDtypeStruct((n, GLA_VAL), bf16), jax.ShapeDtypeStruct((n, GLA_VAL), bf16)],
        scratch_shapes=[pltpu.VMEM((2, GLA_HEADS // 2, PAIR_DK, PAIR_DV), f32), pltpu.VMEM((GLA_CUM_ROWS, GLA_CUM_ROWS), bf16)],
        compiler_params=_cparams(("parallel", "arbitrary")),
        name="gla",
    )(q, k, v, la, q, k, v, la)


def _conv_kernel(prev_ref, cur_ref, next_ref, wdw_ref, bdw_ref, gln_ref, bln_ref, wpw_ref,
                 o_ref, ext_ref, y_ref, *, tiles_per_seq):
    tl = cur_ref.shape[0]
    pos = pl.program_id(0) % tiles_per_seq
    prev = jnp.where(pos == 0, 0.0, prev_ref[...].astype(f32))
    nxt = jnp.where(pos == tiles_per_seq - 1, 0.0, next_ref[...].astype(f32))
    cur = cur_ref[...].astype(f32)
    n_slabs = CONV_CH // LANES
    for c in range(n_slabs):
        lanes = slice(c * LANES, (c + 1) * LANES)
        ext_ref[c, 0:CONV_HALO, :] = prev[:, lanes]
        ext_ref[c, CONV_HALO:CONV_HALO + tl, :] = cur[:, lanes]
        ext_ref[c, CONV_HALO + tl:, :] = nxt[:, lanes]
    off = CONV_HALO - CONV_K // 2

    def body(rb, carry):
        base = pl.multiple_of(rb * CONV_ROWS, CONV_ROWS)
        for c in range(n_slabs):
            lanes = slice(c * LANES, (c + 1) * LANES)
            a = jnp.zeros((CONV_ROWS, LANES), f32) + bdw_ref[:, lanes]
            for j in range(CONV_K):
                taps = ext_ref.at[c, pl.ds(off + j, tl)]
                a = a + taps[pl.ds(base, CONV_ROWS), :] * wdw_ref[j:j + 1, lanes]
            y_ref[pl.ds(base, CONV_ROWS), lanes] = a
        return carry

    lax.fori_loop(0, tl // CONV_ROWS, body, 0)
    acc = y_ref[...]
    mu = jnp.mean(acc, axis=-1, keepdims=True)
    xc = acc - mu
    yn = xc * lax.rsqrt(jnp.mean(xc * xc, axis=-1, keepdims=True) + EPS) * gln_ref[...] + bln_ref[...]
    o_ref[...] = _dot((yn * jax.nn.sigmoid(yn)).astype(bf16), wpw_ref[...]).astype(bf16)


def _conv(glu, wdw, bdw, gln, bln, wpw, seq_len, tl):
    n = glu.shape[0]
    hb = tl // CONV_HALO
    n_halo = n // CONV_HALO
    full = lambda a: pl.BlockSpec(a.shape, lambda i: (0, 0))
    return pl.pallas_call(
        functools.partial(_conv_kernel, tiles_per_seq=seq_len // tl),
        grid=(n // tl,),
        in_specs=[pl.BlockSpec((CONV_HALO, CONV_CH), lambda i: (jnp.maximum(i * hb - 1, 0), 0)),
                  pl.BlockSpec((tl, CONV_CH), lambda i: (i, 0)),
                  pl.BlockSpec((CONV_HALO, CONV_CH), lambda i: (jnp.minimum((i + 1) * hb, n_halo - 1), 0)),
                  full(wdw), full(bdw), full(gln), full(bln), full(wpw)],
        out_specs=pl.BlockSpec((tl, CONV_CH), lambda i: (i, 0)),
        out_shape=jax.ShapeDtypeStruct((n, CONV_CH), bf16),
        scratch_shapes=[pltpu.VMEM((CONV_CH // LANES, tl + 2 * CONV_HALO, LANES), f32),
                        pltpu.VMEM((tl, CONV_CH), f32)],
        compiler_params=_cparams(("parallel",)),
        name="conv",
    )(glu, glu, glu, wdw, bdw, gln, bln, wpw)


def _mix_kernel(x_ref, of_ref, ob_ref, og_ref, oc_ref, gh_ref, woa_ref, wob_ref, gf_ref, wr_ref, br_ref,
                h_ref, up_ref, eid_ref, wts_ref):
    tm = x_ref.shape[0]
    o = of_ref[...].astype(f32) + ob_ref[...].astype(f32)
    o = jnp.concatenate([_rms(o[:, h * GLA_DV:(h + 1) * GLA_DV]) for h in range(GLA_HEADS)], axis=1)
    og = og_ref[...].astype(f32)
    o = o * gh_ref[...] * (og * jax.nn.sigmoid(og))
    h = x_ref[...] + _dot(o.astype(bf16), woa_ref[...]) + _dot(oc_ref[...], wob_ref[...])
    h_ref[...] = h
    u = _rms(h) * gf_ref[...]
    u_hi, u_lo = _split_bf16(u)
    up_ref[...] = u_hi

    w_hi, w_lo = _split_bf16(wr_ref[...])
    n_r = w_hi.shape[0]
    both = _dot_nt(jnp.concatenate([w_hi, w_lo], axis=0), u_hi)
    logits = both[:n_r] + both[n_r:] + _dot_nt(w_hi, u_lo) + br_ref[...]
    le = logits[0:N_EXPERTS]
    lg = logits[N_EXPERTS:N_EXPERTS + N_GROUPS]
    gmax = jnp.max(lg, axis=0, keepdims=True)
    gi = lax.broadcasted_iota(i32, lg.shape, 0)
    gsel = jnp.min(jnp.where(lg == gmax, gi, N_GROUPS), axis=0, keepdims=True)
    gate = 1.0 / jnp.sum(jnp.exp(lg - gmax), axis=0, keepdims=True)
    ri = lax.broadcasted_iota(i32, le.shape, 0)
    lm = jnp.where(ri // EXPERTS_PER_GROUP == gsel, le, -jnp.inf)
    m1 = jnp.max(lm, axis=0, keepdims=True)
    i1 = jnp.min(jnp.where(lm == m1, ri, N_EXPERTS), axis=0, keepdims=True)
    lm2 = jnp.where(ri == i1, -jnp.inf, lm)
    m2 = jnp.max(lm2, axis=0, keepdims=True)
    i2 = jnp.min(jnp.where(lm2 == m2, ri, N_EXPERTS), axis=0, keepdims=True)
    t = jnp.exp(m2 - m1)
    den = 1.0 / (1.0 + t)
    r8 = lax.broadcasted_iota(i32, (SUBLANES, tm), 0)
    eid_ref[...] = jnp.where(r8 == 0, i1, jnp.where(r8 == 1, i2, 0))
    wts_ref[...] = jnp.where(r8 == 0, gate * den, jnp.where(r8 == 1, gate * t * den, 0.0))


def _mix(x, o_f, o_b, og, oc, g_head, woa, wob, g_ffn, wr, br, tm):
    n = x.shape[0]
    row = lambda w: pl.BlockSpec((tm, w), lambda i: (i, 0))
    col = pl.BlockSpec((SUBLANES, tm), lambda i: (0, i))
    full = lambda a: pl.BlockSpec(a.shape, lambda i: (0, 0))
    return pl.pallas_call(
        _mix_kernel,
        grid=(n // tm,),
        in_specs=[row(D_MODEL), row(GLA_VAL), row(GLA_VAL), row(GLA_VAL), row(CONV_CH),
                  full(g_head), full(woa), full(wob), full(g_ffn), full(wr), full(br)],
        out_specs=[row(D_MODEL), row(D_MODEL), col, col],
        out_shape=[jax.ShapeDtypeStruct((n, D_MODEL), f32), jax.ShapeDtypeStruct((n, D_MODEL), bf16),
                   jax.ShapeDtypeStruct((SUBLANES, n), i32), jax.ShapeDtypeStruct((SUBLANES, n), f32)],
        compiler_params=_cparams(("parallel",)),
        name="mix",
    )(x, o_f, o_b, og, oc, g_head, woa, wob, g_ffn, wr, br)


def _rank_kernel(eid_ref, pos_ref, tbl_ref, blk_ref, tail_ref, cnt_ref, base_ref, end_ref, *, blk, nbp, tm):
    ps = pl.program_id(0)
    i = pl.program_id(1)
    rr = lax.broadcasted_iota(i32, (N_EXPERTS, LANES), 0)

    def cumsum_experts(x):
        for s in (1, 2, 4, 8, 16):
            x = x + jnp.where(rr >= s, pltpu.roll(x, s, axis=0), 0.0)
        return x

    @pl.when((ps == 0) & (i == 0))
    def _():
        cnt_ref[...] = jnp.zeros_like(cnt_ref)

    @pl.when(ps == 0)
    def _():
        pos_ref[...] = jnp.zeros_like(pos_ref)
        tbl_ref[...] = jnp.zeros_like(tbl_ref)

    @pl.when((ps == 1) & (i == 0))
    def _():
        cnt = cnt_ref[...]
        pc = jnp.floor((cnt + (blk - 1)) * (1.0 / blk)) * blk
        inc = cumsum_experts(pc)
        base_ref[...] = inc - pc
        end_ref[...] = inc
        pend = jnp.concatenate([inc] * (nbp // LANES), axis=1)
        jl = lax.broadcasted_iota(i32, (N_EXPERTS, nbp), 1).astype(f32) * blk
        be = jnp.minimum(jnp.sum(jnp.where(pend <= jl, 1.0, 0.0), axis=0, keepdims=True), N_EXPERTS - 1.0)
        nused = jnp.concatenate([inc[N_EXPERTS - 1:N_EXPERTS, :]] * (nbp // LANES), axis=1) * (1.0 / blk)
        r8 = lax.broadcasted_iota(i32, (SUBLANES, nbp), 0)
        blk_ref[...] = jnp.where(r8 == 0, be, jnp.where(r8 == 1, nused, 0.0)).astype(i32)

    ri = lax.broadcasted_iota(i32, (N_EXPERTS, tm), 0)
    for t in range(eid_ref.shape[1] // tm):
        lanes = slice(t * tm, (t + 1) * tm)
        oh1 = ri == eid_ref[0:1, lanes]
        oh2 = ri == eid_ref[1:2, lanes]
        ohf = jnp.where(oh1 | oh2, 1.0, 0.0)
        tile_cnt = jnp.sum(ohf, axis=1, keepdims=True)
        run_len = (jnp.floor((tile_cnt + (RUN_ALIGN - 1)) * (1.0 / RUN_ALIGN)) * RUN_ALIGN
                   + jnp.zeros((N_EXPERTS, LANES), f32))

        @pl.when(ps == 0)
        def _():
            cnt_ref[...] += run_len

        @pl.when(ps == 1)
        def _():
            a = lax.broadcasted_iota(i32, (tm, tm), 0)
            b = lax.broadcasted_iota(i32, (tm, tm), 1)
            upper = jnp.where(a < b, 1.0, 0.0).astype(bf16)
            before = _dot(ohf.astype(bf16), upper)
            run_start = cumsum_experts(run_len) - run_len
            row = run_start[:, 0:1] + before
            p1 = jnp.sum(jnp.where(oh1, row, 0.0), axis=0, keepdims=True)
            p2 = jnp.sum(jnp.where(oh2, row, 0.0), axis=0, keepdims=True)
            r8 = lax.broadcasted_iota(i32, (SUBLANES, tm), 0)
            pos_ref[0, :, lanes] = jnp.where(r8 == 0, p1, jnp.where(r8 == 1, p2, 0.0)).astype(i32)
            diag = lax.broadcasted_iota(i32, (N_EXPERTS, LANES), 1) == rr
            on_lanes = lambda x: jnp.sum(jnp.where(diag, x, 0.0), axis=0, keepdims=True)
            big = jnp.floor(run_len * (1.0 / BIG_CHUNK))
            small = (run_len - big * BIG_CHUNK) * (1.0 / RUN_ALIGN)
            rows = (on_lanes(run_start), on_lanes(base_ref[...]), on_lanes(big), on_lanes(small),
                    jnp.sum(run_len, axis=0, keepdims=True))
            t8 = lax.broadcasted_iota(i32, (SUBLANES, LANES), 0)
            tbl = jnp.zeros((SUBLANES, LANES), f32)
            for k, x in enumerate(rows):
                tbl = jnp.where(t8 == k, x, tbl)
            tbl_ref[0, t] = tbl.astype(i32)
            base_ref[...] += run_len

    @pl.when((ps == 1) & (i == pl.num_programs(1) - 1))
    def _():
        diag = lax.broadcasted_iota(i32, (N_EXPERTS, LANES), 1) == rr
        on_lanes = lambda x: jnp.sum(jnp.where(diag, x, 0.0), axis=0, keepdims=True)
        first = base_ref[...]
        count = end_ref[...] - first
        rows = (on_lanes(first), on_lanes(count), jnp.sum(count, axis=0, keepdims=True),
                end_ref[N_EXPERTS - 1:N_EXPERTS, :] * (1.0 / blk))
        t8 = lax.broadcasted_iota(i32, (SUBLANES, LANES), 0)
        tail = jnp.zeros((SUBLANES, LANES), f32)
        for k, x in enumerate(rows):
            tail = jnp.where(t8 == k, x, tail)
        tail_ref[...] = tail.astype(i32)


def _rank(eid, blk, nbp, tm):
    n = eid.shape[1]
    group = _tile(n // tm, RANK_GROUP)
    return pl.pallas_call(
        functools.partial(_rank_kernel, blk=blk, nbp=nbp, tm=tm),
        grid=(2, n // (tm * group)),
        in_specs=[pl.BlockSpec((SUBLANES, tm * group), lambda p, i: (0, i))],
        out_specs=[pl.BlockSpec((1, SUBLANES, tm * group), lambda p, i: (p, 0, i)),
                   pl.BlockSpec((1, group, SUBLANES, LANES), lambda p, i: (p, i, 0, 0)),
                   pl.BlockSpec((SUBLANES, nbp), lambda p, i: (0, 0)),
                   pl.BlockSpec((SUBLANES, LANES), lambda p, i: (0, 0))],
        out_shape=[jax.ShapeDtypeStruct((2, SUBLANES, n), i32),
                   jax.ShapeDtypeStruct((2, n // tm, SUBLANES, LANES), i32),
                   jax.ShapeDtypeStruct((SUBLANES, nbp), i32),
                   jax.ShapeDtypeStruct((SUBLANES, LANES), i32)],
        scratch_shapes=[pltpu.VMEM((N_EXPERTS, LANES), f32)] * 3,
        compiler_params=_cparams(("arbitrary", "arbitrary")),
        name="rank",
    )(eid)


def _start_run_copies(tbl_ref, copy):
    def per_expert(e, carry):
        sorted0 = tbl_ref[0, e]
        slot0 = tbl_ref[1, e]
        n_big = tbl_ref[2, e]

        def pieces(rows, first):
            def body(c, carry2):
                off = first + c * rows
                copy(pl.multiple_of(sorted0 + off, RUN_ALIGN), pl.multiple_of(slot0 + off, RUN_ALIGN), rows).start()
                return carry2
            return body

        lax.fori_loop(0, n_big, pieces(BIG_CHUNK, 0), 0)
        lax.fori_loop(0, tbl_ref[3, e], pieces(RUN_ALIGN, n_big * BIG_CHUNK), 0)
        return carry

    lax.fori_loop(0, N_EXPERTS, per_expert, 0)


def _wait_rows(total_rows, copy):
    def waits(rows):
        def body(c, carry):
            copy(0, 0, rows).wait()
            return carry
        return body

    n_wide = total_rows // WAIT_ROWS
    lax.fori_loop(0, n_wide, waits(WAIT_ROWS), 0)
    lax.fori_loop(0, (total_rows - n_wide * WAIT_ROWS) // RUN_ALIGN, waits(RUN_ALIGN), 0)


def _dispatch_kernel(tbl_ref, tail_ref, pos_ref, u_ref, xs_ref, sorted_ref, zero_ref, sems, pending_ref):
    i = pl.program_id(0)
    tm = u_ref.shape[0]
    buf = i % 2

    def copy_from(b):
        def copy(sorted_row, slot_row, rows):
            return pltpu.make_async_copy(sorted_ref.at[b, pl.ds(sorted_row, rows)],
                                         xs_ref.at[pl.ds(slot_row, rows)], sems.at[b])
        return copy

    @pl.when(i == 0)
    def _():
        pending_ref[0] = 0
        pending_ref[1] = 0

    _wait_rows(pending_ref[buf], copy_from(buf))
    p1 = pos_ref[0:1, :]
    p2 = pos_ref[1:2, :]
    u = u_ref[...]
    for r0 in range(0, sorted_ref.shape[1], SORT_ROWS):
        ri = lax.broadcasted_iota(i32, (SORT_ROWS, tm), 0) + r0
        sel = jnp.where((ri == p1) | (ri == p2), 1.0, 0.0).astype(bf16)
        sorted_ref[buf, r0:r0 + SORT_ROWS, :] = _pack_rows(_dot(sel, u))
    _start_run_copies(tbl_ref, copy_from(buf))
    pending_ref[buf] = tbl_ref[4, 0]

    @pl.when(i == pl.num_programs(0) - 1)
    def _():
        for b in range(2):
            _wait_rows(pending_ref[b], copy_from(b))
        blk = zero_ref.shape[0]
        n_blocks = xs_ref.shape[0] // blk
        zero_ref[...] = jnp.zeros_like(zero_ref)

        def zero_copy(unused_row, slot_row, rows):
            return pltpu.make_async_copy(zero_ref.at[pl.ds(0, rows)], xs_ref.at[pl.ds(slot_row, rows)], sems.at[2])

        def per_expert(e, carry):
            first = tail_ref[0, e]
            n_big = tail_ref[1, e] // BIG_CHUNK

            def pieces(rows, start):
                def body(c, carry2):
                    zero_copy(0, pl.multiple_of(first + start + c * rows, RUN_ALIGN), rows).start()
                    return carry2
                return body

            lax.fori_loop(0, n_big, pieces(BIG_CHUNK, 0), 0)
            lax.fori_loop(0, (tail_ref[1, e] - n_big * BIG_CHUNK) // RUN_ALIGN, pieces(RUN_ALIGN, n_big * BIG_CHUNK), 0)
            return carry

        lax.fori_loop(0, N_EXPERTS, per_expert, 0)
        n_used = tail_ref[3, 0]

        def per_block(j, carry):
            zero_copy(0, pl.multiple_of(j * blk, blk), blk).start()
            return carry

        lax.fori_loop(n_used, n_blocks, per_block, 0)
        _wait_rows(tail_ref[2, 0] + (n_blocks - n_used) * blk, zero_copy)


def _dispatch(tbl, tail, pos, up, n_slots, tm, rmax):
    n = up.shape[0]
    return pl.pallas_call(
        _dispatch_kernel,
        grid=(n // tm,),
        in_specs=[pl.BlockSpec((SUBLANES, LANES), lambda i: (i, 0), memory_space=pltpu.SMEM),
                  pl.BlockSpec((SUBLANES, LANES), lambda i: (0, 0), memory_space=pltpu.SMEM),
                  pl.BlockSpec((SUBLANES, tm), lambda i: (0, i)),
                  pl.BlockSpec((tm, D_MODEL), lambda i: (i, 0))],
        out_specs=pl.BlockSpec(memory_space=pl.ANY),
        out_shape=jax.ShapeDtypeStruct((n_slots, HALF), u32),
        scratch_shapes=[pltpu.VMEM((2, rmax, HALF), u32), pltpu.VMEM((MOE_BLOCK, HALF), u32),
                        pltpu.SemaphoreType.DMA((3,)), pltpu.SMEM((2,), i32)],
        compiler_params=_cparams(("arbitrary",)),
        name="dispatch",
    )(tbl, tail, pos, up)


def _expert_kernel(be_ref, nu_ref, xs_ref, wg_ref, wu_ref, wd_ref, y_ref):
    j = pl.program_id(0)

    @pl.when(j < nu_ref[0])
    def _():
        xa, xb = _unpack_rows(xs_ref[...])
        xa = xa.astype(bf16)
        xb = xb.astype(bf16)
        g = _dot(xa, wg_ref[0, :HALF, :]) + _dot(xb, wg_ref[0, HALF:, :])
        u = _dot(xa, wu_ref[0, :HALF, :]) + _dot(xb, wu_ref[0, HALF:, :])
        hb = (g * jax.nn.sigmoid(g) * u).astype(bf16)
        y_ref[...] = _pack_rows(_dot(hb, wd_ref[0]))

    @pl.when(j >= nu_ref[0])
    def _():
        y_ref[...] = jnp.zeros_like(y_ref)


def _experts(block_e, nused, xs, wg, wu, wd, blk):
    nb = xs.shape[0] // blk
    used = lambda j, be, nu: (jnp.minimum(j, nu[0] - 1), 0)
    return pl.pallas_call(
        _expert_kernel,
        grid_spec=pltpu.PrefetchScalarGridSpec(
            num_scalar_prefetch=2,
            grid=(nb,),
            in_specs=[pl.BlockSpec((blk, HALF), used),
                      pl.BlockSpec((1, D_MODEL, D_EXPERT), lambda j, be, nu: (be[j], 0, 0)),
                      pl.BlockSpec((1, D_MODEL, D_EXPERT), lambda j, be, nu: (be[j], 0, 0)),
                      pl.BlockSpec((1, D_EXPERT, D_MODEL), lambda j, be, nu: (be[j], 0, 0))],
            out_specs=pl.BlockSpec((blk, HALF), lambda j, be, nu: (j, 0)),
        ),
        out_shape=jax.ShapeDtypeStruct(xs.shape, u32),
        compiler_params=_cparams(("arbitrary",)),
        name="experts",
    )(block_e, nused, xs, wg, wu, wd)


def _combine_kernel(tbl_ref, tbl_next_ref, pos_ref, wts_ref, h_ref, gfin_ref, y_ref, o_ref, ys_ref, sems):
    i = pl.program_id(0)
    tm = h_ref.shape[0]
    buf = i % 2

    def copy_into(b):
        def copy(sorted_row, slot_row, rows):
            return pltpu.make_async_copy(y_ref.at[pl.ds(slot_row, rows)],
                                         ys_ref.at[b, pl.ds(sorted_row, rows)], sems.at[b])
        return copy

    @pl.when(i == 0)
    def _():
        ys_ref[...] = jnp.zeros_like(ys_ref)
        _start_run_copies(tbl_ref, copy_into(0))

    @pl.when(i + 1 < pl.num_programs(0))
    def _():
        _start_run_copies(tbl_next_ref, copy_into(1 - buf))

    pad = jnp.zeros((LANES - SUBLANES, tm), f32)
    pos_cols = jnp.transpose(jnp.concatenate([pos_ref[...].astype(f32), pad], axis=0))
    w_cols = jnp.transpose(jnp.concatenate([wts_ref[...], pad], axis=0))
    p1, p2 = pos_cols[:, 0:1], pos_cols[:, 1:2]
    w1, w2 = w_cols[:, 0:1], w_cols[:, 1:2]
    h = h_ref[...]
    ha = h[:, :HALF]
    hb = h[:, HALF:]
    _wait_rows(tbl_ref[4, 0], copy_into(buf))
    for r0 in range(0, ys_ref.shape[1], SORT_ROWS):
        li = (lax.broadcasted_iota(i32, (tm, SORT_ROWS), 1) + r0).astype(f32)
        sel = (jnp.where(li == p1, w1, 0.0) + jnp.where(li == p2, w2, 0.0)).astype(bf16)
        ya, yb = _unpack_rows(ys_ref[buf, r0:r0 + SORT_ROWS, :])
        ha = ha + _dot(sel, ya.astype(bf16))
        hb = hb + _dot(sel, yb.astype(bf16))
    ms = (jnp.sum(ha * ha, axis=-1, keepdims=True) + jnp.sum(hb * hb, axis=-1, keepdims=True)) * (1.0 / D_MODEL)
    inv = lax.rsqrt(ms + EPS)
    o_ref[:, :HALF] = ha * inv * gfin_ref[:, :HALF]
    o_ref[:, HALF:] = hb * inv * gfin_ref[:, HALF:]


def _combine(tbl, pos, wts, h, g_final, ybuf, tm, rmax):
    n = h.shape[0]
    n_tiles = n // tm
    col = pl.BlockSpec((SUBLANES, tm), lambda i: (0, i))
    return pl.pallas_call(
        _combine_kernel,
        grid=(n_tiles,),
        in_specs=[pl.BlockSpec((SUBLANES, LANES), lambda i: (i, 0), memory_space=pltpu.SMEM),
                  pl.BlockSpec((SUBLANES, LANES), lambda i: (jnp.minimum(i + 1, n_tiles - 1), 0),
                               memory_space=pltpu.SMEM),
                  col, col,
                  pl.BlockSpec((tm, D_MODEL), lambda i: (i, 0)),
                  pl.BlockSpec(g_final.shape, lambda i: (0, 0)),
                  pl.BlockSpec(memory_space=pl.ANY)],
        out_specs=pl.BlockSpec((tm, D_MODEL), lambda i: (i, 0)),
        out_shape=jax.ShapeDtypeStruct((n, D_MODEL), f32),
        scratch_shapes=[pltpu.VMEM((2, rmax, HALF), u32), pltpu.SemaphoreType.DMA((2,))],
        compiler_params=_cparams(("arbitrary",)),
        name="combine",
    )(tbl, tbl, pos, wts, h, g_final, ybuf)


def _tile(n, pref):
    t = pref
    while n % t:
        t //= 2
    return t


def _prep_weights(g_mix, w_in, w_af2, b_af, w_ab2, b_ab, g_head, w_dw, b_dw, g_ln, b_ln, w_pw2, w_out,
                  g_ffn, w_rg, b_rg, w_re, b_re, w_e_gate, w_e_up, w_e_down, g_final):
    n_qkvo = 2 * GLA_KEY + 2 * GLA_VAL
    n_a = 2 * GATE_RANK
    w_in = w_in[0]
    zg = jnp.zeros((GATE_RANK, GLA_KEY), f32)
    wgate = jnp.concatenate([jnp.concatenate([w_af2[0], zg], axis=1), jnp.concatenate([zg, w_ab2[0]], axis=1)], axis=0)
    wdw = jnp.concatenate([w_dw[0], jnp.zeros((1, CONV_CH), f32)], axis=0)
    n_pad = 2 * SUBLANES - N_GROUPS
    wr = jnp.concatenate([w_re[0].T, w_rg[0].T, jnp.zeros((n_pad, D_MODEL), f32)], axis=0)
    br = jnp.concatenate([b_re[0], b_rg[0], jnp.zeros((n_pad,), f32)])[:, None]
    return dict(
        g_mix=g_mix[0][None, :],
        wqkvo=w_in[:, :n_qkvo].astype(bf16),
        wa=w_in[:, n_qkvo:n_qkvo + n_a].astype(bf16),
        wc=w_in[:, n_qkvo + n_a:].astype(bf16),
        wgate=wgate.astype(bf16),
        bgate=jnp.concatenate([b_af[0], b_ab[0]])[None, :],
        g_head=g_head[0][None, :],
        wdw=wdw, bdw=b_dw[0][None, :], gln=g_ln[0][None, :], bln=b_ln[0][None, :],
        wpw=w_pw2[0].astype(bf16),
        woa=w_out[0, :GLA_VAL].astype(bf16), wob=w_out[0, GLA_VAL:].astype(bf16),
        g_ffn=g_ffn[0][None, :], wr=wr, br=br,
        wg=w_e_gate[0].astype(bf16), wu=w_e_up[0].astype(bf16), wd=w_e_down[0].astype(bf16),
        g_final=g_final[None, :],
    )


def _trunk(x3, w):
    n_seq, seq_len, _ = x3.shape
    n = n_seq * seq_len
    x = x3.reshape(n, D_MODEL)
    tm = _tile(n, MOE_TILE)
    q, k, v, og, la, glu = _inproj(x, w["g_mix"], w["wqkvo"], w["wa"], w["wc"], w["wgate"], w["bgate"],
                                   _tile(n, 1024))
    o_f, o_b = _gla(q, k, v, la, n_seq, seq_len)
    oc = _conv(glu, w["wdw"], w["bdw"], w["gln"], w["bln"], w["wpw"], seq_len, _tile(seq_len, 512))
    h, up, eid, wts = _mix(x, o_f, o_b, og, oc, w["g_head"], w["woa"], w["wob"], w["g_ffn"], w["wr"], w["br"], tm)

    n_tiles = n // tm
    max_slots = n * TOP_K + n_tiles * N_EXPERTS * (RUN_ALIGN - 1)
    nb = -(-max_slots // MOE_BLOCK) + N_EXPERTS
    nbp = -(-nb // LANES) * LANES
    rmax = -(-(tm * TOP_K + N_EXPERTS * (RUN_ALIGN - 1)) // SORT_ROWS) * SORT_ROWS
    pos, tbl, blkinfo, tail = _rank(eid, MOE_BLOCK, nbp, tm)
    pos = pos[1]
    tbl = tbl[1].reshape(n_tiles * SUBLANES, LANES)
    xs = _dispatch(tbl, tail, pos, up, nb * MOE_BLOCK, tm, rmax)
    ybuf = _experts(blkinfo[0], blkinfo[1, :1], xs, w["wg"], w["wu"], w["wd"], MOE_BLOCK)
    y = _combine(tbl, pos, wts, h, w["g_final"], ybuf, tm, rmax)
    return y.reshape(x3.shape)


def kernel(x_prompt, x_sample, g_mix, w_in, w_af2, b_af, w_ab2, b_ab, g_head, w_dw, b_dw, g_ln, b_ln, w_pw2,
           w_out, g_ffn, w_rg, b_rg, w_re, b_re, w_e_gate, w_e_up, w_e_down, g_final):
    w = _prep_weights(g_mix, w_in, w_af2, b_af, w_ab2, b_ab, g_head, w_dw, b_dw, g_ln, b_ln, w_pw2, w_out,
                      g_ffn, w_rg, b_rg, w_re, b_re, w_e_gate, w_e_up, w_e_down, g_final)
    return (_trunk(x_prompt, w), _trunk(x_sample, w))
```

```python
import functools

import jax
import jax.numpy as jnp
from jax import lax
from jax.experimental import pallas as pl
from jax.experimental.pallas import tpu as pltpu

f32 = jnp.float32
bf16 = jnp.bfloat16
i32 = jnp.int32
u32 = jnp.uint32

D_MODEL = 1024
GLA_HEADS = 4
GLA_DK = 64
GLA_DV = 128
GLA_KEY = GLA_HEADS * GLA_DK
GLA_VAL = GLA_HEADS * GLA_DV
GATE_RANK = 16
GATE_NORMALIZER = 16
CONV_CH = 512
CONV_K = 31
N_GROUPS = 4
EXPERTS_PER_GROUP = 8
N_EXPERTS = N_GROUPS * EXPERTS_PER_GROUP
TOP_K = 2
D_EXPERT = 512
EPS = 1e-6

LANES = 128
SUBLANES = 8
GLA_CHUNK = 128
GLA_TILE = 1024
GLA_CUM_ROWS = 256
PAIR_DK = 2 * GLA_DK
PAIR_DV = 2 * GLA_DV
CONV_HALO = 16
CONV_ROWS = 32
CONV_TILE = 1024
HALF = D_MODEL // 2
MOE_TILE = 512
MOE_BLOCK = 512
RUN_ALIGN = SUBLANES
SORT_ROWS = 256
RANK_GROUP = 8
BIG_CHUNK = 32
WAIT_ROWS = 256
VMEM_LIMIT = 56 * 1024 * 1024


def _cparams(sem, **kw):
    return pltpu.CompilerParams(dimension_semantics=sem, vmem_limit_bytes=VMEM_LIMIT, **kw)


def _dot(a, b):
    return jnp.dot(a, b, preferred_element_type=f32)


def _dot_nt(a, b):
    return lax.dot_general(a, b, (((1,), (1,)), ((), ())), preferred_element_type=f32)


def _dot_tn(a, b):
    return lax.dot_general(a, b, (((0,), (0,)), ((), ())), preferred_element_type=f32)


def _split_bf16(x):
    hi = x.astype(bf16)
    lo = (x - hi.astype(f32)).astype(bf16)
    return hi, lo


def _pack_rows(x):
    a = lax.bitcast_convert_type(x[:, :HALF].astype(bf16).astype(f32), u32)
    b = lax.bitcast_convert_type(x[:, HALF:].astype(bf16).astype(f32), u32)
    return a | (b >> 16)


def _unpack_rows(w):
    a = lax.bitcast_convert_type(w & jnp.uint32(0xFFFF0000), f32)
    b = lax.bitcast_convert_type(w << 16, f32)
    return a, b


def _rms(x):
    return x * lax.rsqrt(jnp.mean(x * x, axis=-1, keepdims=True) + EPS)


def _inproj_kernel(x_ref, g_ref, wqkvo_ref, wa_ref, wc_ref, wgate_ref, bgate_ref,
                   q_ref, k_ref, v_ref, og_ref, la_ref, glu_ref):
    u = (_rms(x_ref[...]) * g_ref[...]).astype(bf16)
    p = _dot(u, wqkvo_ref[...])
    q_ref[...] = (p[:, :GLA_KEY] * (GLA_DK ** -0.5)).astype(bf16)
    k_ref[...] = p[:, GLA_KEY:2 * GLA_KEY].astype(bf16)
    v_ref[...] = p[:, 2 * GLA_KEY:2 * GLA_KEY + GLA_VAL].astype(bf16)
    og_ref[...] = p[:, 2 * GLA_KEY + GLA_VAL:].astype(bf16)
    a = _dot(u, wa_ref[...])
    z = _dot(a.astype(bf16), wgate_ref[...]) + bgate_ref[...]
    la_ref[...] = (jnp.minimum(z, 0.0) - jnp.log1p(jnp.exp(-jnp.abs(z)))) * (1.0 / GATE_NORMALIZER)
    c = _dot(u, wc_ref[...])
    glu_ref[...] = (c[:, :CONV_CH] * jax.nn.sigmoid(c[:, CONV_CH:])).astype(bf16)


def _inproj(x, g_mix, wqkvo, wa, wc, wgate, bgate, tm):
    n = x.shape[0]
    row = lambda w: pl.BlockSpec((tm, w), lambda i: (i, 0))
    full = lambda a: pl.BlockSpec(a.shape, lambda i: (0, 0))
    return pl.pallas_call(
        _inproj_kernel,
        grid=(n // tm,),
        in_specs=[row(D_MODEL), full(g_mix), full(wqkvo), full(wa), full(wc), full(wgate), full(bgate)],
        out_specs=[row(GLA_KEY), row(GLA_KEY), row(GLA_VAL), row(GLA_VAL), row(2 * GLA_KEY), row(CONV_CH)],
        out_shape=[jax.ShapeDtypeStruct((n, GLA_KEY), bf16), jax.ShapeDtypeStruct((n, GLA_KEY), bf16),
                   jax.ShapeDtypeStruct((n, GLA_VAL), bf16), jax.ShapeDtypeStruct((n, GLA_VAL), bf16),
                   jax.ShapeDtypeStruct((n, 2 * GLA_KEY), f32), jax.ShapeDtypeStruct((n, CONV_CH), bf16)],
        compiler_params=_cparams(("parallel",)),
        name="inproj",
    )(x, g_mix, wqkvo, wa, wc, wgate, bgate)


def _gla_unit(q_ref, k_ref, v_ref, cs, la, pair, reverse, masks):
    head0_k, head0_v, keep = masks
    tl = cs.shape[0]
    kl = slice(pair * PAIR_DK, (pair + 1) * PAIR_DK)
    vl = slice(pair * PAIR_DV, (pair + 1) * PAIR_DV)
    n_chunks = tl // GLA_CHUNK
    order = list(range(n_chunks - 1, -1, -1) if reverse else range(n_chunks))

    def stack_heads(x, head0):
        z = jnp.zeros_like(x)
        return jnp.concatenate([jnp.where(head0, x, z), jnp.where(head0, z, x)], axis=0)

    local = {}
    for j in order:
        rows = slice(j * GLA_CHUNK, (j + 1) * GLA_CHUNK)
        cj = cs[rows, kl]
        q = q_ref[rows, kl].astype(f32)
        k = k_ref[rows, kl].astype(f32)
        c_last = cj[GLA_CHUNK - 1:GLA_CHUNK, :]
        mid = 0.5 * c_last
        rel = (mid - (cj - la[rows, kl])) if reverse else (cj - mid)
        qd = q * jnp.exp(rel)
        kd = k * jnp.exp(-rel)
        edge = jnp.exp(mid)
        qi = qd * edge
        ku = kd * edge
        vbd = stack_heads(v_ref[rows, vl], head0_v)
        sc = _dot_nt(qd.astype(bf16), stack_heads(kd.astype(bf16), head0_k))
        p = jnp.where(keep, sc, 0.0).astype(bf16)
        kv = _dot_tn(stack_heads(ku.astype(bf16), head0_k), vbd)
        local[j] = (p, vbd, qi.astype(bf16), kv, c_last)
    return order, vl, local


def _gla_finish(o_ref, st_ref, order, vl, local):
    row_id = lax.broadcasted_iota(i32, (PAIR_DK, PAIR_DK), 0)
    c_rows = jnp.zeros((PAIR_DK, PAIR_DK), f32)
    for j in order:
        c_rows = jnp.where(row_id == j, local[j][4], c_rows)
    decay_cols = jnp.exp(jnp.transpose(c_rows))
    st = st_ref[...]
    st_before = {}
    for j in order:
        st_before[j] = st.astype(bf16)
        st = decay_cols[:, j:j + 1] * st + local[j][3]
    st_ref[...] = st
    for j in order:
        p, vbd, qi, _, _ = local[j]
        o = _dot(p, vbd) + _dot(qi, st_before[j])
        o_ref[j * GLA_CHUNK:(j + 1) * GLA_CHUNK, vl] = o.astype(o_ref.dtype)


def _gla_kernel(qf, kf, vf, laf, qb, kb, vb, lab, of, ob, st_ref, tri_ref):
    tl = qf.shape[0]

    @pl.when(pl.program_id(1) == 0)
    def _():
        st_ref[...] = jnp.zeros_like(st_ref)
        r = lax.broadcasted_iota(i32, tri_ref.shape, 0)
        c = lax.broadcasted_iota(i32, tri_ref.shape, 1)
        tri_ref[...] = jnp.where((r // GLA_CHUNK == c // GLA_CHUNK) & (c <= r), 1.0, 0.0).astype(bf16)

    tri = tri_ref[...]
    cum_rows = tri_ref.shape[0]

    def chunk_cumsum(x):
        blocks = [_dot(tri, x[r0:r0 + cum_rows]) for r0 in range(0, tl, cum_rows)]
        return blocks[0] if len(blocks) == 1 else jnp.concatenate(blocks, axis=0)

    head0_k = lax.broadcasted_iota(i32, (GLA_CHUNK, PAIR_DK), 1) < GLA_DK
    head0_v = lax.broadcasted_iota(i32, (GLA_CHUNK, PAIR_DV), 1) < GLA_DV
    t_idx = lax.broadcasted_iota(i32, (GLA_CHUNK, 2 * GLA_CHUNK), 0)
    s_idx = lax.broadcasted_iota(i32, (GLA_CHUNK, 2 * GLA_CHUNK), 1) % GLA_CHUNK
    units = []
    for d, (q_ref, k_ref, v_ref, la_ref, o_ref) in enumerate(((qf, kf, vf, laf, of), (qb, kb, vb, lab, ob))):
        reverse = d == 1
        la = la_ref[...]
        la_hi, la_lo = _split_bf16(la)
        cs = chunk_cumsum(la_hi) + chunk_cumsum(la_lo)
        keep = (s_idx > t_idx) if reverse else (s_idx <= t_idx)
        for pair in range(GLA_HEADS // 2):
            units.append((o_ref, st_ref.at[d, pair]) + _gla_unit(q_ref, k_ref, v_ref, cs, la, pair, reverse,
                                                                  (head0_k, head0_v, keep)))
    for unit in units:
        _gla_finish(*unit)


def _gla(q, k, v, la, n_seq, seq_len):
    n = q.shape[0]
    tl = _tile(seq_len, GLA_TILE)
    assert tl % GLA_CUM_ROWS == 0 and GLA_CUM_ROWS % GLA_CHUNK == 0
    nblk = seq_len // tl
    fwd = lambda b, i: (b * nblk + i, 0)
    bwd = lambda b, i: (b * nblk + nblk - 1 - i, 0)
    bwd_la = lambda b, i: (b * nblk + nblk - 1 - i, 1)
    kq = lambda im: pl.BlockSpec((tl, GLA_KEY), im)
    vv = lambda im: pl.BlockSpec((tl, GLA_VAL), im)
    return pl.pallas_call(
        _gla_kernel,
        grid=(n_seq, nblk),
        in_specs=[kq(fwd), kq(fwd), vv(fwd), kq(fwd), kq(bwd), kq(bwd), vv(bwd), kq(bwd_la)],
        out_specs=[vv(fwd), vv(bwd)],
        out_shape=[jax.ShapeDtypeStruct((n, GLA_VAL), bf16), jax.ShapeDtypeStruct((n, GLA_VAL), bf16)],
        scratch_shapes=[pltpu.VMEM((2, GLA_HEADS // 2, PAIR_DK, PAIR_DV), f32), pltpu.VMEM((GLA_CUM_ROWS, GLA_CUM_ROWS), bf16)],
        compiler_params=_cparams(("parallel", "arbitrary")),
        name="gla",
    )(q, k, v, la, q, k, v, la)


def _conv_kernel(prev_ref, cur_ref, next_ref, wdw_ref, bdw_ref, gln_ref, bln_ref, wpw_ref,
                 o_ref, ext_ref, y_ref, *, tiles_per_seq):
    tl = cur_ref.shape[0]
    pos = pl.program_id(0) % tiles_per_seq
    prev = jnp.where(pos == 0, 0.0, prev_ref[...].astype(f32))
    nxt = jnp.where(pos == tiles_per_seq - 1, 0.0, next_ref[...].astype(f32))
    cur = cur_ref[...].astype(f32)
    n_slabs = CONV_CH // LANES
    for c in range(n_slabs):
        lanes = slice(c * LANES, (c + 1) * LANES)
        ext_ref[c, 0:CONV_HALO, :] = prev[:, lanes]
        ext_ref[c, CONV_HALO:CONV_HALO + tl, :] = cur[:, lanes]
        ext_ref[c, CONV_HALO + tl:, :] = nxt[:, lanes]
    off = CONV_HALO - CONV_K // 2

    def body(rb, carry):
        base = pl.multiple_of(rb * CONV_ROWS, CONV_ROWS)
        for c in range(n_slabs):
            lanes = slice(c * LANES, (c + 1) * LANES)
            a = jnp.zeros((CONV_ROWS, LANES), f32) + bdw_ref[:, lanes]
            for j in range(CONV_K):
                taps = ext_ref.at[c, pl.ds(off + j, tl)]
                a = a + taps[pl.ds(base, CONV_ROWS), :] * wdw_ref[j:j + 1, lanes]
            y_ref[pl.ds(base, CONV_ROWS), lanes] = a
        return carry

    lax.fori_loop(0, tl // CONV_ROWS, body, 0)
    acc = y_ref[...]
    mu = jnp.mean(acc, axis=-1, keepdims=True)
    xc = acc - mu
    yn = xc * lax.rsqrt(jnp.mean(xc * xc, axis=-1, keepdims=True) + EPS) * gln_ref[...] + bln_ref[...]
    o_ref[...] = _dot((yn * jax.nn.sigmoid(yn)).astype(bf16), wpw_ref[...]).astype(bf16)


def _conv(glu, wdw, bdw, gln, bln, wpw, seq_len, tl):
    n = glu.shape[0]
    hb = tl // CONV_HALO
    n_halo = n // CONV_HALO
    full = lambda a: pl.BlockSpec(a.shape, lambda i: (0, 0))
    return pl.pallas_call(
        functools.partial(_conv_kernel, tiles_per_seq=seq_len // tl),
        grid=(n // tl,),
        in_specs=[pl.BlockSpec((CONV_HALO, CONV_CH), lambda i: (jnp.maximum(i * hb - 1, 0), 0)),
                  pl.BlockSpec((tl, CONV_CH), lambda i: (i, 0)),
                  pl.BlockSpec((CONV_HALO, CONV_CH), lambda i: (jnp.minimum((i + 1) * hb, n_halo - 1), 0)),
                  full(wdw), full(bdw), full(gln), full(bln), full(wpw)],
        out_specs=pl.BlockSpec((tl, CONV_CH), lambda i: (i, 0)),
        out_shape=jax.ShapeDtypeStruct((n, CONV_CH), bf16),
        scratch_shapes=[pltpu.VMEM((CONV_CH // LANES, tl + 2 * CONV_HALO, LANES), f32),
                        pltpu.VMEM((tl, CONV_CH), f32)],
        compiler_params=_cparams(("parallel",)),
        name="conv",
    )(glu, glu, glu, wdw, bdw, gln, bln, wpw)


def _mix_kernel(x_ref, of_ref, ob_ref, og_ref, oc_ref, gh_ref, woa_ref, wob_ref, gf_ref, wr_ref, br_ref,
                h_ref, up_ref, eid_ref, wts_ref):
    tm = x_ref.shape[0]
    o = of_ref[...].astype(f32) + ob_ref[...].astype(f32)
    o = jnp.concatenate([_rms(o[:, h * GLA_DV:(h + 1) * GLA_DV]) for h in range(GLA_HEADS)], axis=1)
    og = og_ref[...].astype(f32)
    o = o * gh_ref[...] * (og * jax.nn.sigmoid(og))
    h = x_ref[...] + _dot(o.astype(bf16), woa_ref[...]) + _dot(oc_ref[...], wob_ref[...])
    h_ref[...] = h
    u = _rms(h) * gf_ref[...]
    u_hi, u_lo = _split_bf16(u)
    up_ref[...] = u_hi

    w_hi, w_lo = _split_bf16(wr_ref[...])
    n_r = w_hi.shape[0]
    both = _dot_nt(jnp.concatenate([w_hi, w_lo], axis=0), u_hi)
    logits = both[:n_r] + both[n_r:] + _dot_nt(w_hi, u_lo) + br_ref[...]
    le = logits[0:N_EXPERTS]
    lg = logits[N_EXPERTS:N_EXPERTS + N_GROUPS]
    gmax = jnp.max(lg, axis=0, keepdims=True)
    gi = lax.broadcasted_iota(i32, lg.shape, 0)
    gsel = jnp.min(jnp.where(lg == gmax, gi, N_GROUPS), axis=0, keepdims=True)
    gate = 1.0 / jnp.sum(jnp.exp(lg - gmax), axis=0, keepdims=True)
    ri = lax.broadcasted_iota(i32, le.shape, 0)
    lm = jnp.where(ri // EXPERTS_PER_GROUP == gsel, le, -jnp.inf)
    m1 = jnp.max(lm, axis=0, keepdims=True)
    i1 = jnp.min(jnp.where(lm == m1, ri, N_EXPERTS), axis=0, keepdims=True)
    lm2 = jnp.where(ri == i1, -jnp.inf, lm)
    m2 = jnp.max(lm2, axis=0, keepdims=True)
    i2 = jnp.min(jnp.where(lm2 == m2, ri, N_EXPERTS), axis=0, keepdims=True)
    t = jnp.exp(m2 - m1)
    den = 1.0 / (1.0 + t)
    r8 = lax.broadcasted_iota(i32, (SUBLANES, tm), 0)
    eid_ref[...] = jnp.where(r8 == 0, i1, jnp.where(r8 == 1, i2, 0))
    wts_ref[...] = jnp.where(r8 == 0, gate * den, jnp.where(r8 == 1, gate * t * den, 0.0))


def _mix(x, o_f, o_b, og, oc, g_head, woa, wob, g_ffn, wr, br, tm):
    n = x.shape[0]
    row = lambda w: pl.BlockSpec((tm, w), lambda i: (i, 0))
    col = pl.BlockSpec((SUBLANES, tm), lambda i: (0, i))
    full = lambda a: pl.BlockSpec(a.shape, lambda i: (0, 0))
    return pl.pallas_call(
        _mix_kernel,
        grid=(n // tm,),
        in_specs=[row(D_MODEL), row(GLA_VAL), row(GLA_VAL), row(GLA_VAL), row(CONV_CH),
                  full(g_head), full(woa), full(wob), full(g_ffn), full(wr), full(br)],
        out_specs=[row(D_MODEL), row(D_MODEL), col, col],
        out_shape=[jax.ShapeDtypeStruct((n, D_MODEL), f32), jax.ShapeDtypeStruct((n, D_MODEL), bf16),
                   jax.ShapeDtypeStruct((SUBLANES, n), i32), jax.ShapeDtypeStruct((SUBLANES, n), f32)],
        compiler_params=_cparams(("parallel",)),
        name="mix",
    )(x, o_f, o_b, og, oc, g_head, woa, wob, g_ffn, wr, br)


def _rank_kernel(eid_ref, pos_ref, tbl_ref, blk_ref, tail_ref, cnt_ref, base_ref, end_ref, *, blk, nbp, tm):
    ps = pl.program_id(0)
    i = pl.program_id(1)
    rr = lax.broadcasted_iota(i32, (N_EXPERTS, LANES), 0)

    def cumsum_experts(x):
        for s in (1, 2, 4, 8, 16):
            x = x + jnp.where(rr >= s, pltpu.roll(x, s, axis=0), 0.0)
        return x

    @pl.when((ps == 0) & (i == 0))
    def _():
        cnt_ref[...] = jnp.zeros_like(cnt_ref)

    @pl.when(ps == 0)
    def _():
        pos_ref[...] = jnp.zeros_like(pos_ref)
        tbl_ref[...] = jnp.zeros_like(tbl_ref)

    @pl.when((ps == 1) & (i == 0))
    def _():
        cnt = cnt_ref[...]
        pc = jnp.floor((cnt + (blk - 1)) * (1.0 / blk)) * blk
        inc = cumsum_experts(pc)
        base_ref[...] = inc - pc
        end_ref[...] = inc
        pend = jnp.concatenate([inc] * (nbp // LANES), axis=1)
        jl = lax.broadcasted_iota(i32, (N_EXPERTS, nbp), 1).astype(f32) * blk
        be = jnp.minimum(jnp.sum(jnp.where(pend <= jl, 1.0, 0.0), axis=0, keepdims=True), N_EXPERTS - 1.0)
        nused = jnp.concatenate([inc[N_EXPERTS - 1:N_EXPERTS, :]] * (nbp // LANES), axis=1) * (1.0 / blk)
        r8 = lax.broadcasted_iota(i32, (SUBLANES, nbp), 0)
        blk_ref[...] = jnp.where(r8 == 0, be, jnp.where(r8 == 1, nused, 0.0)).astype(i32)

    ri = lax.broadcasted_iota(i32, (N_EXPERTS, tm), 0)
    for t in range(eid_ref.shape[1] // tm):
        lanes = slice(t * tm, (t + 1) * tm)
        oh1 = ri == eid_ref[0:1, lanes]
        oh2 = ri == eid_ref[1:2, lanes]
        ohf = jnp.where(oh1 | oh2, 1.0, 0.0)
        tile_cnt = jnp.sum(ohf, axis=1, keepdims=True)
        run_len = (jnp.floor((tile_cnt + (RUN_ALIGN - 1)) * (1.0 / RUN_ALIGN)) * RUN_ALIGN
                   + jnp.zeros((N_EXPERTS, LANES), f32))

        @pl.when(ps == 0)
        def _():
            cnt_ref[...] += run_len

        @pl.when(ps == 1)
        def _():
            a = lax.broadcasted_iota(i32, (tm, tm), 0)
            b = lax.broadcasted_iota(i32, (tm, tm), 1)
            upper = jnp.where(a < b, 1.0, 0.0).astype(bf16)
            before = _dot(ohf.astype(bf16), upper)
            run_start = cumsum_experts(run_len) - run_len
            row = run_start[:, 0:1] + before
            p1 = jnp.sum(jnp.where(oh1, row, 0.0), axis=0, keepdims=True)
            p2 = jnp.sum(jnp.where(oh2, row, 0.0), axis=0, keepdims=True)
            r8 = lax.broadcasted_iota(i32, (SUBLANES, tm), 0)
            pos_ref[0, :, lanes] = jnp.where(r8 == 0, p1, jnp.where(r8 == 1, p2, 0.0)).astype(i32)
            diag = lax.broadcasted_iota(i32, (N_EXPERTS, LANES), 1) == rr
            on_lanes = lambda x: jnp.sum(jnp.where(diag, x, 0.0), axis=0, keepdims=True)
            big = jnp.floor(run_len * (1.0 / BIG_CHUNK))
            small = (run_len - big * BIG_CHUNK) * (1.0 / RUN_ALIGN)
            rows = (on_lanes(run_start), on_lanes(base_ref[...]), on_lanes(big), on_lanes(small),
                    jnp.sum(run_len, axis=0, keepdims=True))
            t8 = lax.broadcasted_iota(i32, (SUBLANES, LANES), 0)
            tbl = jnp.zeros((SUBLANES, LANES), f32)
            for k, x in enumerate(rows):
                tbl = jnp.where(t8 == k, x, tbl)
            tbl_ref[0, t] = tbl.astype(i32)
            base_ref[...] += run_len

    @pl.when((ps == 1) & (i == pl.num_programs(1) - 1))
    def _():
        diag = lax.broadcasted_iota(i32, (N_EXPERTS, LANES), 1) == rr
        on_lanes = lambda x: jnp.sum(jnp.where(diag, x, 0.0), axis=0, keepdims=True)
        first = base_ref[...]
        count = end_ref[...] - first
        rows = (on_lanes(first), on_lanes(count), jnp.sum(count, axis=0, keepdims=True),
                end_ref[N_EXPERTS - 1:N_EXPERTS, :] * (1.0 / blk))
        t8 = lax.broadcasted_iota(i32, (SUBLANES, LANES), 0)
        tail = jnp.zeros((SUBLANES, LANES), f32)
        for k, x in enumerate(rows):
            tail = jnp.where(t8 == k, x, tail)
        tail_ref[...] = tail.astype(i32)


def _rank(eid, blk, nbp, tm):
    n = eid.shape[1]
    group = _tile(n // tm, RANK_GROUP)
    return pl.pallas_call(
        functools.partial(_rank_kernel, blk=blk, nbp=nbp, tm=tm),
        grid=(2, n // (tm * group)),
        in_specs=[pl.BlockSpec((SUBLANES, tm * group), lambda p, i: (0, i))],
        out_specs=[pl.BlockSpec((1, SUBLANES, tm * group), lambda p, i: (p, 0, i)),
                   pl.BlockSpec((1, group, SUBLANES, LANES), lambda p, i: (p, i, 0, 0)),
                   pl.BlockSpec((SUBLANES, nbp), lambda p, i: (0, 0)),
                   pl.BlockSpec((SUBLANES, LANES), lambda p, i: (0, 0))],
        out_shape=[jax.ShapeDtypeStruct((2, SUBLANES, n), i32),
                   jax.ShapeDtypeStruct((2, n // tm, SUBLANES, LANES), i32),
                   jax.ShapeDtypeStruct((SUBLANES, nbp), i32),
                   jax.ShapeDtypeStruct((SUBLANES, LANES), i32)],
        scratch_shapes=[pltpu.VMEM((N_EXPERTS, LANES), f32)] * 3,
        compiler_params=_cparams(("arbitrary", "arbitrary")),
        name="rank",
    )(eid)


def _start_run_copies(tbl_ref, copy):
    def per_expert(e, carry):
        sorted0 = tbl_ref[0, e]
        slot0 = tbl_ref[1, e]
        n_big = tbl_ref[2, e]

        def pieces(rows, first):
            def body(c, carry2):
                off = first + c * rows
                copy(pl.multiple_of(sorted0 + off, RUN_ALIGN), pl.multiple_of(slot0 + off, RUN_ALIGN), rows).start()
                return carry2
            return body

        lax.fori_loop(0, n_big, pieces(BIG_CHUNK, 0), 0)
        lax.fori_loop(0, tbl_ref[3, e], pieces(RUN_ALIGN, n_big * BIG_CHUNK), 0)
        return carry

    lax.fori_loop(0, N_EXPERTS, per_expert, 0)


def _wait_rows(total_rows, copy):
    def waits(rows):
        def body(c, carry):
            copy(0, 0, rows).wait()
            return carry
        return body

    n_wide = total_rows // WAIT_ROWS
    lax.fori_loop(0, n_wide, waits(WAIT_ROWS), 0)
    lax.fori_loop(0, (total_rows - n_wide * WAIT_ROWS) // RUN_ALIGN, waits(RUN_ALIGN), 0)


def _dispatch_kernel(tbl_ref, tail_ref, pos_ref, u_ref, xs_ref, sorted_ref, zero_ref, sems, pending_ref):
    i = pl.program_id(0)
    tm = u_ref.shape[0]
    buf = i % 2

    def copy_from(b):
        def copy(sorted_row, slot_row, rows):
            return pltpu.make_async_copy(sorted_ref.at[b, pl.ds(sorted_row, rows)],
                                         xs_ref.at[pl.ds(slot_row, rows)], sems.at[b])
        return copy

    @pl.when(i == 0)
    def _():
        pending_ref[0] = 0
        pending_ref[1] = 0

    _wait_rows(pending_ref[buf], copy_from(buf))
    p1 = pos_ref[0:1, :]
    p2 = pos_ref[1:2, :]
    u = u_ref[...]
    ri = lax.broadcasted_iota(i32, (SORT_ROWS, tm), 0).astype(f32).astype(bf16)
    one = jnp.ones((SORT_ROWS, tm), bf16)
    for r0 in range(0, sorted_ref.shape[1], SORT_ROWS):
        q1 = (p1 - r0).astype(f32).astype(bf16)
        q2 = (p2 - r0).astype(f32).astype(bf16)
        sel = jnp.where((ri == q1) | (ri == q2), one, jnp.zeros_like(one))
        sorted_ref[buf, r0:r0 + SORT_ROWS, :] = _pack_rows(_dot(sel, u))
    _start_run_copies(tbl_ref, copy_from(buf))
    pending_ref[buf] = tbl_ref[4, 0]

    @pl.when(i == pl.num_programs(0) - 1)
    def _():
        for b in range(2):
            _wait_rows(pending_ref[b], copy_from(b))
        blk = zero_ref.shape[0]
        n_blocks = xs_ref.shape[0] // blk
        zero_ref[...] = jnp.zeros_like(zero_ref)

        def zero_copy(unused_row, slot_row, rows):
            return pltpu.make_async_copy(zero_ref.at[pl.ds(0, rows)], xs_ref.at[pl.ds(slot_row, rows)], sems.at[2])

        def per_expert(e, carry):
            first = tail_ref[0, e]
            n_big = tail_ref[1, e] // BIG_CHUNK

            def pieces(rows, start):
                def body(c, carry2):
                    zero_copy(0, pl.multiple_of(first + start + c * rows, RUN_ALIGN), rows).start()
                    return carry2
                return body

            lax.fori_loop(0, n_big, pieces(BIG_CHUNK, 0), 0)
            lax.fori_loop(0, (tail_ref[1, e] - n_big * BIG_CHUNK) // RUN_ALIGN, pieces(RUN_ALIGN, n_big * BIG_CHUNK), 0)
            return carry

        lax.fori_loop(0, N_EXPERTS, per_expert, 0)
        n_used = tail_ref[3, 0]

        def per_block(j, carry):
            zero_copy(0, pl.multiple_of(j * blk, blk), blk).start()
            return carry

        lax.fori_loop(n_used, n_blocks, per_block, 0)
        _wait_rows(tail_ref[2, 0] + (n_blocks - n_used) * blk, zero_copy)


def _dispatch(tbl, tail, pos, up, n_slots, tm, rmax):
    n = up.shape[0]
    return pl.pallas_call(
        _dispatch_kernel,
        grid=(n // tm,),
        in_specs=[pl.BlockSpec((SUBLANES, LANES), lambda i: (i, 0), memory_space=pltpu.SMEM),
                  pl.BlockSpec((SUBLANES, LANES), lambda i: (0, 0), memory_space=pltpu.SMEM),
                  pl.BlockSpec((SUBLANES, tm), lambda i: (0, i)),
                  pl.BlockSpec((tm, D_MODEL), lambda i: (i, 0))],
        out_specs=pl.BlockSpec(memory_space=pl.ANY),
        out_shape=jax.ShapeDtypeStruct((n_slots, HALF), u32),
        scratch_shapes=[pltpu.VMEM((2, rmax, HALF), u32), pltpu.VMEM((MOE_BLOCK, HALF), u32),
                        pltpu.SemaphoreType.DMA((3,)), pltpu.SMEM((2,), i32)],
        compiler_params=_cparams(("arbitrary",)),
        name="dispatch",
    )(tbl, tail, pos, up)


def _expert_kernel(be_ref, nu_ref, xs_ref, wg_ref, wu_ref, wd_ref, y_ref, wg_bf, wu_bf, wd_bf):
    j = pl.program_id(0)

    @pl.when((j == 0) | (be_ref[j] != be_ref[jnp.maximum(j - 1, 0)]))
    def _():
        wg_bf[...] = wg_ref[0].astype(bf16)
        wu_bf[...] = wu_ref[0].astype(bf16)
        wd_bf[...] = wd_ref[0].astype(bf16)

    @pl.when(j < nu_ref[0])
    def _():
        xa, xb = _unpack_rows(xs_ref[...])
        x = jnp.concatenate([xa.astype(bf16), xb.astype(bf16)], axis=1)
        g = _dot(x, wg_bf[...])
        u = _dot(x, wu_bf[...])
        hb = (g * jax.nn.sigmoid(g) * u).astype(bf16)
        y_ref[...] = _pack_rows(_dot(hb, wd_bf[...]))

    @pl.when(j >= nu_ref[0])
    def _():
        y_ref[...] = jnp.zeros_like(y_ref)


def _experts(block_e, nused, xs, wg, wu, wd, blk):
    nb = xs.shape[0] // blk
    used = lambda j, be, nu: (jnp.minimum(j, nu[0] - 1), 0)
    return pl.pallas_call(
        _expert_kernel,
        grid_spec=pltpu.PrefetchScalarGridSpec(
            num_scalar_prefetch=2,
            grid=(nb,),
            in_specs=[pl.BlockSpec((blk, HALF), used),
                      pl.BlockSpec((1, D_MODEL, D_EXPERT), lambda j, be, nu: (be[j], 0, 0)),
                      pl.BlockSpec((1, D_MODEL, D_EXPERT), lambda j, be, nu: (be[j], 0, 0)),
                      pl.BlockSpec((1, D_EXPERT, D_MODEL), lambda j, be, nu: (be[j], 0, 0))],
            out_specs=pl.BlockSpec((blk, HALF), lambda j, be, nu: (j, 0)),
            scratch_shapes=[pltpu.VMEM((D_MODEL, D_EXPERT), bf16), pltpu.VMEM((D_MODEL, D_EXPERT), bf16),
                            pltpu.VMEM((D_EXPERT, D_MODEL), bf16)],
        ),
        out_shape=jax.ShapeDtypeStruct(xs.shape, u32),
        compiler_params=_cparams(("arbitrary",)),
        name="experts",
    )(block_e, nused, xs, wg, wu, wd)


def _combine_kernel(tbl_ref, tbl_next_ref, pos_ref, wts_ref, h_ref, gfin_ref, y_ref, o_ref, ys_ref, sems):
    i = pl.program_id(0)
    tm = h_ref.shape[0]
    buf = i % 2

    def copy_into(b):
        def copy(sorted_row, slot_row, rows):
            return pltpu.make_async_copy(y_ref.at[pl.ds(slot_row, rows)],
                                         ys_ref.at[b, pl.ds(sorted_row, rows)], sems.at[b])
        return copy

    @pl.when(i == 0)
    def _():
        ys_ref[...] = jnp.zeros_like(ys_ref)
        _start_run_copies(tbl_ref, copy_into(0))

    @pl.when(i + 1 < pl.num_programs(0))
    def _():
        _start_run_copies(tbl_next_ref, copy_into(1 - buf))

    pad = jnp.zeros((LANES - SUBLANES, tm), f32)
    pos_cols = jnp.transpose(jnp.concatenate([pos_ref[...].astype(f32), pad], axis=0))
    w_cols = jnp.transpose(jnp.concatenate([wts_ref[...], pad], axis=0))
    p1, p2 = pos_cols[:, 0:1], pos_cols[:, 1:2]
    w1, w2 = w_cols[:, 0:1], w_cols[:, 1:2]
    h = h_ref[...]
    ha = h[:, :HALF]
    hb = h[:, HALF:]
    _wait_rows(tbl_ref[4, 0], copy_into(buf))
    li = lax.broadcasted_iota(i32, (tm, SORT_ROWS), 1).astype(f32).astype(bf16)
    w1b = w1.astype(bf16) + jnp.zeros((tm, SORT_ROWS), bf16)
    w2b = w2.astype(bf16) + jnp.zeros((tm, SORT_ROWS), bf16)
    none = jnp.zeros((tm, SORT_ROWS), bf16)
    sel = jnp.concatenate(
        [jnp.where(li == (p1 - r0).astype(bf16), w1b, none) + jnp.where(li == (p2 - r0).astype(bf16), w2b, none)
         for r0 in range(0, ys_ref.shape[1], SORT_ROWS)], axis=1)
    ya, yb = _unpack_rows(ys_ref[buf])
    ha = ha + _dot(sel, ya.astype(bf16))
    hb = hb + _dot(sel, yb.astype(bf16))
    ms = (jnp.sum(ha * ha, axis=-1, keepdims=True) + jnp.sum(hb * hb, axis=-1, keepdims=True)) * (1.0 / D_MODEL)
    inv = lax.rsqrt(ms + EPS)
    o_ref[:, :HALF] = ha * inv * gfin_ref[:, :HALF]
    o_ref[:, HALF:] = hb * inv * gfin_ref[:, HALF:]


def _combine(tbl, pos, wts, h, g_final, ybuf, tm, rmax):
    n = h.shape[0]
    n_tiles = n // tm
    col = pl.BlockSpec((SUBLANES, tm), lambda i: (0, i))
    return pl.pallas_call(
        _combine_kernel,
        grid=(n_tiles,),
        in_specs=[pl.BlockSpec((SUBLANES, LANES), lambda i: (i, 0), memory_space=pltpu.SMEM),
                  pl.BlockSpec((SUBLANES, LANES), lambda i: (jnp.minimum(i + 1, n_tiles - 1), 0),
                               memory_space=pltpu.SMEM),
                  col, col,
                  pl.BlockSpec((tm, D_MODEL), lambda i: (i, 0)),
                  pl.BlockSpec(g_final.shape, lambda i: (0, 0)),
                  pl.BlockSpec(memory_space=pl.ANY)],
        out_specs=pl.BlockSpec((tm, D_MODEL), lambda i: (i, 0)),
        out_shape=jax.ShapeDtypeStruct((n, D_MODEL), f32),
        scratch_shapes=[pltpu.VMEM((2, rmax, HALF), u32), pltpu.SemaphoreType.DMA((2,))],
        compiler_params=_cparams(("arbitrary",)),
        name="combine",
    )(tbl, tbl, pos, wts, h, g_final, ybuf)


def _tile(n, pref):
    t = pref
    while n % t:
        t //= 2
    return t


def _prep_weights(g_mix, w_in, w_af2, b_af, w_ab2, b_ab, g_head, w_dw, b_dw, g_ln, b_ln, w_pw2, w_out,
                  g_ffn, w_rg, b_rg, w_re, b_re, w_e_gate, w_e_up, w_e_down, g_final):
    n_qkvo = 2 * GLA_KEY + 2 * GLA_VAL
    n_a = 2 * GATE_RANK
    w_in = w_in[0]
    zg = jnp.zeros((GATE_RANK, GLA_KEY), f32)
    wgate = jnp.concatenate([jnp.concatenate([w_af2[0], zg], axis=1), jnp.concatenate([zg, w_ab2[0]], axis=1)], axis=0)
    wdw = jnp.concatenate([w_dw[0], jnp.zeros((1, CONV_CH), f32)], axis=0)
    n_pad = 2 * SUBLANES - N_GROUPS
    wr = jnp.concatenate([w_re[0].T, w_rg[0].T, jnp.zeros((n_pad, D_MODEL), f32)], axis=0)
    br = jnp.concatenate([b_re[0], b_rg[0], jnp.zeros((n_pad,), f32)])[:, None]
    return dict(
        g_mix=g_mix[0][None, :],
        wqkvo=w_in[:, :n_qkvo].astype(bf16),
        wa=w_in[:, n_qkvo:n_qkvo + n_a].astype(bf16),
        wc=w_in[:, n_qkvo + n_a:].astype(bf16),
        wgate=wgate.astype(bf16),
        bgate=jnp.concatenate([b_af[0], b_ab[0]])[None, :],
        g_head=g_head[0][None, :],
        wdw=wdw, bdw=b_dw[0][None, :], gln=g_ln[0][None, :], bln=b_ln[0][None, :],
        wpw=w_pw2[0].astype(bf16),
        woa=w_out[0, :GLA_VAL].astype(bf16), wob=w_out[0, GLA_VAL:].astype(bf16),
        g_ffn=g_ffn[0][None, :], wr=wr, br=br,
        wg=w_e_gate[0], wu=w_e_up[0], wd=w_e_down[0],
        g_final=g_final[None, :],
    )


def _trunk(x3, w):
    n_seq, seq_len, _ = x3.shape
    n = n_seq * seq_len
    x = x3.reshape(n, D_MODEL)
    tm = _tile(n, MOE_TILE)
    q, k, v, og, la, glu = _inproj(x, w["g_mix"], w["wqkvo"], w["wa"], w["wc"], w["wgate"], w["bgate"],
                                   _tile(n, 1024))
    o_f, o_b = _gla(q, k, v, la, n_seq, seq_len)
    oc = _conv(glu, w["wdw"], w["bdw"], w["gln"], w["bln"], w["wpw"], seq_len, _tile(seq_len, CONV_TILE))
    h, up, eid, wts = _mix(x, o_f, o_b, og, oc, w["g_head"], w["woa"], w["wob"], w["g_ffn"], w["wr"], w["br"], tm)

    n_tiles = n // tm
    max_slots = n * TOP_K + n_tiles * N_EXPERTS * (RUN_ALIGN - 1)
    nb = -(-max_slots // MOE_BLOCK) + N_EXPERTS
    nbp = -(-nb // LANES) * LANES
    rmax = -(-(tm * TOP_K + N_EXPERTS * (RUN_ALIGN - 1)) // SORT_ROWS) * SORT_ROWS
    pos, tbl, blkinfo, tail = _rank(eid, MOE_BLOCK, nbp, tm)
    pos = pos[1]
    tbl = tbl[1].reshape(n_tiles * SUBLANES, LANES)
    xs = _dispatch(tbl, tail, pos, up, nb * MOE_BLOCK, tm, rmax)
    ybuf = _experts(blkinfo[0], blkinfo[1, :1], xs, w["wg"], w["wu"], w["wd"], MOE_BLOCK)
    y = _combine(tbl, pos, wts, h, w["g_final"], ybuf, tm, rmax)
    return y.reshape(x3.shape)


def kernel(x_prompt, x_sample, g_mix, w_in, w_af2, b_af, w_ab2, b_ab, g_head, w_dw, b_dw, g_ln, b_ln, w_pw2,
           w_out, g_ffn, w_rg, b_rg, w_re, b_re, w_e_gate, w_e_up, w_e_down, g_final):
    w = _prep_weights(g_mix, w_in, w_af2, b_af, w_ab2, b_ab, g_head, w_dw, b_dw, g_ln, b_ln, w_pw2, w_out,
                      g_ffn, w_rg, b_rg, w_re, b_re, w_e_gate, w_e_up, w_e_down, g_final)
    return (_trunk(x_prompt, w), _trunk(x_sample, w))
```

```python
import functools

import jax
import jax.numpy as jnp
from jax import lax
from jax.experimental import pallas as pl
from jax.experimental.pallas import tpu as pltpu

f32 = jnp.float32
bf16 = jnp.bfloat16
i32 = jnp.int32
u32 = jnp.uint32

D_MODEL = 1024
GLA_HEADS = 4
GLA_DK = 64
GLA_DV = 128
GLA_KEY = GLA_HEADS * GLA_DK
GLA_VAL = GLA_HEADS * GLA_DV
GATE_RANK = 16
GATE_NORMALIZER = 16
CONV_CH = 512
CONV_K = 31
N_GROUPS = 4
EXPERTS_PER_GROUP = 8
N_EXPERTS = N_GROUPS * EXPERTS_PER_GROUP
TOP_K = 2
D_EXPERT = 512
EPS = 1e-6

LANES = 128
SUBLANES = 8
GLA_CHUNK = 128
GLA_TILE = 1024
GLA_CUM_ROWS = 256
PAIR_DK = 2 * GLA_DK
PAIR_DV = 2 * GLA_DV
CONV_HALO = 16
CONV_ROWS = 32
CONV_TILE = 1024
HALF = D_MODEL // 2
MOE_TILE = 512
MOE_BLOCK = 512
RUN_ALIGN = SUBLANES
SORT_ROWS = 256
RANK_GROUP = 8
BIG_CHUNK = 32
WAIT_ROWS = 256
VMEM_LIMIT = 56 * 1024 * 1024


def _cparams(sem, **kw):
    return pltpu.CompilerParams(dimension_semantics=sem, vmem_limit_bytes=VMEM_LIMIT, **kw)


def _dot(a, b):
    return jnp.dot(a, b, preferred_element_type=f32)


def _dot_nt(a, b):
    return lax.dot_general(a, b, (((1,), (1,)), ((), ())), preferred_element_type=f32)


def _dot_tn(a, b):
    return lax.dot_general(a, b, (((0,), (0,)), ((), ())), preferred_element_type=f32)


def _split_bf16(x):
    hi = x.astype(bf16)
    lo = (x - hi.astype(f32)).astype(bf16)
    return hi, lo


def _pack_rows(x):
    a = lax.bitcast_convert_type(x[:, :HALF].astype(bf16).astype(f32), u32)
    b = lax.bitcast_convert_type(x[:, HALF:].astype(bf16).astype(f32), u32)
    return a | (b >> 16)


def _pack_exact_rows(x):
    a = lax.bitcast_convert_type(x[:, :HALF], u32)
    b = lax.bitcast_convert_type(x[:, HALF:], u32)
    return a | (b >> 16)


def _unpack_rows(w):
    a = lax.bitcast_convert_type(w & jnp.uint32(0xFFFF0000), f32)
    b = lax.bitcast_convert_type(w << 16, f32)
    return a, b


def _rms(x):
    return x * lax.rsqrt(jnp.mean(x * x, axis=-1, keepdims=True) + EPS)


def _inproj_kernel(x_ref, g_ref, wqkvo_ref, wa_ref, wc_ref, wgate_ref, bgate_ref,
                   q_ref, k_ref, v_ref, og_ref, la_ref, glu_ref):
    u = (_rms(x_ref[...]) * g_ref[...]).astype(bf16)
    p = _dot(u, wqkvo_ref[...])
    q_ref[...] = (p[:, :GLA_KEY] * (GLA_DK ** -0.5)).astype(bf16)
    k_ref[...] = p[:, GLA_KEY:2 * GLA_KEY].astype(bf16)
    v_ref[...] = p[:, 2 * GLA_KEY:2 * GLA_KEY + GLA_VAL].astype(bf16)
    og_ref[...] = p[:, 2 * GLA_KEY + GLA_VAL:].astype(bf16)
    a = _dot(u, wa_ref[...])
    z = _dot(a.astype(bf16), wgate_ref[...]) + bgate_ref[...]
    la_ref[...] = (jnp.minimum(z, 0.0) - jnp.log1p(jnp.exp(-jnp.abs(z)))) * (1.0 / GATE_NORMALIZER)
    c = _dot(u, wc_ref[...])
    glu_ref[...] = (c[:, :CONV_CH] * jax.nn.sigmoid(c[:, CONV_CH:])).astype(bf16)


def _inproj(x, g_mix, wqkvo, wa, wc, wgate, bgate, tm):
    n = x.shape[0]
    row = lambda w: pl.BlockSpec((tm, w), lambda i: (i, 0))
    full = lambda a: pl.BlockSpec(a.shape, lambda i: (0, 0))
    return pl.pallas_call(
        _inproj_kernel,
        grid=(n // tm,),
        in_specs=[row(D_MODEL), full(g_mix), full(wqkvo), full(wa), full(wc), full(wgate), full(bgate)],
        out_specs=[row(GLA_KEY), row(GLA_KEY), row(GLA_VAL), row(GLA_VAL), row(2 * GLA_KEY), row(CONV_CH)],
        out_shape=[jax.ShapeDtypeStruct((n, GLA_KEY), bf16), jax.ShapeDtypeStruct((n, GLA_KEY), bf16),
                   jax.ShapeDtypeStruct((n, GLA_VAL), bf16), jax.ShapeDtypeStruct((n, GLA_VAL), bf16),
                   jax.ShapeDtypeStruct((n, 2 * GLA_KEY), f32), jax.ShapeDtypeStruct((n, CONV_CH), bf16)],
        compiler_params=_cparams(("parallel",)),
        name="inproj",
    )(x, g_mix, wqkvo, wa, wc, wgate, bgate)


def _gla_unit(q_ref, k_ref, v_ref, cs, la, pair, reverse, masks):
    head0_k, head0_v, keep = masks
    tl = cs.shape[0]
    kl = slice(pair * PAIR_DK, (pair + 1) * PAIR_DK)
    vl = slice(pair * PAIR_DV, (pair + 1) * PAIR_DV)
    n_chunks = tl // GLA_CHUNK
    order = list(range(n_chunks - 1, -1, -1) if reverse else range(n_chunks))

    def stack_heads(x, head0):
        z = jnp.zeros_like(x)
        return jnp.concatenate([jnp.where(head0, x, z), jnp.where(head0, z, x)], axis=0)

    local = {}
    for j in order:
        rows = slice(j * GLA_CHUNK, (j + 1) * GLA_CHUNK)
        cj = cs[rows, kl]
        q = q_ref[rows, kl].astype(f32)
        k = k_ref[rows, kl].astype(f32)
        c_last = cj[GLA_CHUNK - 1:GLA_CHUNK, :]
        mid = 0.5 * c_last
        rel = (mid - (cj - la[rows, kl])) if reverse else (cj - mid)
        qd = q * jnp.exp(rel)
        kd = k * jnp.exp(-rel)
        edge = jnp.exp(mid)
        qi = qd * edge
        ku = kd * edge
        vbd = stack_heads(v_ref[rows, vl], head0_v)
        sc = _dot_nt(qd.astype(bf16), stack_heads(kd.astype(bf16), head0_k))
        p = jnp.where(keep, sc, 0.0).astype(bf16)
        kv = _dot_tn(stack_heads(ku.astype(bf16), head0_k), vbd)
        local[j] = (p, vbd, qi.astype(bf16), kv, c_last)
    return order, vl, local


def _gla_finish(o_ref, st_ref, order, vl, local):
    row_id = lax.broadcasted_iota(i32, (PAIR_DK, PAIR_DK), 0)
    c_rows = jnp.zeros((PAIR_DK, PAIR_DK), f32)
    for j in order:
        c_rows = jnp.where(row_id == j, local[j][4], c_rows)
    decay_cols = jnp.exp(jnp.transpose(c_rows))
    st = st_ref[...]
    st_before = {}
    for j in order:
        st_before[j] = st.astype(bf16)
        st = decay_cols[:, j:j + 1] * st + local[j][3]
    st_ref[...] = st
    for j in order:
        p, vbd, qi, _, _ = local[j]
        o = _dot(p, vbd) + _dot(qi, st_before[j])
        o_ref[j * GLA_CHUNK:(j + 1) * GLA_CHUNK, vl] = o.astype(o_ref.dtype)


def _gla_kernel(qf, kf, vf, laf, qb, kb, vb, lab, of, ob, st_ref, tri_ref):
    tl = qf.shape[0]

    @pl.when(pl.program_id(1) == 0)
    def _():
        st_ref[...] = jnp.zeros_like(st_ref)
        r = lax.broadcasted_iota(i32, tri_ref.shape, 0)
        c = lax.broadcasted_iota(i32, tri_ref.shape, 1)
        tri_ref[...] = jnp.where((r // GLA_CHUNK == c // GLA_CHUNK) & (c <= r), 1.0, 0.0).astype(bf16)

    tri = tri_ref[...]
    cum_rows = tri_ref.shape[0]

    def chunk_cumsum(x):
        blocks = [_dot(tri, x[r0:r0 + cum_rows]) for r0 in range(0, tl, cum_rows)]
        return blocks[0] if len(blocks) == 1 else jnp.concatenate(blocks, axis=0)

    head0_k = lax.broadcasted_iota(i32, (GLA_CHUNK, PAIR_DK), 1) < GLA_DK
    head0_v = lax.broadcasted_iota(i32, (GLA_CHUNK, PAIR_DV), 1) < GLA_DV
    t_idx = lax.broadcasted_iota(i32, (GLA_CHUNK, 2 * GLA_CHUNK), 0)
    s_idx = lax.broadcasted_iota(i32, (GLA_CHUNK, 2 * GLA_CHUNK), 1) % GLA_CHUNK
    units = []
    for d, (q_ref, k_ref, v_ref, la_ref, o_ref) in enumerate(((qf, kf, vf, laf, of), (qb, kb, vb, lab, ob))):
        reverse = d == 1
        la = la_ref[...]
        la_hi, la_lo = _split_bf16(la)
        cs = chunk_cumsum(la_hi) + chunk_cumsum(la_lo)
        keep = (s_idx > t_idx) if reverse else (s_idx <= t_idx)
        for pair in range(GLA_HEADS // 2):
            units.append((o_ref, st_ref.at[d, pair]) + _gla_unit(q_ref, k_ref, v_ref, cs, la, pair, reverse,
                                                                  (head0_k, head0_v, keep)))
    for unit in units:
        _gla_finish(*unit)


def _gla(q, k, v, la, n_seq, seq_len):
    n = q.shape[0]
    tl = _tile(seq_len, GLA_TILE)
    assert tl % GLA_CUM_ROWS == 0 and GLA_CUM_ROWS % GLA_CHUNK == 0
    nblk = seq_len // tl
    fwd = lambda b, i: (b * nblk + i, 0)
    bwd = lambda b, i: (b * nblk + nblk - 1 - i, 0)
    bwd_la = lambda b, i: (b * nblk + nblk - 1 - i, 1)
    kq = lambda im: pl.BlockSpec((tl, GLA_KEY), im)
    vv = lambda im: pl.BlockSpec((tl, GLA_VAL), im)
    return pl.pallas_call(
        _gla_kernel,
        grid=(n_seq, nblk),
        in_specs=[kq(fwd), kq(fwd), vv(fwd), kq(fwd), kq(bwd), kq(bwd), vv(bwd), kq(bwd_la)],
        out_specs=[vv(fwd), vv(bwd)],
        out_shape=[jax.ShapeDtypeStruct((n, GLA_VAL), bf16), jax.ShapeDtypeStruct((n, GLA_VAL), bf16)],
        scratch_shapes=[pltpu.VMEM((2, GLA_HEADS // 2, PAIR_DK, PAIR_DV), f32), pltpu.VMEM((GLA_CUM_ROWS, GLA_CUM_ROWS), bf16)],
        compiler_params=_cparams(("parallel", "arbitrary")),
        name="gla",
    )(q, k, v, la, q, k, v, la)


def _conv_kernel(prev_ref, cur_ref, next_ref, wdw_ref, bdw_ref, gln_ref, bln_ref, wpw_ref,
                 o_ref, ext_ref, y_ref, *, tiles_per_seq):
    tl = cur_ref.shape[0]
    pos = pl.program_id(0) % tiles_per_seq
    prev = jnp.where(pos == 0, 0.0, prev_ref[...].astype(f32))
    nxt = jnp.where(pos == tiles_per_seq - 1, 0.0, next_ref[...].astype(f32))
    cur = cur_ref[...].astype(f32)
    n_slabs = CONV_CH // LANES
    for c in range(n_slabs):
        lanes = slice(c * LANES, (c + 1) * LANES)
        ext_ref[c, 0:CONV_HALO, :] = prev[:, lanes]
        ext_ref[c, CONV_HALO:CONV_HALO + tl, :] = cur[:, lanes]
        ext_ref[c, CONV_HALO + tl:, :] = nxt[:, lanes]
    off = CONV_HALO - CONV_K // 2

    def body(rb, carry):
        base = pl.multiple_of(rb * CONV_ROWS, CONV_ROWS)
        for c in range(n_slabs):
            lanes = slice(c * LANES, (c + 1) * LANES)
            a = jnp.zeros((CONV_ROWS, LANES), f32) + bdw_ref[:, lanes]
            for j in range(CONV_K):
                taps = ext_ref.at[c, pl.ds(off + j, tl)]
                a = a + taps[pl.ds(base, CONV_ROWS), :] * wdw_ref[j:j + 1, lanes]
            y_ref[pl.ds(base, CONV_ROWS), lanes] = a
        return carry

    lax.fori_loop(0, tl // CONV_ROWS, body, 0)
    acc = y_ref[...]
    mu = jnp.mean(acc, axis=-1, keepdims=True)
    xc = acc - mu
    yn = xc * lax.rsqrt(jnp.mean(xc * xc, axis=-1, keepdims=True) + EPS) * gln_ref[...] + bln_ref[...]
    o_ref[...] = _dot((yn * jax.nn.sigmoid(yn)).astype(bf16), wpw_ref[...]).astype(bf16)


def _conv(glu, wdw, bdw, gln, bln, wpw, seq_len, tl):
    n = glu.shape[0]
    hb = tl // CONV_HALO
    n_halo = n // CONV_HALO
    full = lambda a: pl.BlockSpec(a.shape, lambda i: (0, 0))
    return pl.pallas_call(
        functools.partial(_conv_kernel, tiles_per_seq=seq_len // tl),
        grid=(n // tl,),
        in_specs=[pl.BlockSpec((CONV_HALO, CONV_CH), lambda i: (jnp.maximum(i * hb - 1, 0), 0)),
                  pl.BlockSpec((tl, CONV_CH), lambda i: (i, 0)),
                  pl.BlockSpec((CONV_HALO, CONV_CH), lambda i: (jnp.minimum((i + 1) * hb, n_halo - 1), 0)),
                  full(wdw), full(bdw), full(gln), full(bln), full(wpw)],
        out_specs=pl.BlockSpec((tl, CONV_CH), lambda i: (i, 0)),
        out_shape=jax.ShapeDtypeStruct((n, CONV_CH), bf16),
        scratch_shapes=[pltpu.VMEM((CONV_CH // LANES, tl + 2 * CONV_HALO, LANES), f32),
                        pltpu.VMEM((tl, CONV_CH), f32)],
        compiler_params=_cparams(("parallel",)),
        name="conv",
    )(glu, glu, glu, wdw, bdw, gln, bln, wpw)


def _mix_kernel(x_ref, of_ref, ob_ref, og_ref, oc_ref, gh_ref, woa_ref, wob_ref, gf_ref, wr_ref, br_ref,
                h_ref, up_ref, eid_ref, wts_ref):
    tm = x_ref.shape[0]
    o = of_ref[...].astype(f32) + ob_ref[...].astype(f32)
    o = jnp.concatenate([_rms(o[:, h * GLA_DV:(h + 1) * GLA_DV]) for h in range(GLA_HEADS)], axis=1)
    og = og_ref[...].astype(f32)
    o = o * gh_ref[...] * (og * jax.nn.sigmoid(og))
    h = x_ref[...] + _dot(o.astype(bf16), woa_ref[...]) + _dot(oc_ref[...], wob_ref[...])
    h_ref[...] = h
    u = _rms(h) * gf_ref[...]
    u_hi, u_lo = _split_bf16(u)
    up_ref[...] = u_hi

    w_hi, w_lo = _split_bf16(wr_ref[...])
    n_r = w_hi.shape[0]
    both = _dot_nt(jnp.concatenate([w_hi, w_lo], axis=0), u_hi)
    logits = both[:n_r] + both[n_r:] + _dot_nt(w_hi, u_lo) + br_ref[...]
    le = logits[0:N_EXPERTS]
    lg = logits[N_EXPERTS:N_EXPERTS + N_GROUPS]
    gmax = jnp.max(lg, axis=0, keepdims=True)
    gi = lax.broadcasted_iota(i32, lg.shape, 0)
    gsel = jnp.min(jnp.where(lg == gmax, gi, N_GROUPS), axis=0, keepdims=True)
    gate = 1.0 / jnp.sum(jnp.exp(lg - gmax), axis=0, keepdims=True)
    ri = lax.broadcasted_iota(i32, le.shape, 0)
    lm = jnp.where(ri // EXPERTS_PER_GROUP == gsel, le, -jnp.inf)
    m1 = jnp.max(lm, axis=0, keepdims=True)
    i1 = jnp.min(jnp.where(lm == m1, ri, N_EXPERTS), axis=0, keepdims=True)
    lm2 = jnp.where(ri == i1, -jnp.inf, lm)
    m2 = jnp.max(lm2, axis=0, keepdims=True)
    i2 = jnp.min(jnp.where(lm2 == m2, ri, N_EXPERTS), axis=0, keepdims=True)
    t = jnp.exp(m2 - m1)
    den = 1.0 / (1.0 + t)
    r8 = lax.broadcasted_iota(i32, (SUBLANES, tm), 0)
    eid_ref[...] = jnp.where(r8 == 0, i1, jnp.where(r8 == 1, i2, 0))
    wts_ref[...] = jnp.where(r8 == 0, gate * den, jnp.where(r8 == 1, gate * t * den, 0.0))


def _mix(x, o_f, o_b, og, oc, g_head, woa, wob, g_ffn, wr, br, tm):
    n = x.shape[0]
    row = lambda w: pl.BlockSpec((tm, w), lambda i: (i, 0))
    col = pl.BlockSpec((SUBLANES, tm), lambda i: (0, i))
    full = lambda a: pl.BlockSpec(a.shape, lambda i: (0, 0))
    return pl.pallas_call(
        _mix_kernel,
        grid=(n // tm,),
        in_specs=[row(D_MODEL), row(GLA_VAL), row(GLA_VAL), row(GLA_VAL), row(CONV_CH),
                  full(g_head), full(woa), full(wob), full(g_ffn), full(wr), full(br)],
        out_specs=[row(D_MODEL), row(D_MODEL), col, col],
        out_shape=[jax.ShapeDtypeStruct((n, D_MODEL), f32), jax.ShapeDtypeStruct((n, D_MODEL), bf16),
                   jax.ShapeDtypeStruct((SUBLANES, n), i32), jax.ShapeDtypeStruct((SUBLANES, n), f32)],
        compiler_params=_cparams(("parallel",)),
        name="mix",
    )(x, o_f, o_b, og, oc, g_head, woa, wob, g_ffn, wr, br)


def _rank_kernel(eid_ref, pos_ref, tbl_ref, blk_ref, tail_ref, cnt_ref, base_ref, end_ref, *, blk, nbp, tm):
    ps = pl.program_id(0)
    i = pl.program_id(1)
    rr = lax.broadcasted_iota(i32, (N_EXPERTS, LANES), 0)

    def cumsum_experts(x):
        for s in (1, 2, 4, 8, 16):
            x = x + jnp.where(rr >= s, pltpu.roll(x, s, axis=0), 0.0)
        return x

    @pl.when((ps == 0) & (i == 0))
    def _():
        cnt_ref[...] = jnp.zeros_like(cnt_ref)

    @pl.when(ps == 0)
    def _():
        pos_ref[...] = jnp.zeros_like(pos_ref)
        tbl_ref[...] = jnp.zeros_like(tbl_ref)

    @pl.when((ps == 1) & (i == 0))
    def _():
        cnt = cnt_ref[...]
        pc = jnp.floor((cnt + (blk - 1)) * (1.0 / blk)) * blk
        inc = cumsum_experts(pc)
        base_ref[...] = inc - pc
        end_ref[...] = inc
        pend = jnp.concatenate([inc] * (nbp // LANES), axis=1)
        jl = lax.broadcasted_iota(i32, (N_EXPERTS, nbp), 1).astype(f32) * blk
        be = jnp.minimum(jnp.sum(jnp.where(pend <= jl, 1.0, 0.0), axis=0, keepdims=True), N_EXPERTS - 1.0)
        nused = jnp.concatenate([inc[N_EXPERTS - 1:N_EXPERTS, :]] * (nbp // LANES), axis=1) * (1.0 / blk)
        r8 = lax.broadcasted_iota(i32, (SUBLANES, nbp), 0)
        blk_ref[...] = jnp.where(r8 == 0, be, jnp.where(r8 == 1, nused, 0.0)).astype(i32)

    ri = lax.broadcasted_iota(i32, (N_EXPERTS, tm), 0)
    for t in range(eid_ref.shape[1] // tm):
        lanes = slice(t * tm, (t + 1) * tm)
        oh1 = ri == eid_ref[0:1, lanes]
        oh2 = ri == eid_ref[1:2, lanes]
        ohf = jnp.where(oh1 | oh2, 1.0, 0.0)
        tile_cnt = jnp.sum(ohf, axis=1, keepdims=True)
        run_len = (jnp.floor((tile_cnt + (RUN_ALIGN - 1)) * (1.0 / RUN_ALIGN)) * RUN_ALIGN
                   + jnp.zeros((N_EXPERTS, LANES), f32))

        @pl.when(ps == 0)
        def _():
            cnt_ref[...] += run_len

        @pl.when(ps == 1)
        def _():
            a = lax.broadcasted_iota(i32, (tm, tm), 0)
            b = lax.broadcasted_iota(i32, (tm, tm), 1)
            upper = jnp.where(a < b, 1.0, 0.0).astype(bf16)
            before = _dot(ohf.astype(bf16), upper)
            run_start = cumsum_experts(run_len) - run_len
            row = run_start[:, 0:1] + before
            p1 = jnp.sum(jnp.where(oh1, row, 0.0), axis=0, keepdims=True)
            p2 = jnp.sum(jnp.where(oh2, row, 0.0), axis=0, keepdims=True)
            r8 = lax.broadcasted_iota(i32, (SUBLANES, tm), 0)
            pos_ref[0, :, lanes] = jnp.where(r8 == 0, p1, jnp.where(r8 == 1, p2, 0.0)).astype(i32)
            diag = lax.broadcasted_iota(i32, (N_EXPERTS, LANES), 1) == rr
            on_lanes = lambda x: jnp.sum(jnp.where(diag, x, 0.0), axis=0, keepdims=True)
            big = jnp.floor(run_len * (1.0 / BIG_CHUNK))
            small = (run_len - big * BIG_CHUNK) * (1.0 / RUN_ALIGN)
            rows = (on_lanes(run_start), on_lanes(base_ref[...]), on_lanes(big), on_lanes(small),
                    jnp.sum(run_len, axis=0, keepdims=True))
            t8 = lax.broadcasted_iota(i32, (SUBLANES, LANES), 0)
            tbl = jnp.zeros((SUBLANES, LANES), f32)
            for k, x in enumerate(rows):
                tbl = jnp.where(t8 == k, x, tbl)
            tbl_ref[0, t] = tbl.astype(i32)
            base_ref[...] += run_len

    @pl.when((ps == 1) & (i == pl.num_programs(1) - 1))
    def _():
        diag = lax.broadcasted_iota(i32, (N_EXPERTS, LANES), 1) == rr
        on_lanes = lambda x: jnp.sum(jnp.where(diag, x, 0.0), axis=0, keepdims=True)
        first = base_ref[...]
        count = end_ref[...] - first
        rows = (on_lanes(first), on_lanes(count), jnp.sum(count, axis=0, keepdims=True),
                end_ref[N_EXPERTS - 1:N_EXPERTS, :] * (1.0 / blk))
        t8 = lax.broadcasted_iota(i32, (SUBLANES, LANES), 0)
        tail = jnp.zeros((SUBLANES, LANES), f32)
        for k, x in enumerate(rows):
            tail = jnp.where(t8 == k, x, tail)
        tail_ref[...] = tail.astype(i32)


def _rank(eid, blk, nbp, tm):
    n = eid.shape[1]
    group = _tile(n // tm, RANK_GROUP)
    return pl.pallas_call(
        functools.partial(_rank_kernel, blk=blk, nbp=nbp, tm=tm),
        grid=(2, n // (tm * group)),
        in_specs=[pl.BlockSpec((SUBLANES, tm * group), lambda p, i: (0, i))],
        out_specs=[pl.BlockSpec((1, SUBLANES, tm * group), lambda p, i: (p, 0, i)),
                   pl.BlockSpec((1, group, SUBLANES, LANES), lambda p, i: (p, i, 0, 0)),
                   pl.BlockSpec((SUBLANES, nbp), lambda p, i: (0, 0)),
                   pl.BlockSpec((SUBLANES, LANES), lambda p, i: (0, 0))],
        out_shape=[jax.ShapeDtypeStruct((2, SUBLANES, n), i32),
                   jax.ShapeDtypeStruct((2, n // tm, SUBLANES, LANES), i32),
                   jax.ShapeDtypeStruct((SUBLANES, nbp), i32),
                   jax.ShapeDtypeStruct((SUBLANES, LANES), i32)],
        scratch_shapes=[pltpu.VMEM((N_EXPERTS, LANES), f32)] * 3,
        compiler_params=_cparams(("arbitrary", "arbitrary")),
        name="rank",
    )(eid)


def _start_run_copies(tbl_ref, copy):
    def per_expert(e, carry):
        sorted0 = tbl_ref[0, e]
        slot0 = tbl_ref[1, e]
        n_big = tbl_ref[2, e]

        def pieces(rows, first):
            def body(c, carry2):
                off = first + c * rows
                copy(pl.multiple_of(sorted0 + off, RUN_ALIGN), pl.multiple_of(slot0 + off, RUN_ALIGN), rows).start()
                return carry2
            return body

        lax.fori_loop(0, n_big, pieces(BIG_CHUNK, 0), 0)
        lax.fori_loop(0, tbl_ref[3, e], pieces(RUN_ALIGN, n_big * BIG_CHUNK), 0)
        return carry

    lax.fori_loop(0, N_EXPERTS, per_expert, 0)


def _wait_rows(total_rows, copy):
    def waits(rows):
        def body(c, carry):
            copy(0, 0, rows).wait()
            return carry
        return body

    n_wide = total_rows // WAIT_ROWS
    lax.fori_loop(0, n_wide, waits(WAIT_ROWS), 0)
    lax.fori_loop(0, (total_rows - n_wide * WAIT_ROWS) // RUN_ALIGN, waits(RUN_ALIGN), 0)


def _dispatch_kernel(tbl_ref, tail_ref, pos_ref, u_ref, xs_ref, sorted_ref, zero_ref, sems, pending_ref):
    i = pl.program_id(0)
    tm = u_ref.shape[0]
    buf = i % 2

    def copy_from(b):
        def copy(sorted_row, slot_row, rows):
            return pltpu.make_async_copy(sorted_ref.at[b, pl.ds(sorted_row, rows)],
                                         xs_ref.at[pl.ds(slot_row, rows)], sems.at[b])
        return copy

    @pl.when(i == 0)
    def _():
        pending_ref[0] = 0
        pending_ref[1] = 0

    _wait_rows(pending_ref[buf], copy_from(buf))
    p1 = pos_ref[0:1, :]
    p2 = pos_ref[1:2, :]
    u = u_ref[...]
    ri = lax.broadcasted_iota(i32, (SORT_ROWS, tm), 0).astype(f32).astype(bf16)
    one = jnp.ones((SORT_ROWS, tm), bf16)
    for r0 in range(0, sorted_ref.shape[1], SORT_ROWS):
        q1 = (p1 - r0).astype(f32).astype(bf16)
        q2 = (p2 - r0).astype(f32).astype(bf16)
        sel = jnp.where((ri == q1) | (ri == q2), one, jnp.zeros_like(one))
        sorted_ref[buf, r0:r0 + SORT_ROWS, :] = _pack_exact_rows(_dot(sel, u))
    _start_run_copies(tbl_ref, copy_from(buf))
    pending_ref[buf] = tbl_ref[4, 0]

    @pl.when(i == pl.num_programs(0) - 1)
    def _():
        for b in range(2):
            _wait_rows(pending_ref[b], copy_from(b))
        blk = zero_ref.shape[0]
        n_blocks = xs_ref.shape[0] // blk
        zero_ref[...] = jnp.zeros_like(zero_ref)

        def zero_copy(unused_row, slot_row, rows):
            return pltpu.make_async_copy(zero_ref.at[pl.ds(0, rows)], xs_ref.at[pl.ds(slot_row, rows)], sems.at[2])

        def per_expert(e, carry):
            first = tail_ref[0, e]
            n_big = tail_ref[1, e] // BIG_CHUNK

            def pieces(rows, start):
                def body(c, carry2):
                    zero_copy(0, pl.multiple_of(first + start + c * rows, RUN_ALIGN), rows).start()
                    return carry2
                return body

            lax.fori_loop(0, n_big, pieces(BIG_CHUNK, 0), 0)
            lax.fori_loop(0, (tail_ref[1, e] - n_big * BIG_CHUNK) // RUN_ALIGN, pieces(RUN_ALIGN, n_big * BIG_CHUNK), 0)
            return carry

        lax.fori_loop(0, N_EXPERTS, per_expert, 0)
        n_used = tail_ref[3, 0]

        def per_block(j, carry):
            zero_copy(0, pl.multiple_of(j * blk, blk), blk).start()
            return carry

        lax.fori_loop(n_used, n_blocks, per_block, 0)
        _wait_rows(tail_ref[2, 0] + (n_blocks - n_used) * blk, zero_copy)


def _dispatch(tbl, tail, pos, up, n_slots, tm, rmax):
    n = up.shape[0]
    return pl.pallas_call(
        _dispatch_kernel,
        grid=(n // tm,),
        in_specs=[pl.BlockSpec((SUBLANES, LANES), lambda i: (i, 0), memory_space=pltpu.SMEM),
                  pl.BlockSpec((SUBLANES, LANES), lambda i: (0, 0), memory_space=pltpu.SMEM),
                  pl.BlockSpec((SUBLANES, tm), lambda i: (0, i)),
                  pl.BlockSpec((tm, D_MODEL), lambda i: (i, 0))],
        out_specs=pl.BlockSpec(memory_space=pl.ANY),
        out_shape=jax.ShapeDtypeStruct((n_slots, HALF), u32),
        scratch_shapes=[pltpu.VMEM((2, rmax, HALF), u32), pltpu.VMEM((MOE_BLOCK, HALF), u32),
                        pltpu.SemaphoreType.DMA((3,)), pltpu.SMEM((2,), i32)],
        compiler_params=_cparams(("arbitrary",)),
        name="dispatch",
    )(tbl, tail, pos, up)


def _expert_kernel(be_ref, nu_ref, xs_ref, wg_hbm, wu_hbm, wd_hbm, y_ref,
                   wg_f32, wu_f32, wd_f32, wg_bf, wu_bf, wd_bf, sems, slot_ref):
    j = pl.program_id(0)
    n_steps = pl.num_programs(0)
    expert = be_ref[j]

    def fetch(e, s):
        return [pltpu.make_async_copy(hbm.at[e], stage.at[s], sems.at[s, k])
                for k, (hbm, stage) in enumerate(((wg_hbm, wg_f32), (wu_hbm, wu_f32), (wd_hbm, wd_f32)))]

    @pl.when(j == 0)
    def _():
        slot_ref[0] = 0
        for c in fetch(expert, 0):
            c.start()

    @pl.when((j == 0) | (expert != be_ref[jnp.maximum(j - 1, 0)]))
    def _():
        s = slot_ref[0]
        for c in fetch(expert, s):
            c.wait()
        wg_bf[...] = wg_f32[s].astype(bf16)
        wu_bf[...] = wu_f32[s].astype(bf16)
        wd_bf[...] = wd_f32[s].astype(bf16)
        nxt = lax.while_loop(lambda k: (k < n_steps) & (be_ref[jnp.minimum(k, n_steps - 1)] == expert),
                             lambda k: k + 1, j + 1)

        @pl.when(nxt < n_steps)
        def _():
            for c in fetch(be_ref[jnp.minimum(nxt, n_steps - 1)], 1 - s):
                c.start()

        slot_ref[0] = 1 - s

    @pl.when(j < nu_ref[0])
    def _():
        xa, xb = _unpack_rows(xs_ref[...])
        x = jnp.concatenate([xa.astype(bf16), xb.astype(bf16)], axis=1)
        g = _dot(x, wg_bf[...])
        u = _dot(x, wu_bf[...])
        hb = (g * jax.nn.sigmoid(g) * u).astype(bf16)
        y_ref[...] = _pack_rows(_dot(hb, wd_bf[...]))

    @pl.when(j >= nu_ref[0])
    def _():
        y_ref[...] = jnp.zeros_like(y_ref)


def _experts(block_e, nused, xs, wg, wu, wd, blk):
    nb = xs.shape[0] // blk
    used = lambda j, be, nu: (jnp.minimum(j, nu[0] - 1), 0)
    hbm = pl.BlockSpec(memory_space=pl.ANY)
    return pl.pallas_call(
        _expert_kernel,
        grid_spec=pltpu.PrefetchScalarGridSpec(
            num_scalar_prefetch=2,
            grid=(nb,),
            in_specs=[pl.BlockSpec((blk, HALF), used), hbm, hbm, hbm],
            out_specs=pl.BlockSpec((blk, HALF), lambda j, be, nu: (j, 0)),
            scratch_shapes=[pltpu.VMEM((2, D_MODEL, D_EXPERT), f32), pltpu.VMEM((2, D_MODEL, D_EXPERT), f32),
                            pltpu.VMEM((2, D_EXPERT, D_MODEL), f32),
                            pltpu.VMEM((D_MODEL, D_EXPERT), bf16), pltpu.VMEM((D_MODEL, D_EXPERT), bf16),
                            pltpu.VMEM((D_EXPERT, D_MODEL), bf16),
                            pltpu.SemaphoreType.DMA((2, 3)), pltpu.SMEM((1,), i32)],
        ),
        out_shape=jax.ShapeDtypeStruct(xs.shape, u32),
        compiler_params=_cparams(("arbitrary",)),
        name="experts",
    )(block_e, nused, xs, wg, wu, wd)


def _combine_kernel(tbl_ref, tbl_next_ref, pos_ref, wts_ref, h_ref, gfin_ref, y_ref, o_ref, ys_ref, sems):
    i = pl.program_id(0)
    tm = h_ref.shape[0]
    buf = i % 2

    def copy_into(b):
        def copy(sorted_row, slot_row, rows):
            return pltpu.make_async_copy(y_ref.at[pl.ds(slot_row, rows)],
                                         ys_ref.at[b, pl.ds(sorted_row, rows)], sems.at[b])
        return copy

    @pl.when(i == 0)
    def _():
        ys_ref[...] = jnp.zeros_like(ys_ref)
        _start_run_copies(tbl_ref, copy_into(0))

    @pl.when(i + 1 < pl.num_programs(0))
    def _():
        _start_run_copies(tbl_next_ref, copy_into(1 - buf))

    pad = jnp.zeros((LANES - SUBLANES, tm), f32)
    pos_cols = jnp.transpose(jnp.concatenate([pos_ref[...].astype(f32), pad], axis=0))
    w_cols = jnp.transpose(jnp.concatenate([wts_ref[...], pad], axis=0))
    p1, p2 = pos_cols[:, 0:1], pos_cols[:, 1:2]
    w1, w2 = w_cols[:, 0:1], w_cols[:, 1:2]
    h = h_ref[...]
    ha = h[:, :HALF]
    hb = h[:, HALF:]
    _wait_rows(tbl_ref[4, 0], copy_into(buf))
    li = lax.broadcasted_iota(i32, (tm, SORT_ROWS), 1).astype(f32).astype(bf16)
    w1b = w1.astype(bf16) + jnp.zeros((tm, SORT_ROWS), bf16)
    w2b = w2.astype(bf16) + jnp.zeros((tm, SORT_ROWS), bf16)
    none = jnp.zeros((tm, SORT_ROWS), bf16)
    sel = jnp.concatenate(
        [jnp.where(li == (p1 - r0).astype(bf16), w1b, none) + jnp.where(li == (p2 - r0).astype(bf16), w2b, none)
         for r0 in range(0, ys_ref.shape[1], SORT_ROWS)], axis=1)
    ya, yb = _unpack_rows(ys_ref[buf])
    ha = ha + _dot(sel, ya.astype(bf16))
    hb = hb + _dot(sel, yb.astype(bf16))
    ms = (jnp.sum(ha * ha, axis=-1, keepdims=True) + jnp.sum(hb * hb, axis=-1, keepdims=True)) * (1.0 / D_MODEL)
    inv = lax.rsqrt(ms + EPS)
    o_ref[:, :HALF] = ha * inv * gfin_ref[:, :HALF]
    o_ref[:, HALF:] = hb * inv * gfin_ref[:, HALF:]


def _combine(tbl, pos, wts, h, g_final, ybuf, tm, rmax):
    n = h.shape[0]
    n_tiles = n // tm
    col = pl.BlockSpec((SUBLANES, tm), lambda i: (0, i))
    return pl.pallas_call(
        _combine_kernel,
        grid=(n_tiles,),
        in_specs=[pl.BlockSpec((SUBLANES, LANES), lambda i: (i, 0), memory_space=pltpu.SMEM),
                  pl.BlockSpec((SUBLANES, LANES), lambda i: (jnp.minimum(i + 1, n_tiles - 1), 0),
                               memory_space=pltpu.SMEM),
                  col, col,
                  pl.BlockSpec((tm, D_MODEL), lambda i: (i, 0)),
                  pl.BlockSpec(g_final.shape, lambda i: (0, 0)),
                  pl.BlockSpec(memory_space=pl.ANY)],
        out_specs=pl.BlockSpec((tm, D_MODEL), lambda i: (i, 0)),
        out_shape=jax.ShapeDtypeStruct((n, D_MODEL), f32),
        scratch_shapes=[pltpu.VMEM((2, rmax, HALF), u32), pltpu.SemaphoreType.DMA((2,))],
        compiler_params=_cparams(("arbitrary",)),
        name="combine",
    )(tbl, tbl, pos, wts, h, g_final, ybuf)


def _tile(n, pref):
    t = pref
    while n % t:
        t //= 2
    return t


def _prep_weights(g_mix, w_in, w_af2, b_af, w_ab2, b_ab, g_head, w_dw, b_dw, g_ln, b_ln, w_pw2, w_out,
                  g_ffn, w_rg, b_rg, w_re, b_re, w_e_gate, w_e_up, w_e_down, g_final):
    n_qkvo = 2 * GLA_KEY + 2 * GLA_VAL
    n_a = 2 * GATE_RANK
    w_in = w_in[0]
    zg = jnp.zeros((GATE_RANK, GLA_KEY), f32)
    wgate = jnp.concatenate([jnp.concatenate([w_af2[0], zg], axis=1), jnp.concatenate([zg, w_ab2[0]], axis=1)], axis=0)
    wdw = jnp.concatenate([w_dw[0], jnp.zeros((1, CONV_CH), f32)], axis=0)
    n_pad = 2 * SUBLANES - N_GROUPS
    wr = jnp.concatenate([w_re[0].T, w_rg[0].T, jnp.zeros((n_pad, D_MODEL), f32)], axis=0)
    br = jnp.concatenate([b_re[0], b_rg[0], jnp.zeros((n_pad,), f32)])[:, None]
    return dict(
        g_mix=g_mix[0][None, :],
        wqkvo=w_in[:, :n_qkvo].astype(bf16),
        wa=w_in[:, n_qkvo:n_qkvo + n_a].astype(bf16),
        wc=w_in[:, n_qkvo + n_a:].astype(bf16),
        wgate=wgate.astype(bf16),
        bgate=jnp.concatenate([b_af[0], b_ab[0]])[None, :],
        g_head=g_head[0][None, :],
        wdw=wdw, bdw=b_dw[0][None, :], gln=g_ln[0][None, :], bln=b_ln[0][None, :],
        wpw=w_pw2[0].astype(bf16),
        woa=w_out[0, :GLA_VAL].astype(bf16), wob=w_out[0, GLA_VAL:].astype(bf16),
        g_ffn=g_ffn[0][None, :], wr=wr, br=br,
        wg=w_e_gate[0], wu=w_e_up[0], wd=w_e_down[0],
        g_final=g_final[None, :],
    )


def _trunk(x3, w):
    n_seq, seq_len, _ = x3.shape
    n = n_seq * seq_len
    x = x3.reshape(n, D_MODEL)
    tm = _tile(n, MOE_TILE)
    q, k, v, og, la, glu = _inproj(x, w["g_mix"], w["wqkvo"], w["wa"], w["wc"], w["wgate"], w["bgate"],
                                   _tile(n, 1024))
    o_f, o_b = _gla(q, k, v, la, n_seq, seq_len)
    oc = _conv(glu, w["wdw"], w["bdw"], w["gln"], w["bln"], w["wpw"], seq_len, _tile(seq_len, CONV_TILE))
    h, up, eid, wts = _mix(x, o_f, o_b, og, oc, w["g_head"], w["woa"], w["wob"], w["g_ffn"], w["wr"], w["br"], tm)

    n_tiles = n // tm
    max_slots = n * TOP_K + n_tiles * N_EXPERTS * (RUN_ALIGN - 1)
    nb = -(-max_slots // MOE_BLOCK) + N_EXPERTS
    nbp = -(-nb // LANES) * LANES
    rmax = -(-(tm * TOP_K + N_EXPERTS * (RUN_ALIGN - 1)) // SORT_ROWS) * SORT_ROWS
    pos, tbl, blkinfo, tail = _rank(eid, MOE_BLOCK, nbp, tm)
    pos = pos[1]
    tbl = tbl[1].reshape(n_tiles * SUBLANES, LANES)
    xs = _dispatch(tbl, tail, pos, up, nb * MOE_BLOCK, tm, rmax)
    ybuf = _experts(blkinfo[0], blkinfo[1, :1], xs, w["wg"], w["wu"], w["wd"], MOE_BLOCK)
    y = _combine(tbl, pos, wts, h, w["g_final"], ybuf, tm, rmax)
    return y.reshape(x3.shape)


def kernel(x_prompt, x_sample, g_mix, w_in, w_af2, b_af, w_ab2, b_ab, g_head, w_dw, b_dw, g_ln, b_ln, w_pw2,
           w_out, g_ffn, w_rg, b_rg, w_re, b_re, w_e_gate, w_e_up, w_e_down, g_final):
    w = _prep_weights(g_mix, w_in, w_af2, b_af, w_ab2, b_ab, g_head, w_dw, b_dw, g_ln, b_ln, w_pw2, w_out,
                      g_ffn, w_rg, b_rg, w_re, b_re, w_e_gate, w_e_up, w_e_down, g_final)
    return (_trunk(x_prompt, w), _trunk(x_sample, w))
```

```python
import functools

import jax
import jax.numpy as jnp
from jax import lax
from jax.experimental import pallas as pl
from jax.experimental.pallas import tpu as pltpu

f32 = jnp.float32
bf16 = jnp.bfloat16
i32 = jnp.int32
u32 = jnp.uint32

D_MODEL = 1024
GLA_HEADS = 4
GLA_DK = 64
GLA_DV = 128
GLA_KEY = GLA_HEADS * GLA_DK
GLA_VAL = GLA_HEADS * GLA_DV
GATE_RANK = 16
GATE_NORMALIZER = 16
CONV_CH = 512
CONV_K = 31
N_GROUPS = 4
EXPERTS_PER_GROUP = 8
N_EXPERTS = N_GROUPS * EXPERTS_PER_GROUP
TOP_K = 2
D_EXPERT = 512
EPS = 1e-6

LANES = 128
SUBLANES = 8
GLA_CHUNK = 128
GLA_TILE = 1024
GLA_CUM_ROWS = 256
PAIR_DK = 2 * GLA_DK
PAIR_DV = 2 * GLA_DV
CONV_HALO = 16
CONV_ROWS = 32
CONV_TILE = 1024
HALF = D_MODEL // 2
MOE_TILE = 512
MOE_BLOCK = 512
RUN_ALIGN = SUBLANES
SORT_ROWS = 256
RANK_GROUP = 8
DISPATCH_GROUP = 2
BIG_CHUNK = 32
WAIT_ROWS = 256
VMEM_LIMIT = 56 * 1024 * 1024


def _cparams(sem, **kw):
    return pltpu.CompilerParams(dimension_semantics=sem, vmem_limit_bytes=VMEM_LIMIT, **kw)


def _dot(a, b):
    return jnp.dot(a, b, preferred_element_type=f32)


def _dot_nt(a, b):
    return lax.dot_general(a, b, (((1,), (1,)), ((), ())), preferred_element_type=f32)


def _dot_tn(a, b):
    return lax.dot_general(a, b, (((0,), (0,)), ((), ())), preferred_element_type=f32)


def _split_bf16(x):
    hi = x.astype(bf16)
    lo = (x - hi.astype(f32)).astype(bf16)
    return hi, lo


def _pack_rows(x):
    a = lax.bitcast_convert_type(x[:, :HALF].astype(bf16).astype(f32), u32)
    b = lax.bitcast_convert_type(x[:, HALF:].astype(bf16).astype(f32), u32)
    return a | (b >> 16)


def _pack_exact_rows(x):
    a = lax.bitcast_convert_type(x[:, :HALF], u32)
    b = lax.bitcast_convert_type(x[:, HALF:], u32)
    return a | (b >> 16)


def _unpack_rows(w):
    a = lax.bitcast_convert_type(w & jnp.uint32(0xFFFF0000), f32)
    b = lax.bitcast_convert_type(w << 16, f32)
    return a, b


def _rms(x):
    return x * lax.rsqrt(jnp.mean(x * x, axis=-1, keepdims=True) + EPS)


def _inproj_kernel(x_ref, g_ref, wqkvo_ref, wa_ref, wc_ref, wgate_ref, bgate_ref,
                   q_ref, k_ref, v_ref, og_ref, la_ref, glu_ref):
    u = (_rms(x_ref[...]) * g_ref[...]).astype(bf16)
    a = _dot(u, wa_ref[...]).astype(bf16)
    p = _dot(u, wqkvo_ref[...])
    q_ref[...] = (p[:, :GLA_KEY] * (GLA_DK ** -0.5)).astype(bf16)
    k_ref[...] = p[:, GLA_KEY:2 * GLA_KEY].astype(bf16)
    v_ref[...] = p[:, 2 * GLA_KEY:2 * GLA_KEY + GLA_VAL].astype(bf16)
    og_ref[...] = p[:, 2 * GLA_KEY + GLA_VAL:].astype(bf16)
    z = _dot(a, wgate_ref[...]) + bgate_ref[...]
    la_ref[...] = (jnp.minimum(z, 0.0) - jnp.log1p(jnp.exp(-jnp.abs(z)))) * (1.0 / GATE_NORMALIZER)
    c = _dot(u, wc_ref[...])
    glu_ref[...] = (c[:, :CONV_CH] * jax.nn.sigmoid(c[:, CONV_CH:])).astype(bf16)


def _inproj(x, g_mix, wqkvo, wa, wc, wgate, bgate, tm):
    n = x.shape[0]
    row = lambda w: pl.BlockSpec((tm, w), lambda i: (i, 0))
    full = lambda a: pl.BlockSpec(a.shape, lambda i: (0, 0))
    return pl.pallas_call(
        _inproj_kernel,
        grid=(n // tm,),
        in_specs=[row(D_MODEL), full(g_mix), full(wqkvo), full(wa), full(wc), full(wgate), full(bgate)],
        out_specs=[row(GLA_KEY), row(GLA_KEY), row(GLA_VAL), row(GLA_VAL), row(2 * GLA_KEY), row(CONV_CH)],
        out_shape=[jax.ShapeDtypeStruct((n, GLA_KEY), bf16), jax.ShapeDtypeStruct((n, GLA_KEY), bf16),
                   jax.ShapeDtypeStruct((n, GLA_VAL), bf16), jax.ShapeDtypeStruct((n, GLA_VAL), bf16),
                   jax.ShapeDtypeStruct((n, 2 * GLA_KEY), f32), jax.ShapeDtypeStruct((n, CONV_CH), bf16)],
        compiler_params=_cparams(("parallel",)),
        name="inproj",
    )(x, g_mix, wqkvo, wa, wc, wgate, bgate)


def _gla_unit(q_ref, k_ref, v_ref, cs, la, pair, reverse, masks):
    head0_k, head0_v, keep = masks
    tl = cs.shape[0]
    kl = slice(pair * PAIR_DK, (pair + 1) * PAIR_DK)
    vl = slice(pair * PAIR_DV, (pair + 1) * PAIR_DV)
    n_chunks = tl // GLA_CHUNK
    order = list(range(n_chunks - 1, -1, -1) if reverse else range(n_chunks))

    def stack_heads(x, head0):
        z = jnp.zeros_like(x)
        return jnp.concatenate([jnp.where(head0, x, z), jnp.where(head0, z, x)], axis=0)

    local = {}
    for j in order:
        rows = slice(j * GLA_CHUNK, (j + 1) * GLA_CHUNK)
        cj = cs[rows, kl]
        q = q_ref[rows, kl].astype(f32)
        k = k_ref[rows, kl].astype(f32)
        c_last = cj[GLA_CHUNK - 1:GLA_CHUNK, :]
        mid = 0.5 * c_last
        rel = (mid - (cj - la[rows, kl])) if reverse else (cj - mid)
        qd = q * jnp.exp(rel)
        kd = k * jnp.exp(-rel)
        edge = jnp.exp(mid)
        qi = qd * edge
        ku = kd * edge
        vbd = stack_heads(v_ref[rows, vl], head0_v)
        sc = _dot_nt(qd.astype(bf16), stack_heads(kd.astype(bf16), head0_k))
        p = jnp.where(keep, sc, 0.0).astype(bf16)
        kv = _dot_tn(stack_heads(ku.astype(bf16), head0_k), vbd)
        local[j] = (p, vbd, qi.astype(bf16), kv, c_last)
    return order, vl, local


def _gla_finish(o_ref, st_ref, order, vl, local):
    row_id = lax.broadcasted_iota(i32, (PAIR_DK, PAIR_DK), 0)
    c_rows = jnp.zeros((PAIR_DK, PAIR_DK), f32)
    for j in order:
        c_rows = jnp.where(row_id == j, local[j][4], c_rows)
    decay_cols = jnp.exp(jnp.transpose(c_rows))
    st = st_ref[...]
    st_before = {}
    for j in order:
        st_before[j] = st.astype(bf16)
        st = decay_cols[:, j:j + 1] * st + local[j][3]
    st_ref[...] = st
    for j in order:
        p, vbd, qi, _, _ = local[j]
        o = _dot(p, vbd) + _dot(qi, st_before[j])
        o_ref[j * GLA_CHUNK:(j + 1) * GLA_CHUNK, vl] = o.astype(o_ref.dtype)


def _gla_kernel(qf, kf, vf, laf, qb, kb, vb, lab, of, ob, st_ref, tri_ref):
    tl = qf.shape[0]

    @pl.when(pl.program_id(1) == 0)
    def _():
        st_ref[...] = jnp.zeros_like(st_ref)
        r = lax.broadcasted_iota(i32, tri_ref.shape, 0)
        c = lax.broadcasted_iota(i32, tri_ref.shape, 1)
        tri_ref[...] = jnp.where((r // GLA_CHUNK == c // GLA_CHUNK) & (c <= r), 1.0, 0.0).astype(bf16)

    tri = tri_ref[...]
    cum_rows = tri_ref.shape[0]

    def chunk_cumsum(x):
        blocks = [_dot(tri, x[r0:r0 + cum_rows]) for r0 in range(0, tl, cum_rows)]
        return blocks[0] if len(blocks) == 1 else jnp.concatenate(blocks, axis=0)

    head0_k = lax.broadcasted_iota(i32, (GLA_CHUNK, PAIR_DK), 1) < GLA_DK
    head0_v = lax.broadcasted_iota(i32, (GLA_CHUNK, PAIR_DV), 1) < GLA_DV
    t_idx = lax.broadcasted_iota(i32, (GLA_CHUNK, 2 * GLA_CHUNK), 0)
    s_idx = lax.broadcasted_iota(i32, (GLA_CHUNK, 2 * GLA_CHUNK), 1) % GLA_CHUNK
    units = []
    for d, (q_ref, k_ref, v_ref, la_ref, o_ref) in enumerate(((qf, kf, vf, laf, of), (qb, kb, vb, lab, ob))):
        reverse = d == 1
        la = la_ref[...]
        la_hi, la_lo = _split_bf16(la)
        cs = chunk_cumsum(la_hi) + chunk_cumsum(la_lo)
        keep = (s_idx > t_idx) if reverse else (s_idx <= t_idx)
        for pair in range(GLA_HEADS // 2):
            units.append((o_ref, st_ref.at[d, pair]) + _gla_unit(q_ref, k_ref, v_ref, cs, la, pair, reverse,
                                                                  (head0_k, head0_v, keep)))
    for unit in units:
        _gla_finish(*unit)


def _gla(q, k, v, la, n_seq, seq_len):
    n = q.shape[0]
    tl = _tile(seq_len, GLA_TILE)
    assert tl % GLA_CUM_ROWS == 0 and GLA_CUM_ROWS % GLA_CHUNK == 0
    nblk = seq_len // tl
    fwd = lambda b, i: (b * nblk + i, 0)
    bwd = lambda b, i: (b * nblk + nblk - 1 - i, 0)
    bwd_la = lambda b, i: (b * nblk + nblk - 1 - i, 1)
    kq = lambda im: pl.BlockSpec((tl, GLA_KEY), im)
    vv = lambda im: pl.BlockSpec((tl, GLA_VAL), im)
    return pl.pallas_call(
        _gla_kernel,
        grid=(n_seq, nblk),
        in_specs=[kq(fwd), kq(fwd), vv(fwd), kq(fwd), kq(bwd), kq(bwd), vv(bwd), kq(bwd_la)],
        out_specs=[vv(fwd), vv(bwd)],
        out_shape=[jax.ShapeDtypeStruct((n, GLA_VAL), bf16), jax.ShapeDtypeStruct((n, GLA_VAL), bf16)],
        scratch_shapes=[pltpu.VMEM((2, GLA_HEADS // 2, PAIR_DK, PAIR_DV), f32), pltpu.VMEM((GLA_CUM_ROWS, GLA_CUM_ROWS), bf16)],
        compiler_params=_cparams(("parallel", "arbitrary")),
        name="gla",
    )(q, k, v, la, q, k, v, la)


def _conv_kernel(prev_ref, cur_ref, next_ref, wdw_ref, bdw_ref, gln_ref, bln_ref, wpw_ref,
                 o_ref, ext_ref, y_ref, *, tiles_per_seq):
    tl = cur_ref.shape[0]
    pos = pl.program_id(0) % tiles_per_seq
    prev = jnp.where(pos == 0, 0.0, prev_ref[...].astype(f32))
    nxt = jnp.where(pos == tiles_per_seq - 1, 0.0, next_ref[...].astype(f32))
    cur = cur_ref[...].astype(f32)
    n_slabs = CONV_CH // LANES
    for c in range(n_slabs):
        lanes = slice(c * LANES, (c + 1) * LANES)
        ext_ref[c, 0:CONV_HALO, :] = prev[:, lanes]
        ext_ref[c, CONV_HALO:CONV_HALO + tl, :] = cur[:, lanes]
        ext_ref[c, CONV_HALO + tl:, :] = nxt[:, lanes]
    off = CONV_HALO - CONV_K // 2

    def body(rb, carry):
        base = pl.multiple_of(rb * CONV_ROWS, CONV_ROWS)
        for c in range(n_slabs):
            lanes = slice(c * LANES, (c + 1) * LANES)
            a = jnp.zeros((CONV_ROWS, LANES), f32) + bdw_ref[:, lanes]
            for j in range(CONV_K):
                taps = ext_ref.at[c, pl.ds(off + j, tl)]
                a = a + taps[pl.ds(base, CONV_ROWS), :] * wdw_ref[j:j + 1, lanes]
            y_ref[pl.ds(base, CONV_ROWS), lanes] = a
        return carry

    lax.fori_loop(0, tl // CONV_ROWS, body, 0)
    acc = y_ref[...]
    mu = jnp.mean(acc, axis=-1, keepdims=True)
    xc = acc - mu
    yn = xc * lax.rsqrt(jnp.mean(xc * xc, axis=-1, keepdims=True) + EPS) * gln_ref[...] + bln_ref[...]
    o_ref[...] = _dot((yn * jax.nn.sigmoid(yn)).astype(bf16), wpw_ref[...]).astype(bf16)


def _conv(glu, wdw, bdw, gln, bln, wpw, seq_len, tl):
    n = glu.shape[0]
    hb = tl // CONV_HALO
    n_halo = n // CONV_HALO
    full = lambda a: pl.BlockSpec(a.shape, lambda i: (0, 0))
    return pl.pallas_call(
        functools.partial(_conv_kernel, tiles_per_seq=seq_len // tl),
        grid=(n // tl,),
        in_specs=[pl.BlockSpec((CONV_HALO, CONV_CH), lambda i: (jnp.maximum(i * hb - 1, 0), 0)),
                  pl.BlockSpec((tl, CONV_CH), lambda i: (i, 0)),
                  pl.BlockSpec((CONV_HALO, CONV_CH), lambda i: (jnp.minimum((i + 1) * hb, n_halo - 1), 0)),
                  full(wdw), full(bdw), full(gln), full(bln), full(wpw)],
        out_specs=pl.BlockSpec((tl, CONV_CH), lambda i: (i, 0)),
        out_shape=jax.ShapeDtypeStruct((n, CONV_CH), bf16),
        scratch_shapes=[pltpu.VMEM((CONV_CH // LANES, tl + 2 * CONV_HALO, LANES), f32),
                        pltpu.VMEM((tl, CONV_CH), f32)],
        compiler_params=_cparams(("parallel",)),
        name="conv",
    )(glu, glu, glu, wdw, bdw, gln, bln, wpw)


def _mix_kernel(x_ref, of_ref, ob_ref, og_ref, oc_ref, gh_ref, woa_ref, wob_ref, gf_ref, wr_ref, br_ref,
                h_ref, up_ref, eid_ref, wts_ref):
    tm = x_ref.shape[0]
    o = of_ref[...].astype(f32) + ob_ref[...].astype(f32)
    o = jnp.concatenate([_rms(o[:, h * GLA_DV:(h + 1) * GLA_DV]) for h in range(GLA_HEADS)], axis=1)
    og = og_ref[...].astype(f32)
    o = o * gh_ref[...] * (og * jax.nn.sigmoid(og))
    h = x_ref[...] + _dot(o.astype(bf16), woa_ref[...]) + _dot(oc_ref[...], wob_ref[...])
    h_ref[...] = h
    u = _rms(h) * gf_ref[...]
    u_hi, u_lo = _split_bf16(u)
    up_ref[...] = u_hi

    w_hi, w_lo = _split_bf16(wr_ref[...])
    n_r = w_hi.shape[0]
    both = _dot_nt(jnp.concatenate([w_hi, w_lo], axis=0), u_hi)
    logits = both[:n_r] + both[n_r:] + _dot_nt(w_hi, u_lo) + br_ref[...]
    le = logits[0:N_EXPERTS]
    lg = logits[N_EXPERTS:N_EXPERTS + N_GROUPS]
    gmax = jnp.max(lg, axis=0, keepdims=True)
    gi = lax.broadcasted_iota(i32, lg.shape, 0)
    gsel = jnp.min(jnp.where(lg == gmax, gi, N_GROUPS), axis=0, keepdims=True)
    gate = 1.0 / jnp.sum(jnp.exp(lg - gmax), axis=0, keepdims=True)
    ri = lax.broadcasted_iota(i32, le.shape, 0)
    lm = jnp.where(ri // EXPERTS_PER_GROUP == gsel, le, -jnp.inf)
    m1 = jnp.max(lm, axis=0, keepdims=True)
    i1 = jnp.min(jnp.where(lm == m1, ri, N_EXPERTS), axis=0, keepdims=True)
    lm2 = jnp.where(ri == i1, -jnp.inf, lm)
    m2 = jnp.max(lm2, axis=0, keepdims=True)
    i2 = jnp.min(jnp.where(lm2 == m2, ri, N_EXPERTS), axis=0, keepdims=True)
    t = jnp.exp(m2 - m1)
    den = 1.0 / (1.0 + t)
    r8 = lax.broadcasted_iota(i32, (SUBLANES, tm), 0)
    eid_ref[...] = jnp.where(r8 == 0, i1, jnp.where(r8 == 1, i2, 0))
    wts_ref[...] = jnp.where(r8 == 0, gate * den, jnp.where(r8 == 1, gate * t * den, 0.0))


def _mix(x, o_f, o_b, og, oc, g_head, woa, wob, g_ffn, wr, br, tm):
    n = x.shape[0]
    row = lambda w: pl.BlockSpec((tm, w), lambda i: (i, 0))
    col = pl.BlockSpec((SUBLANES, tm), lambda i: (0, i))
    full = lambda a: pl.BlockSpec(a.shape, lambda i: (0, 0))
    return pl.pallas_call(
        _mix_kernel,
        grid=(n // tm,),
        in_specs=[row(D_MODEL), row(GLA_VAL), row(GLA_VAL), row(GLA_VAL), row(CONV_CH),
                  full(g_head), full(woa), full(wob), full(g_ffn), full(wr), full(br)],
        out_specs=[row(D_MODEL), row(D_MODEL), col, col],
        out_shape=[jax.ShapeDtypeStruct((n, D_MODEL), f32), jax.ShapeDtypeStruct((n, D_MODEL), bf16),
                   jax.ShapeDtypeStruct((SUBLANES, n), i32), jax.ShapeDtypeStruct((SUBLANES, n), f32)],
        compiler_params=_cparams(("parallel",)),
        name="mix",
    )(x, o_f, o_b, og, oc, g_head, woa, wob, g_ffn, wr, br)


def _rank_kernel(eid_ref, pos_ref, tbl_ref, blk_ref, tail_ref, cnt_ref, base_ref, end_ref, *, blk, nbp, tm):
    ps = pl.program_id(0)
    i = pl.program_id(1)
    rr = lax.broadcasted_iota(i32, (N_EXPERTS, LANES), 0)

    def cumsum_experts(x):
        for s in (1, 2, 4, 8, 16):
            x = x + jnp.where(rr >= s, pltpu.roll(x, s, axis=0), 0.0)
        return x

    @pl.when((ps == 0) & (i == 0))
    def _():
        cnt_ref[...] = jnp.zeros_like(cnt_ref)

    @pl.when(ps == 0)
    def _():
        pos_ref[...] = jnp.zeros_like(pos_ref)
        tbl_ref[...] = jnp.zeros_like(tbl_ref)

    @pl.when((ps == 1) & (i == 0))
    def _():
        cnt = cnt_ref[...]
        pc = jnp.floor((cnt + (blk - 1)) * (1.0 / blk)) * blk
        inc = cumsum_experts(pc)
        base_ref[...] = inc - pc
        end_ref[...] = inc
        pend = jnp.concatenate([inc] * (nbp // LANES), axis=1)
        jl = lax.broadcasted_iota(i32, (N_EXPERTS, nbp), 1).astype(f32) * blk
        be = jnp.minimum(jnp.sum(jnp.where(pend <= jl, 1.0, 0.0), axis=0, keepdims=True), N_EXPERTS - 1.0)
        nused = jnp.concatenate([inc[N_EXPERTS - 1:N_EXPERTS, :]] * (nbp // LANES), axis=1) * (1.0 / blk)
        r8 = lax.broadcasted_iota(i32, (SUBLANES, nbp), 0)
        blk_ref[...] = jnp.where(r8 == 0, be, jnp.where(r8 == 1, nused, 0.0)).astype(i32)

    ri = lax.broadcasted_iota(i32, (N_EXPERTS, tm), 0)
    for t in range(eid_ref.shape[1] // tm):
        lanes = slice(t * tm, (t + 1) * tm)
        oh1 = ri == eid_ref[0:1, lanes]
        oh2 = ri == eid_ref[1:2, lanes]
        ohf = jnp.where(oh1 | oh2, 1.0, 0.0)
        tile_cnt = jnp.sum(ohf, axis=1, keepdims=True)
        run_len = (jnp.floor((tile_cnt + (RUN_ALIGN - 1)) * (1.0 / RUN_ALIGN)) * RUN_ALIGN
                   + jnp.zeros((N_EXPERTS, LANES), f32))

        @pl.when(ps == 0)
        def _():
            cnt_ref[...] += run_len

        @pl.when(ps == 1)
        def _():
            a = lax.broadcasted_iota(i32, (tm, tm), 0)
            b = lax.broadcasted_iota(i32, (tm, tm), 1)
            upper = jnp.where(a < b, 1.0, 0.0).astype(bf16)
            before = _dot(ohf.astype(bf16), upper)
            run_start = cumsum_experts(run_len) - run_len
            row = run_start[:, 0:1] + before
            p1 = jnp.sum(jnp.where(oh1, row, 0.0), axis=0, keepdims=True)
            p2 = jnp.sum(jnp.where(oh2, row, 0.0), axis=0, keepdims=True)
            r8 = lax.broadcasted_iota(i32, (SUBLANES, tm), 0)
            pos_ref[0, :, lanes] = jnp.where(r8 == 0, p1, jnp.where(r8 == 1, p2, 0.0)).astype(i32)
            diag = lax.broadcasted_iota(i32, (N_EXPERTS, LANES), 1) == rr
            on_lanes = lambda x: jnp.sum(jnp.where(diag, x, 0.0), axis=0, keepdims=True)
            big = jnp.floor(run_len * (1.0 / BIG_CHUNK))
            small = (run_len - big * BIG_CHUNK) * (1.0 / RUN_ALIGN)
            rows = (on_lanes(run_start), on_lanes(base_ref[...]), on_lanes(big), on_lanes(small),
                    jnp.sum(run_len, axis=0, keepdims=True))
            t8 = lax.broadcasted_iota(i32, (SUBLANES, LANES), 0)
            tbl = jnp.zeros((SUBLANES, LANES), f32)
            for k, x in enumerate(rows):
                tbl = jnp.where(t8 == k, x, tbl)
            tbl_ref[0, t] = tbl.astype(i32)
            base_ref[...] += run_len

    @pl.when((ps == 1) & (i == pl.num_programs(1) - 1))
    def _():
        diag = lax.broadcasted_iota(i32, (N_EXPERTS, LANES), 1) == rr
        on_lanes = lambda x: jnp.sum(jnp.where(diag, x, 0.0), axis=0, keepdims=True)
        first = base_ref[...]
        count = end_ref[...] - first
        rows = (on_lanes(first), on_lanes(count), jnp.sum(count, axis=0, keepdims=True),
                end_ref[N_EXPERTS - 1:N_EXPERTS, :] * (1.0 / blk))
        t8 = lax.broadcasted_iota(i32, (SUBLANES, LANES), 0)
        tail = jnp.zeros((SUBLANES, LANES), f32)
        for k, x in enumerate(rows):
            tail = jnp.where(t8 == k, x, tail)
        tail_ref[...] = tail.astype(i32)


def _rank(eid, blk, nbp, tm):
    n = eid.shape[1]
    group = _tile(n // tm, RANK_GROUP)
    return pl.pallas_call(
        functools.partial(_rank_kernel, blk=blk, nbp=nbp, tm=tm),
        grid=(2, n // (tm * group)),
        in_specs=[pl.BlockSpec((SUBLANES, tm * group), lambda p, i: (0, i))],
        out_specs=[pl.BlockSpec((1, SUBLANES, tm * group), lambda p, i: (p, 0, i)),
                   pl.BlockSpec((1, group, SUBLANES, LANES), lambda p, i: (p, i, 0, 0)),
                   pl.BlockSpec((SUBLANES, nbp), lambda p, i: (0, 0)),
                   pl.BlockSpec((SUBLANES, LANES), lambda p, i: (0, 0))],
        out_shape=[jax.ShapeDtypeStruct((2, SUBLANES, n), i32),
                   jax.ShapeDtypeStruct((2, n // tm, SUBLANES, LANES), i32),
                   jax.ShapeDtypeStruct((SUBLANES, nbp), i32),
                   jax.ShapeDtypeStruct((SUBLANES, LANES), i32)],
        scratch_shapes=[pltpu.VMEM((N_EXPERTS, LANES), f32)] * 3,
        compiler_params=_cparams(("arbitrary", "arbitrary")),
        name="rank",
    )(eid)


def _start_run_copies(tbl_ref, t, copy):
    base = t * SUBLANES

    def per_expert(e, carry):
        sorted0 = tbl_ref[base, e]
        slot0 = tbl_ref[base + 1, e]
        n_big = tbl_ref[base + 2, e]

        def pieces(rows, first):
            def body(c, carry2):
                off = first + c * rows
                copy(pl.multiple_of(sorted0 + off, RUN_ALIGN), pl.multiple_of(slot0 + off, RUN_ALIGN), rows).start()
                return carry2
            return body

        lax.fori_loop(0, n_big, pieces(BIG_CHUNK, 0), 0)
        lax.fori_loop(0, tbl_ref[base + 3, e], pieces(RUN_ALIGN, n_big * BIG_CHUNK), 0)
        return carry

    lax.fori_loop(0, N_EXPERTS, per_expert, 0)


def _wait_rows(total_rows, copy):
    def waits(rows):
        def body(c, carry):
            copy(0, 0, rows).wait()
            return carry
        return body

    n_wide = total_rows // WAIT_ROWS
    lax.fori_loop(0, n_wide, waits(WAIT_ROWS), 0)
    lax.fori_loop(0, (total_rows - n_wide * WAIT_ROWS) // RUN_ALIGN, waits(RUN_ALIGN), 0)


def _dispatch_kernel(tbl_ref, tail_ref, pos_ref, u_ref, xs_ref, sorted_ref, zero_ref, sems, pending_ref):
    i = pl.program_id(0)
    n_sub = sorted_ref.shape[0]
    tm = u_ref.shape[0] // n_sub

    def copy_from(b):
        def copy(sorted_row, slot_row, rows):
            return pltpu.make_async_copy(sorted_ref.at[b, pl.ds(sorted_row, rows)],
                                         xs_ref.at[pl.ds(slot_row, rows)], sems.at[b])
        return copy

    @pl.when(i == 0)
    def _():
        for b in range(n_sub):
            pending_ref[b] = 0

    ri = lax.broadcasted_iota(i32, (SORT_ROWS, tm), 0).astype(f32).astype(bf16)
    one = jnp.ones((SORT_ROWS, tm), bf16)
    for t in range(n_sub):
        _wait_rows(pending_ref[t], copy_from(t))
        p1 = pos_ref[0:1, t * tm:(t + 1) * tm]
        p2 = pos_ref[1:2, t * tm:(t + 1) * tm]
        u = u_ref[t * tm:(t + 1) * tm, :]
        for r0 in range(0, sorted_ref.shape[1], SORT_ROWS):
            q1 = (p1 - r0).astype(f32).astype(bf16)
            q2 = (p2 - r0).astype(f32).astype(bf16)
            sel = jnp.where((ri == q1) | (ri == q2), one, jnp.zeros_like(one))
            sorted_ref[t, r0:r0 + SORT_ROWS, :] = _pack_exact_rows(_dot(sel, u))
        _start_run_copies(tbl_ref, t, copy_from(t))
        pending_ref[t] = tbl_ref[t * SUBLANES + 4, 0]

    @pl.when(i == pl.num_programs(0) - 1)
    def _():
        for b in range(n_sub):
            _wait_rows(pending_ref[b], copy_from(b))
        blk = zero_ref.shape[0]
        n_blocks = xs_ref.shape[0] // blk
        zero_ref[...] = jnp.zeros_like(zero_ref)

        def zero_copy(unused_row, slot_row, rows):
            return pltpu.make_async_copy(zero_ref.at[pl.ds(0, rows)], xs_ref.at[pl.ds(slot_row, rows)], sems.at[n_sub])

        def per_expert(e, carry):
            first = tail_ref[0, e]
            n_big = tail_ref[1, e] // BIG_CHUNK

            def pieces(rows, start):
                def body(c, carry2):
                    zero_copy(0, pl.multiple_of(first + start + c * rows, RUN_ALIGN), rows).start()
                    return carry2
                return body

            lax.fori_loop(0, n_big, pieces(BIG_CHUNK, 0), 0)
            lax.fori_loop(0, (tail_ref[1, e] - n_big * BIG_CHUNK) // RUN_ALIGN, pieces(RUN_ALIGN, n_big * BIG_CHUNK), 0)
            return carry

        lax.fori_loop(0, N_EXPERTS, per_expert, 0)
        n_used = tail_ref[3, 0]

        def per_block(j, carry):
            zero_copy(0, pl.multiple_of(j * blk, blk), blk).start()
            return carry

        lax.fori_loop(n_used, n_blocks, per_block, 0)
        _wait_rows(tail_ref[2, 0] + (n_blocks - n_used) * blk, zero_copy)


def _dispatch(tbl, tail, pos, up, n_slots, tm, rmax):
    n = up.shape[0]
    group = _tile(n // tm, DISPATCH_GROUP)
    return pl.pallas_call(
        _dispatch_kernel,
        grid=(n // (tm * group),),
        in_specs=[pl.BlockSpec((group * SUBLANES, LANES), lambda i: (i, 0), memory_space=pltpu.SMEM),
                  pl.BlockSpec((SUBLANES, LANES), lambda i: (0, 0), memory_space=pltpu.SMEM),
                  pl.BlockSpec((SUBLANES, group * tm), lambda i: (0, i)),
                  pl.BlockSpec((group * tm, D_MODEL), lambda i: (i, 0))],
        out_specs=pl.BlockSpec(memory_space=pl.ANY),
        out_shape=jax.ShapeDtypeStruct((n_slots, HALF), u32),
        scratch_shapes=[pltpu.VMEM((group, rmax, HALF), u32), pltpu.VMEM((MOE_BLOCK, HALF), u32),
                        pltpu.SemaphoreType.DMA((group + 1,)), pltpu.SMEM((group,), i32)],
        compiler_params=_cparams(("arbitrary",)),
        name="dispatch",
    )(tbl, tail, pos, up)


def _expert_kernel(be_ref, nu_ref, xs_ref, wg_hbm, wu_hbm, wd_hbm, y_ref,
                   wg_f32, wu_f32, wd_f32, wg_bf, wu_bf, wd_bf, sems, slot_ref):
    j = pl.program_id(0)
    n_steps = pl.num_programs(0)
    expert = be_ref[j]

    def fetch(e, s):
        return [pltpu.make_async_copy(hbm.at[e], stage.at[s], sems.at[s, k])
                for k, (hbm, stage) in enumerate(((wg_hbm, wg_f32), (wu_hbm, wu_f32), (wd_hbm, wd_f32)))]

    @pl.when(j == 0)
    def _():
        slot_ref[0] = 0
        for c in fetch(expert, 0):
            c.start()

    @pl.when((j == 0) | (expert != be_ref[jnp.maximum(j - 1, 0)]))
    def _():
        s = slot_ref[0]
        for c in fetch(expert, s):
            c.wait()
        wg_bf[...] = wg_f32[s].astype(bf16)
        wu_bf[...] = wu_f32[s].astype(bf16)
        wd_bf[...] = wd_f32[s].astype(bf16)
        nxt = lax.while_loop(lambda k: (k < n_steps) & (be_ref[jnp.minimum(k, n_steps - 1)] == expert),
                             lambda k: k + 1, j + 1)

        @pl.when(nxt < n_steps)
        def _():
            for c in fetch(be_ref[jnp.minimum(nxt, n_steps - 1)], 1 - s):
                c.start()

        slot_ref[0] = 1 - s

    @pl.when(j < nu_ref[0])
    def _():
        xa, xb = _unpack_rows(xs_ref[...])
        x = jnp.concatenate([xa.astype(bf16), xb.astype(bf16)], axis=1)
        g = _dot(x, wg_bf[...])
        u = _dot(x, wu_bf[...])
        hb = (g * jax.nn.sigmoid(g) * u).astype(bf16)
        y_ref[...] = _pack_rows(_dot(hb, wd_bf[...]))

    @pl.when(j >= nu_ref[0])
    def _():
        y_ref[...] = jnp.zeros_like(y_ref)


def _experts(block_e, nused, xs, wg, wu, wd, blk):
    nb = xs.shape[0] // blk
    used = lambda j, be, nu: (jnp.minimum(j, nu[0] - 1), 0)
    hbm = pl.BlockSpec(memory_space=pl.ANY)
    return pl.pallas_call(
        _expert_kernel,
        grid_spec=pltpu.PrefetchScalarGridSpec(
            num_scalar_prefetch=2,
            grid=(nb,),
            in_specs=[pl.BlockSpec((blk, HALF), used), hbm, hbm, hbm],
            out_specs=pl.BlockSpec((blk, HALF), lambda j, be, nu: (j, 0)),
            scratch_shapes=[pltpu.VMEM((2, D_MODEL, D_EXPERT), f32), pltpu.VMEM((2, D_MODEL, D_EXPERT), f32),
                            pltpu.VMEM((2, D_EXPERT, D_MODEL), f32),
                            pltpu.VMEM((D_MODEL, D_EXPERT), bf16), pltpu.VMEM((D_MODEL, D_EXPERT), bf16),
                            pltpu.VMEM((D_EXPERT, D_MODEL), bf16),
                            pltpu.SemaphoreType.DMA((2, 3)), pltpu.SMEM((1,), i32)],
        ),
        out_shape=jax.ShapeDtypeStruct(xs.shape, u32),
        compiler_params=_cparams(("arbitrary",)),
        name="experts",
    )(block_e, nused, xs, wg, wu, wd)


def _combine_kernel(tbl_ref, tbl_next_ref, pos_ref, wts_ref, h_ref, gfin_ref, y_ref, o_ref, ys_ref, sems):
    i = pl.program_id(0)
    n_sub = ys_ref.shape[0] // 2
    tm = h_ref.shape[0] // n_sub
    cur = (i % 2) * n_sub
    nxt = n_sub - cur

    def copy_into(b):
        def copy(sorted_row, slot_row, rows):
            return pltpu.make_async_copy(y_ref.at[pl.ds(slot_row, rows)],
                                         ys_ref.at[b, pl.ds(sorted_row, rows)], sems.at[b])
        return copy

    @pl.when(i == 0)
    def _():
        ys_ref[...] = jnp.zeros_like(ys_ref)
        for t in range(n_sub):
            _start_run_copies(tbl_ref, t, copy_into(t))

    @pl.when(i + 1 < pl.num_programs(0))
    def _():
        for t in range(n_sub):
            _start_run_copies(tbl_next_ref, t, copy_into(nxt + t))

    li = lax.broadcasted_iota(i32, (tm, SORT_ROWS), 1).astype(f32).astype(bf16)
    none = jnp.zeros((tm, SORT_ROWS), bf16)
    pad = jnp.zeros((LANES - SUBLANES, tm), f32)
    for t in range(n_sub):
        rows = slice(t * tm, (t + 1) * tm)
        pos_cols = jnp.transpose(jnp.concatenate([pos_ref[:, rows].astype(f32), pad], axis=0))
        w_cols = jnp.transpose(jnp.concatenate([wts_ref[:, rows], pad], axis=0))
        p1, p2 = pos_cols[:, 0:1], pos_cols[:, 1:2]
        w1b = w_cols[:, 0:1].astype(bf16) + none
        w2b = w_cols[:, 1:2].astype(bf16) + none
        sel = jnp.concatenate(
            [jnp.where(li == (p1 - r0).astype(bf16), w1b, none) + jnp.where(li == (p2 - r0).astype(bf16), w2b, none)
             for r0 in range(0, ys_ref.shape[1], SORT_ROWS)], axis=1)
        _wait_rows(tbl_ref[t * SUBLANES + 4, 0], copy_into(cur + t))
        ya, yb = _unpack_rows(ys_ref[cur + t])
        ha = h_ref[rows, :HALF] + _dot(sel, ya.astype(bf16))
        hb = h_ref[rows, HALF:] + _dot(sel, yb.astype(bf16))
        ms = (jnp.sum(ha * ha, axis=-1, keepdims=True) + jnp.sum(hb * hb, axis=-1, keepdims=True)) * (1.0 / D_MODEL)
        inv = lax.rsqrt(ms + EPS)
        o_ref[rows, :HALF] = ha * inv * gfin_ref[:, :HALF]
        o_ref[rows, HALF:] = hb * inv * gfin_ref[:, HALF:]


def _combine(tbl, pos, wts, h, g_final, ybuf, tm, rmax):
    n = h.shape[0]
    group = _tile(n // tm, DISPATCH_GROUP)
    n_steps = n // (tm * group)
    col = pl.BlockSpec((SUBLANES, group * tm), lambda i: (0, i))
    return pl.pallas_call(
        _combine_kernel,
        grid=(n_steps,),
        in_specs=[pl.BlockSpec((group * SUBLANES, LANES), lambda i: (i, 0), memory_space=pltpu.SMEM),
                  pl.BlockSpec((group * SUBLANES, LANES), lambda i: (jnp.minimum(i + 1, n_steps - 1), 0),
                               memory_space=pltpu.SMEM),
                  col, col,
                  pl.BlockSpec((group * tm, D_MODEL), lambda i: (i, 0)),
                  pl.BlockSpec(g_final.shape, lambda i: (0, 0)),
                  pl.BlockSpec(memory_space=pl.ANY)],
        out_specs=pl.BlockSpec((group * tm, D_MODEL), lambda i: (i, 0)),
        out_shape=jax.ShapeDtypeStruct((n, D_MODEL), f32),
        scratch_shapes=[pltpu.VMEM((2 * group, rmax, HALF), u32), pltpu.SemaphoreType.DMA((2 * group,))],
        compiler_params=_cparams(("arbitrary",)),
        name="combine",
    )(tbl, tbl, pos, wts, h, g_final, ybuf)


def _tile(n, pref):
    t = pref
    while n % t:
        t //= 2
    return t


def _prep_weights(g_mix, w_in, w_af2, b_af, w_ab2, b_ab, g_head, w_dw, b_dw, g_ln, b_ln, w_pw2, w_out,
                  g_ffn, w_rg, b_rg, w_re, b_re, w_e_gate, w_e_up, w_e_down, g_final):
    n_qkvo = 2 * GLA_KEY + 2 * GLA_VAL
    n_a = 2 * GATE_RANK
    w_in = w_in[0]
    zg = jnp.zeros((GATE_RANK, GLA_KEY), f32)
    wgate = jnp.concatenate([jnp.concatenate([w_af2[0], zg], axis=1), jnp.concatenate([zg, w_ab2[0]], axis=1)], axis=0)
    wdw = jnp.concatenate([w_dw[0], jnp.zeros((1, CONV_CH), f32)], axis=0)
    n_pad = 2 * SUBLANES - N_GROUPS
    wr = jnp.concatenate([w_re[0].T, w_rg[0].T, jnp.zeros((n_pad, D_MODEL), f32)], axis=0)
    br = jnp.concatenate([b_re[0], b_rg[0], jnp.zeros((n_pad,), f32)])[:, None]
    return dict(
        g_mix=g_mix[0][None, :],
        wqkvo=w_in[:, :n_qkvo].astype(bf16),
        wa=w_in[:, n_qkvo:n_qkvo + n_a].astype(bf16),
        wc=w_in[:, n_qkvo + n_a:].astype(bf16),
        wgate=wgate.astype(bf16),
        bgate=jnp.concatenate([b_af[0], b_ab[0]])[None, :],
        g_head=g_head[0][None, :],
        wdw=wdw, bdw=b_dw[0][None, :], gln=g_ln[0][None, :], bln=b_ln[0][None, :],
        wpw=w_pw2[0].astype(bf16),
        woa=w_out[0, :GLA_VAL].astype(bf16), wob=w_out[0, GLA_VAL:].astype(bf16),
        g_ffn=g_ffn[0][None, :], wr=wr, br=br,
        wg=w_e_gate[0], wu=w_e_up[0], wd=w_e_down[0],
        g_final=g_final[None, :],
    )


def _trunk(x3, w):
    n_seq, seq_len, _ = x3.shape
    n = n_seq * seq_len
    x = x3.reshape(n, D_MODEL)
    tm = _tile(n, MOE_TILE)
    q, k, v, og, la, glu = _inproj(x, w["g_mix"], w["wqkvo"], w["wa"], w["wc"], w["wgate"], w["bgate"],
                                   _tile(n, 1024))
    o_f, o_b = _gla(q, k, v, la, n_seq, seq_len)
    oc = _conv(glu, w["wdw"], w["bdw"], w["gln"], w["bln"], w["wpw"], seq_len, _tile(seq_len, CONV_TILE))
    h, up, eid, wts = _mix(x, o_f, o_b, og, oc, w["g_head"], w["woa"], w["wob"], w["g_ffn"], w["wr"], w["br"],
                            _tile(n, 1024))

    n_tiles = n // tm
    max_slots = n * TOP_K + n_tiles * N_EXPERTS * (RUN_ALIGN - 1)
    nb = -(-max_slots // MOE_BLOCK) + N_EXPERTS
    nbp = -(-nb // LANES) * LANES
    rmax = -(-(tm * TOP_K + N_EXPERTS * (RUN_ALIGN - 1)) // SORT_ROWS) * SORT_ROWS
    pos, tbl, blkinfo, tail = _rank(eid, MOE_BLOCK, nbp, tm)
    pos = pos[1]
    tbl = tbl[1].reshape(n_tiles * SUBLANES, LANES)
    xs = _dispatch(tbl, tail, pos, up, nb * MOE_BLOCK, tm, rmax)
    ybuf = _experts(blkinfo[0], blkinfo[1, :1], xs, w["wg"], w["wu"], w["wd"], MOE_BLOCK)
    y = _combine(tbl, pos, wts, h, w["g_final"], ybuf, tm, rmax)
    return y.reshape(x3.shape)


def kernel(x_prompt, x_sample, g_mix, w_in, w_af2, b_af, w_ab2, b_ab, g_head, w_dw, b_dw, g_ln, b_ln, w_pw2,
           w_out, g_ffn, w_rg, b_rg, w_re, b_re, w_e_gate, w_e_up, w_e_down, g_final):
    w = _prep_weights(g_mix, w_in, w_af2, b_af, w_ab2, b_ab, g_head, w_dw, b_dw, g_ln, b_ln, w_pw2, w_out,
                      g_ffn, w_rg, b_rg, w_re, b_re, w_e_gate, w_e_up, w_e_down, g_final)
    return (_trunk(x_prompt, w), _trunk(x_sample, w))
```

```python
import functools

import jax
import jax.numpy as jnp
from jax import lax
from jax.experimental import pallas as pl
from jax.experimental.pallas import tpu as pltpu

f32 = jnp.float32
bf16 = jnp.bfloat16
i32 = jnp.int32
u32 = jnp.uint32

D_MODEL = 1024
GLA_HEADS = 4
GLA_DK = 64
GLA_DV = 128
GLA_KEY = GLA_HEADS * GLA_DK
GLA_VAL = GLA_HEADS * GLA_DV
GATE_RANK = 16
GATE_NORMALIZER = 16
CONV_CH = 512
CONV_K = 31
N_GROUPS = 4
EXPERTS_PER_GROUP = 8
N_EXPERTS = N_GROUPS * EXPERTS_PER_GROUP
TOP_K = 2
D_EXPERT = 512
EPS = 1e-6

LANES = 128
SUBLANES = 8
GLA_CHUNK = 128
GLA_TILE = 1024
GLA_CUM_ROWS = 256
PAIR_DK = 2 * GLA_DK
PAIR_DV = 2 * GLA_DV
CONV_HALO = 16
CONV_ROWS = 32
CONV_TILE = 1024
HALF = D_MODEL // 2
MOE_TILE = 512
MOE_BLOCK = 512
RUN_ALIGN = SUBLANES
SORT_ROWS = 256
RANK_GROUP = 8
DISPATCH_GROUP = 2
COMBINE_GROUP = 1
BIG_CHUNK = 32
WAIT_ROWS = 256
VMEM_LIMIT = 56 * 1024 * 1024


def _cparams(sem, **kw):
    return pltpu.CompilerParams(dimension_semantics=sem, vmem_limit_bytes=VMEM_LIMIT, **kw)


def _dot(a, b):
    return jnp.dot(a, b, preferred_element_type=f32)


def _dot_nt(a, b):
    return lax.dot_general(a, b, (((1,), (1,)), ((), ())), preferred_element_type=f32)


def _dot_tn(a, b):
    return lax.dot_general(a, b, (((0,), (0,)), ((), ())), preferred_element_type=f32)


def _split_bf16(x):
    hi = x.astype(bf16)
    lo = (x - hi.astype(f32)).astype(bf16)
    return hi, lo


def _pack_rows(x):
    a = lax.bitcast_convert_type(x[:, :HALF].astype(bf16).astype(f32), u32)
    b = lax.bitcast_convert_type(x[:, HALF:].astype(bf16).astype(f32), u32)
    return a | (b >> 16)


def _pack_exact_rows(x):
    a = lax.bitcast_convert_type(x[:, :HALF], u32)
    b = lax.bitcast_convert_type(x[:, HALF:], u32)
    return a | (b >> 16)


def _unpack_rows(w):
    a = lax.bitcast_convert_type(w & jnp.uint32(0xFFFF0000), f32)
    b = lax.bitcast_convert_type(w << 16, f32)
    return a, b


def _rms(x):
    return x * lax.rsqrt(jnp.mean(x * x, axis=-1, keepdims=True) + EPS)


def _inproj_kernel(x_ref, g_ref, wqkvo_ref, wa_ref, wc_ref, wgate_ref, bgate_ref,
                   q_ref, k_ref, v_ref, og_ref, la_ref, glu_ref):
    u = (_rms(x_ref[...]) * g_ref[...]).astype(bf16)
    a = _dot(u, wa_ref[...]).astype(bf16)
    p = _dot(u, wqkvo_ref[...])
    q_ref[...] = (p[:, :GLA_KEY] * (GLA_DK ** -0.5)).astype(bf16)
    k_ref[...] = p[:, GLA_KEY:2 * GLA_KEY].astype(bf16)
    v_ref[...] = p[:, 2 * GLA_KEY:2 * GLA_KEY + GLA_VAL].astype(bf16)
    og_ref[...] = p[:, 2 * GLA_KEY + GLA_VAL:].astype(bf16)
    z = _dot(a, wgate_ref[...]) + bgate_ref[...]
    la_ref[...] = (jnp.minimum(z, 0.0) - jnp.log1p(jnp.exp(-jnp.abs(z)))) * (1.0 / GATE_NORMALIZER)
    c = _dot(u, wc_ref[...])
    glu_ref[...] = (c[:, :CONV_CH] * jax.nn.sigmoid(c[:, CONV_CH:])).astype(bf16)


def _inproj(x, g_mix, wqkvo, wa, wc, wgate, bgate, tm):
    n = x.shape[0]
    row = lambda w: pl.BlockSpec((tm, w), lambda i: (i, 0))
    full = lambda a: pl.BlockSpec(a.shape, lambda i: (0, 0))
    return pl.pallas_call(
        _inproj_kernel,
        grid=(n // tm,),
        in_specs=[row(D_MODEL), full(g_mix), full(wqkvo), full(wa), full(wc), full(wgate), full(bgate)],
        out_specs=[row(GLA_KEY), row(GLA_KEY), row(GLA_VAL), row(GLA_VAL), row(2 * GLA_KEY), row(CONV_CH)],
        out_shape=[jax.ShapeDtypeStruct((n, GLA_KEY), bf16), jax.ShapeDtypeStruct((n, GLA_KEY), bf16),
                   jax.ShapeDtypeStruct((n, GLA_VAL), bf16), jax.ShapeDtypeStruct((n, GLA_VAL), bf16),
                   jax.ShapeDtypeStruct((n, 2 * GLA_KEY), f32), jax.ShapeDtypeStruct((n, CONV_CH), bf16)],
        compiler_params=_cparams(("parallel",)),
        name="inproj",
    )(x, g_mix, wqkvo, wa, wc, wgate, bgate)


def _gla_unit(q_ref, k_ref, v_ref, cs, la, pair, reverse, masks):
    head0_k, head0_v, keep = masks
    tl = cs.shape[0]
    kl = slice(pair * PAIR_DK, (pair + 1) * PAIR_DK)
    vl = slice(pair * PAIR_DV, (pair + 1) * PAIR_DV)
    n_chunks = tl // GLA_CHUNK
    order = list(range(n_chunks - 1, -1, -1) if reverse else range(n_chunks))

    def stack_heads(x, head0):
        z = jnp.zeros_like(x)
        return jnp.concatenate([jnp.where(head0, x, z), jnp.where(head0, z, x)], axis=0)

    local = {}
    for j in order:
        rows = slice(j * GLA_CHUNK, (j + 1) * GLA_CHUNK)
        cj = cs[rows, kl]
        q = q_ref[rows, kl].astype(f32)
        k = k_ref[rows, kl].astype(f32)
        c_last = cj[GLA_CHUNK - 1:GLA_CHUNK, :]
        mid = 0.5 * c_last
        rel = (mid - (cj - la[rows, kl])) if reverse else (cj - mid)
        qd = q * jnp.exp(rel)
        kd = k * jnp.exp(-rel)
        edge = jnp.exp(mid)
        qi = qd * edge
        ku = kd * edge
        vbd = stack_heads(v_ref[rows, vl], head0_v)
        sc = _dot_nt(qd.astype(bf16), stack_heads(kd.astype(bf16), head0_k))
        p = jnp.where(keep, sc, 0.0).astype(bf16)
        kv = _dot_tn(stack_heads(ku.astype(bf16), head0_k), vbd)
        local[j] = (p, vbd, qi.astype(bf16), kv, c_last)
    return order, vl, local


def _gla_finish(o_ref, st_ref, order, vl, local):
    row_id = lax.broadcasted_iota(i32, (PAIR_DK, PAIR_DK), 0)
    c_rows = jnp.zeros((PAIR_DK, PAIR_DK), f32)
    for j in order:
        c_rows = jnp.where(row_id == j, local[j][4], c_rows)
    decay_cols = jnp.exp(jnp.transpose(c_rows))
    st = st_ref[...]
    st_before = {}
    for j in order:
        st_before[j] = st.astype(bf16)
        st = decay_cols[:, j:j + 1] * st + local[j][3]
    st_ref[...] = st
    for j in order:
        p, vbd, qi, _, _ = local[j]
        o = _dot(p, vbd) + _dot(qi, st_before[j])
        o_ref[j * GLA_CHUNK:(j + 1) * GLA_CHUNK, vl] = o.astype(o_ref.dtype)


def _gla_kernel(qf, kf, vf, laf, qb, kb, vb, lab, of, ob, st_ref, tri_ref):
    tl = qf.shape[0]

    @pl.when(pl.program_id(1) == 0)
    def _():
        st_ref[...] = jnp.zeros_like(st_ref)
        r = lax.broadcasted_iota(i32, tri_ref.shape, 0)
        c = lax.broadcasted_iota(i32, tri_ref.shape, 1)
        tri_ref[...] = jnp.where((r // GLA_CHUNK == c // GLA_CHUNK) & (c <= r), 1.0, 0.0).astype(bf16)

    tri = tri_ref[...]
    cum_rows = tri_ref.shape[0]

    def chunk_cumsum(x):
        blocks = [_dot(tri, x[r0:r0 + cum_rows]) for r0 in range(0, tl, cum_rows)]
        return blocks[0] if len(blocks) == 1 else jnp.concatenate(blocks, axis=0)

    head0_k = lax.broadcasted_iota(i32, (GLA_CHUNK, PAIR_DK), 1) < GLA_DK
    head0_v = lax.broadcasted_iota(i32, (GLA_CHUNK, PAIR_DV), 1) < GLA_DV
    t_idx = lax.broadcasted_iota(i32, (GLA_CHUNK, 2 * GLA_CHUNK), 0)
    s_idx = lax.broadcasted_iota(i32, (GLA_CHUNK, 2 * GLA_CHUNK), 1) % GLA_CHUNK
    units = []
    for d, (q_ref, k_ref, v_ref, la_ref, o_ref) in enumerate(((qf, kf, vf, laf, of), (qb, kb, vb, lab, ob))):
        reverse = d == 1
        la = la_ref[...]
        la_hi, la_lo = _split_bf16(la)
        cs = chunk_cumsum(la_hi) + chunk_cumsum(la_lo)
        keep = (s_idx > t_idx) if reverse else (s_idx <= t_idx)
        for pair in range(GLA_HEADS // 2):
            units.append((o_ref, st_ref.at[d, pair]) + _gla_unit(q_ref, k_ref, v_ref, cs, la, pair, reverse,
                                                                  (head0_k, head0_v, keep)))
    for unit in units:
        _gla_finish(*unit)


def _gla(q, k, v, la, n_seq, seq_len):
    n = q.shape[0]
    tl = _tile(seq_len, GLA_TILE)
    assert tl % GLA_CUM_ROWS == 0 and GLA_CUM_ROWS % GLA_CHUNK == 0
    nblk = seq_len // tl
    fwd = lambda b, i: (b * nblk + i, 0)
    bwd = lambda b, i: (b * nblk + nblk - 1 - i, 0)
    bwd_la = lambda b, i: (b * nblk + nblk - 1 - i, 1)
    kq = lambda im: pl.BlockSpec((tl, GLA_KEY), im)
    vv = lambda im: pl.BlockSpec((tl, GLA_VAL), im)
    return pl.pallas_call(
        _gla_kernel,
        grid=(n_seq, nblk),
        in_specs=[kq(fwd), kq(fwd), vv(fwd), kq(fwd), kq(bwd), kq(bwd), vv(bwd), kq(bwd_la)],
        out_specs=[vv(fwd), vv(bwd)],
        out_shape=[jax.ShapeDtypeStruct((n, GLA_VAL), bf16), jax.ShapeDtypeStruct((n, GLA_VAL), bf16)],
        scratch_shapes=[pltpu.VMEM((2, GLA_HEADS // 2, PAIR_DK, PAIR_DV), f32), pltpu.VMEM((GLA_CUM_ROWS, GLA_CUM_ROWS), bf16)],
        compiler_params=_cparams(("parallel", "arbitrary")),
        name="gla",
    )(q, k, v, la, q, k, v, la)


def _conv_kernel(prev_ref, cur_ref, next_ref, wdw_ref, bdw_ref, gln_ref, bln_ref, wpw_ref,
                 o_ref, ext_ref, y_ref, *, tiles_per_seq):
    tl = cur_ref.shape[0]
    pos = pl.program_id(0) % tiles_per_seq
    prev = jnp.where(pos == 0, 0.0, prev_ref[...].astype(f32))
    nxt = jnp.where(pos == tiles_per_seq - 1, 0.0, next_ref[...].astype(f32))
    cur = cur_ref[...].astype(f32)
    n_slabs = CONV_CH // LANES
    for c in range(n_slabs):
        lanes = slice(c * LANES, (c + 1) * LANES)
        ext_ref[c, 0:CONV_HALO, :] = prev[:, lanes]
        ext_ref[c, CONV_HALO:CONV_HALO + tl, :] = cur[:, lanes]
        ext_ref[c, CONV_HALO + tl:, :] = nxt[:, lanes]
    off = CONV_HALO - CONV_K // 2

    def body(rb, carry):
        base = pl.multiple_of(rb * CONV_ROWS, CONV_ROWS)
        for c in range(n_slabs):
            lanes = slice(c * LANES, (c + 1) * LANES)
            a = jnp.zeros((CONV_ROWS, LANES), f32) + bdw_ref[:, lanes]
            for j in range(CONV_K):
                taps = ext_ref.at[c, pl.ds(off + j, tl)]
                a = a + taps[pl.ds(base, CONV_ROWS), :] * wdw_ref[j:j + 1, lanes]
            y_ref[pl.ds(base, CONV_ROWS), lanes] = a
        return carry

    lax.fori_loop(0, tl // CONV_ROWS, body, 0)
    acc = y_ref[...]
    mu = jnp.mean(acc, axis=-1, keepdims=True)
    xc = acc - mu
    yn = xc * lax.rsqrt(jnp.mean(xc * xc, axis=-1, keepdims=True) + EPS) * gln_ref[...] + bln_ref[...]
    o_ref[...] = _dot((yn * jax.nn.sigmoid(yn)).astype(bf16), wpw_ref[...]).astype(bf16)


def _conv(glu, wdw, bdw, gln, bln, wpw, seq_len, tl):
    n = glu.shape[0]
    hb = tl // CONV_HALO
    n_halo = n // CONV_HALO
    full = lambda a: pl.BlockSpec(a.shape, lambda i: (0, 0))
    return pl.pallas_call(
        functools.partial(_conv_kernel, tiles_per_seq=seq_len // tl),
        grid=(n // tl,),
        in_specs=[pl.BlockSpec((CONV_HALO, CONV_CH), lambda i: (jnp.maximum(i * hb - 1, 0), 0)),
                  pl.BlockSpec((tl, CONV_CH), lambda i: (i, 0)),
                  pl.BlockSpec((CONV_HALO, CONV_CH), lambda i: (jnp.minimum((i + 1) * hb, n_halo - 1), 0)),
                  full(wdw), full(bdw), full(gln), full(bln), full(wpw)],
        out_specs=pl.BlockSpec((tl, CONV_CH), lambda i: (i, 0)),
        out_shape=jax.ShapeDtypeStruct((n, CONV_CH), bf16),
        scratch_shapes=[pltpu.VMEM((CONV_CH // LANES, tl + 2 * CONV_HALO, LANES), f32),
                        pltpu.VMEM((tl, CONV_CH), f32)],
        compiler_params=_cparams(("parallel",)),
        name="conv",
    )(glu, glu, glu, wdw, bdw, gln, bln, wpw)


def _mix_kernel(x_ref, of_ref, ob_ref, og_ref, oc_ref, gh_ref, woa_ref, wob_ref, gf_ref, wr_ref, br_ref,
                h_ref, up_ref, eid_ref, wts_ref):
    tm = x_ref.shape[0]
    o = of_ref[...].astype(f32) + ob_ref[...].astype(f32)
    o = jnp.concatenate([_rms(o[:, h * GLA_DV:(h + 1) * GLA_DV]) for h in range(GLA_HEADS)], axis=1)
    og = og_ref[...].astype(f32)
    o = o * gh_ref[...] * (og * jax.nn.sigmoid(og))
    h = x_ref[...] + _dot(o.astype(bf16), woa_ref[...]) + _dot(oc_ref[...], wob_ref[...])
    h_ref[...] = h
    u = _rms(h) * gf_ref[...]
    u_hi, u_lo = _split_bf16(u)
    up_ref[...] = u_hi

    w_hi, w_lo = _split_bf16(wr_ref[...])
    n_r = w_hi.shape[0]
    both = _dot_nt(jnp.concatenate([w_hi, w_lo], axis=0), u_hi)
    logits = both[:n_r] + both[n_r:] + _dot_nt(w_hi, u_lo) + br_ref[...]
    le = logits[0:N_EXPERTS]
    lg = logits[N_EXPERTS:N_EXPERTS + N_GROUPS]
    gmax = jnp.max(lg, axis=0, keepdims=True)
    gi = lax.broadcasted_iota(i32, lg.shape, 0)
    gsel = jnp.min(jnp.where(lg == gmax, gi, N_GROUPS), axis=0, keepdims=True)
    gate = 1.0 / jnp.sum(jnp.exp(lg - gmax), axis=0, keepdims=True)
    ri = lax.broadcasted_iota(i32, le.shape, 0)
    lm = jnp.where(ri // EXPERTS_PER_GROUP == gsel, le, -jnp.inf)
    m1 = jnp.max(lm, axis=0, keepdims=True)
    i1 = jnp.min(jnp.where(lm == m1, ri, N_EXPERTS), axis=0, keepdims=True)
    lm2 = jnp.where(ri == i1, -jnp.inf, lm)
    m2 = jnp.max(lm2, axis=0, keepdims=True)
    i2 = jnp.min(jnp.where(lm2 == m2, ri, N_EXPERTS), axis=0, keepdims=True)
    t = jnp.exp(m2 - m1)
    den = 1.0 / (1.0 + t)
    r8 = lax.broadcasted_iota(i32, (SUBLANES, tm), 0)
    eid_ref[...] = jnp.where(r8 == 0, i1, jnp.where(r8 == 1, i2, 0))
    wts_ref[...] = jnp.where(r8 == 0, gate * den, jnp.where(r8 == 1, gate * t * den, 0.0))


def _mix(x, o_f, o_b, og, oc, g_head, woa, wob, g_ffn, wr, br, tm):
    n = x.shape[0]
    row = lambda w: pl.BlockSpec((tm, w), lambda i: (i, 0))
    col = pl.BlockSpec((SUBLANES, tm), lambda i: (0, i))
    full = lambda a: pl.BlockSpec(a.shape, lambda i: (0, 0))
    return pl.pallas_call(
        _mix_kernel,
        grid=(n // tm,),
        in_specs=[row(D_MODEL), row(GLA_VAL), row(GLA_VAL), row(GLA_VAL), row(CONV_CH),
                  full(g_head), full(woa), full(wob), full(g_ffn), full(wr), full(br)],
        out_specs=[row(D_MODEL), row(D_MODEL), col, col],
        out_shape=[jax.ShapeDtypeStruct((n, D_MODEL), f32), jax.ShapeDtypeStruct((n, D_MODEL), bf16),
                   jax.ShapeDtypeStruct((SUBLANES, n), i32), jax.ShapeDtypeStruct((SUBLANES, n), f32)],
        compiler_params=_cparams(("parallel",)),
        name="mix",
    )(x, o_f, o_b, og, oc, g_head, woa, wob, g_ffn, wr, br)


def _rank_kernel(eid_ref, pos_ref, tbl_ref, blk_ref, tail_ref, cnt_ref, base_ref, end_ref, *, blk, nbp, tm):
    ps = pl.program_id(0)
    i = pl.program_id(1)
    rr = lax.broadcasted_iota(i32, (N_EXPERTS, LANES), 0)

    def cumsum_experts(x):
        for s in (1, 2, 4, 8, 16):
            x = x + jnp.where(rr >= s, pltpu.roll(x, s, axis=0), 0.0)
        return x

    @pl.when((ps == 0) & (i == 0))
    def _():
        cnt_ref[...] = jnp.zeros_like(cnt_ref)

    @pl.when(ps == 0)
    def _():
        pos_ref[...] = jnp.zeros_like(pos_ref)
        tbl_ref[...] = jnp.zeros_like(tbl_ref)

    @pl.when((ps == 1) & (i == 0))
    def _():
        cnt = cnt_ref[...]
        pc = jnp.floor((cnt + (blk - 1)) * (1.0 / blk)) * blk
        inc = cumsum_experts(pc)
        base_ref[...] = inc - pc
        end_ref[...] = inc
        pend = jnp.concatenate([inc] * (nbp // LANES), axis=1)
        jl = lax.broadcasted_iota(i32, (N_EXPERTS, nbp), 1).astype(f32) * blk
        be = jnp.minimum(jnp.sum(jnp.where(pend <= jl, 1.0, 0.0), axis=0, keepdims=True), N_EXPERTS - 1.0)
        nused = jnp.concatenate([inc[N_EXPERTS - 1:N_EXPERTS, :]] * (nbp // LANES), axis=1) * (1.0 / blk)
        r8 = lax.broadcasted_iota(i32, (SUBLANES, nbp), 0)
        blk_ref[...] = jnp.where(r8 == 0, be, jnp.where(r8 == 1, nused, 0.0)).astype(i32)

    ri = lax.broadcasted_iota(i32, (N_EXPERTS, tm), 0)
    for t in range(eid_ref.shape[1] // tm):
        lanes = slice(t * tm, (t + 1) * tm)
        oh1 = ri == eid_ref[0:1, lanes]
        oh2 = ri == eid_ref[1:2, lanes]
        ohf = jnp.where(oh1 | oh2, 1.0, 0.0)
        tile_cnt = jnp.sum(ohf, axis=1, keepdims=True)
        run_len = (jnp.floor((tile_cnt + (RUN_ALIGN - 1)) * (1.0 / RUN_ALIGN)) * RUN_ALIGN
                   + jnp.zeros((N_EXPERTS, LANES), f32))

        @pl.when(ps == 0)
        def _():
            cnt_ref[...] += run_len

        @pl.when(ps == 1)
        def _():
            a = lax.broadcasted_iota(i32, (tm, tm), 0)
            b = lax.broadcasted_iota(i32, (tm, tm), 1)
            upper = jnp.where(a < b, 1.0, 0.0).astype(bf16)
            before = _dot(ohf.astype(bf16), upper)
            run_start = cumsum_experts(run_len) - run_len
            row = run_start[:, 0:1] + before
            p1 = jnp.sum(jnp.where(oh1, row, 0.0), axis=0, keepdims=True)
            p2 = jnp.sum(jnp.where(oh2, row, 0.0), axis=0, keepdims=True)
            r8 = lax.broadcasted_iota(i32, (SUBLANES, tm), 0)
            pos_ref[0, :, lanes] = jnp.where(r8 == 0, p1, jnp.where(r8 == 1, p2, 0.0)).astype(i32)
            diag = lax.broadcasted_iota(i32, (N_EXPERTS, LANES), 1) == rr
            on_lanes = lambda x: jnp.sum(jnp.where(diag, x, 0.0), axis=0, keepdims=True)
            big = jnp.floor(run_len * (1.0 / BIG_CHUNK))
            small = (run_len - big * BIG_CHUNK) * (1.0 / RUN_ALIGN)
            rows = (on_lanes(run_start), on_lanes(base_ref[...]), on_lanes(big), on_lanes(small),
                    jnp.sum(run_len, axis=0, keepdims=True))
            t8 = lax.broadcasted_iota(i32, (SUBLANES, LANES), 0)
            tbl = jnp.zeros((SUBLANES, LANES), f32)
            for k, x in enumerate(rows):
                tbl = jnp.where(t8 == k, x, tbl)
            tbl_ref[0, t] = tbl.astype(i32)
            base_ref[...] += run_len

    @pl.when((ps == 1) & (i == pl.num_programs(1) - 1))
    def _():
        diag = lax.broadcasted_iota(i32, (N_EXPERTS, LANES), 1) == rr
        on_lanes = lambda x: jnp.sum(jnp.where(diag, x, 0.0), axis=0, keepdims=True)
        first = base_ref[...]
        count = end_ref[...] - first
        rows = (on_lanes(first), on_lanes(count), jnp.sum(count, axis=0, keepdims=True),
                end_ref[N_EXPERTS - 1:N_EXPERTS, :] * (1.0 / blk))
        t8 = lax.broadcasted_iota(i32, (SUBLANES, LANES), 0)
        tail = jnp.zeros((SUBLANES, LANES), f32)
        for k, x in enumerate(rows):
            tail = jnp.where(t8 == k, x, tail)
        tail_ref[...] = tail.astype(i32)


def _rank(eid, blk, nbp, tm):
    n = eid.shape[1]
    group = _tile(n // tm, RANK_GROUP)
    return pl.pallas_call(
        functools.partial(_rank_kernel, blk=blk, nbp=nbp, tm=tm),
        grid=(2, n // (tm * group)),
        in_specs=[pl.BlockSpec((SUBLANES, tm * group), lambda p, i: (0, i))],
        out_specs=[pl.BlockSpec((1, SUBLANES, tm * group), lambda p, i: (p, 0, i)),
                   pl.BlockSpec((1, group, SUBLANES, LANES), lambda p, i: (p, i, 0, 0)),
                   pl.BlockSpec((SUBLANES, nbp), lambda p, i: (0, 0)),
                   pl.BlockSpec((SUBLANES, LANES), lambda p, i: (0, 0))],
        out_shape=[jax.ShapeDtypeStruct((2, SUBLANES, n), i32),
                   jax.ShapeDtypeStruct((2, n // tm, SUBLANES, LANES), i32),
                   jax.ShapeDtypeStruct((SUBLANES, nbp), i32),
                   jax.ShapeDtypeStruct((SUBLANES, LANES), i32)],
        scratch_shapes=[pltpu.VMEM((N_EXPERTS, LANES), f32)] * 3,
        compiler_params=_cparams(("arbitrary", "arbitrary")),
        name="rank",
    )(eid)


def _start_run_copies(tbl_ref, t, copy):
    base = t * SUBLANES

    def per_expert(e, carry):
        sorted0 = tbl_ref[base, e]
        slot0 = tbl_ref[base + 1, e]
        n_big = tbl_ref[base + 2, e]

        def pieces(rows, first):
            def body(c, carry2):
                off = first + c * rows
                copy(pl.multiple_of(sorted0 + off, RUN_ALIGN), pl.multiple_of(slot0 + off, RUN_ALIGN), rows).start()
                return carry2
            return body

        lax.fori_loop(0, n_big, pieces(BIG_CHUNK, 0), 0)
        lax.fori_loop(0, tbl_ref[base + 3, e], pieces(RUN_ALIGN, n_big * BIG_CHUNK), 0)
        return carry

    lax.fori_loop(0, N_EXPERTS, per_expert, 0)


def _wait_rows(total_rows, copy):
    def waits(rows):
        def body(c, carry):
            copy(0, 0, rows).wait()
            return carry
        return body

    n_wide = total_rows // WAIT_ROWS
    lax.fori_loop(0, n_wide, waits(WAIT_ROWS), 0)
    lax.fori_loop(0, (total_rows - n_wide * WAIT_ROWS) // RUN_ALIGN, waits(RUN_ALIGN), 0)


def _dispatch_kernel(tbl_ref, tail_ref, pos_ref, u_ref, xs_ref, sorted_ref, zero_ref, sems, pending_ref):
    i = pl.program_id(0)
    n_sub = sorted_ref.shape[0]
    tm = u_ref.shape[0] // n_sub

    def copy_from(b):
        def copy(sorted_row, slot_row, rows):
            return pltpu.make_async_copy(sorted_ref.at[b, pl.ds(sorted_row, rows)],
                                         xs_ref.at[pl.ds(slot_row, rows)], sems.at[b])
        return copy

    @pl.when(i == 0)
    def _():
        for b in range(n_sub):
            pending_ref[b] = 0

    ri = lax.broadcasted_iota(i32, (SORT_ROWS, tm), 0).astype(f32).astype(bf16)
    one = jnp.ones((SORT_ROWS, tm), bf16)
    for t in range(n_sub):
        _wait_rows(pending_ref[t], copy_from(t))
        p1 = pos_ref[0:1, t * tm:(t + 1) * tm]
        p2 = pos_ref[1:2, t * tm:(t + 1) * tm]
        u = u_ref[t * tm:(t + 1) * tm, :]
        for r0 in range(0, sorted_ref.shape[1], SORT_ROWS):
            q1 = (p1 - r0).astype(f32).astype(bf16)
            q2 = (p2 - r0).astype(f32).astype(bf16)
            sel = jnp.where((ri == q1) | (ri == q2), one, jnp.zeros_like(one))
            sorted_ref[t, r0:r0 + SORT_ROWS, :] = _pack_exact_rows(_dot(sel, u))
        _start_run_copies(tbl_ref, t, copy_from(t))
        pending_ref[t] = tbl_ref[t * SUBLANES + 4, 0]

    @pl.when(i == pl.num_programs(0) - 1)
    def _():
        for b in range(n_sub):
            _wait_rows(pending_ref[b], copy_from(b))
        blk = zero_ref.shape[0]
        n_blocks = xs_ref.shape[0] // blk
        zero_ref[...] = jnp.zeros_like(zero_ref)

        def zero_copy(unused_row, slot_row, rows):
            return pltpu.make_async_copy(zero_ref.at[pl.ds(0, rows)], xs_ref.at[pl.ds(slot_row, rows)], sems.at[n_sub])

        def per_expert(e, carry):
            first = tail_ref[0, e]
            n_big = tail_ref[1, e] // BIG_CHUNK

            def pieces(rows, start):
                def body(c, carry2):
                    zero_copy(0, pl.multiple_of(first + start + c * rows, RUN_ALIGN), rows).start()
                    return carry2
                return body

            lax.fori_loop(0, n_big, pieces(BIG_CHUNK, 0), 0)
            lax.fori_loop(0, (tail_ref[1, e] - n_big * BIG_CHUNK) // RUN_ALIGN, pieces(RUN_ALIGN, n_big * BIG_CHUNK), 0)
            return carry

        lax.fori_loop(0, N_EXPERTS, per_expert, 0)
        n_used = tail_ref[3, 0]

        def per_block(j, carry):
            zero_copy(0, pl.multiple_of(j * blk, blk), blk).start()
            return carry

        lax.fori_loop(n_used, n_blocks, per_block, 0)
        _wait_rows(tail_ref[2, 0] + (n_blocks - n_used) * blk, zero_copy)


def _dispatch(tbl, tail, pos, up, n_slots, tm, rmax):
    n = up.shape[0]
    group = _tile(n // tm, DISPATCH_GROUP)
    return pl.pallas_call(
        _dispatch_kernel,
        grid=(n // (tm * group),),
        in_specs=[pl.BlockSpec((group * SUBLANES, LANES), lambda i: (i, 0), memory_space=pltpu.SMEM),
                  pl.BlockSpec((SUBLANES, LANES), lambda i: (0, 0), memory_space=pltpu.SMEM),
                  pl.BlockSpec((SUBLANES, group * tm), lambda i: (0, i)),
                  pl.BlockSpec((group * tm, D_MODEL), lambda i: (i, 0))],
        out_specs=pl.BlockSpec(memory_space=pl.ANY),
        out_shape=jax.ShapeDtypeStruct((n_slots, HALF), u32),
        scratch_shapes=[pltpu.VMEM((group, rmax, HALF), u32), pltpu.VMEM((MOE_BLOCK, HALF), u32),
                        pltpu.SemaphoreType.DMA((group + 1,)), pltpu.SMEM((group,), i32)],
        compiler_params=_cparams(("arbitrary",)),
        name="dispatch",
    )(tbl, tail, pos, up)


def _expert_kernel(be_ref, nu_ref, xs_ref, wg_hbm, wu_hbm, wd_hbm, y_ref,
                   wg_f32, wu_f32, wd_f32, wg_bf, wu_bf, wd_bf, sems, slot_ref):
    j = pl.program_id(0)
    n_steps = pl.num_programs(0)
    expert = be_ref[j]

    def fetch(e, s):
        return [pltpu.make_async_copy(hbm.at[e], stage.at[s], sems.at[s, k])
                for k, (hbm, stage) in enumerate(((wg_hbm, wg_f32), (wu_hbm, wu_f32), (wd_hbm, wd_f32)))]

    @pl.when(j == 0)
    def _():
        slot_ref[0] = 0
        for c in fetch(expert, 0):
            c.start()

    @pl.when((j == 0) | (expert != be_ref[jnp.maximum(j - 1, 0)]))
    def _():
        s = slot_ref[0]
        for c in fetch(expert, s):
            c.wait()
        wg_bf[...] = wg_f32[s].astype(bf16)
        wu_bf[...] = wu_f32[s].astype(bf16)
        wd_bf[...] = wd_f32[s].astype(bf16)
        nxt = lax.while_loop(lambda k: (k < n_steps) & (be_ref[jnp.minimum(k, n_steps - 1)] == expert),
                             lambda k: k + 1, j + 1)

        @pl.when(nxt < n_steps)
        def _():
            for c in fetch(be_ref[jnp.minimum(nxt, n_steps - 1)], 1 - s):
                c.start()

        slot_ref[0] = 1 - s

    @pl.when(j < nu_ref[0])
    def _():
        xa, xb = _unpack_rows(xs_ref[...])
        x = jnp.concatenate([xa.astype(bf16), xb.astype(bf16)], axis=1)
        g = _dot(x, wg_bf[...])
        u = _dot(x, wu_bf[...])
        hb = (g * jax.nn.sigmoid(g) * u).astype(bf16)
        y_ref[...] = _pack_rows(_dot(hb, wd_bf[...]))

    @pl.when(j >= nu_ref[0])
    def _():
        y_ref[...] = jnp.zeros_like(y_ref)


def _experts(block_e, nused, xs, wg, wu, wd, blk):
    nb = xs.shape[0] // blk
    used = lambda j, be, nu: (jnp.minimum(j, nu[0] - 1), 0)
    hbm = pl.BlockSpec(memory_space=pl.ANY)
    return pl.pallas_call(
        _expert_kernel,
        grid_spec=pltpu.PrefetchScalarGridSpec(
            num_scalar_prefetch=2,
            grid=(nb,),
            in_specs=[pl.BlockSpec((blk, HALF), used), hbm, hbm, hbm],
            out_specs=pl.BlockSpec((blk, HALF), lambda j, be, nu: (j, 0)),
            scratch_shapes=[pltpu.VMEM((2, D_MODEL, D_EXPERT), f32), pltpu.VMEM((2, D_MODEL, D_EXPERT), f32),
                            pltpu.VMEM((2, D_EXPERT, D_MODEL), f32),
                            pltpu.VMEM((D_MODEL, D_EXPERT), bf16), pltpu.VMEM((D_MODEL, D_EXPERT), bf16),
                            pltpu.VMEM((D_EXPERT, D_MODEL), bf16),
                            pltpu.SemaphoreType.DMA((2, 3)), pltpu.SMEM((1,), i32)],
        ),
        out_shape=jax.ShapeDtypeStruct(xs.shape, u32),
        compiler_params=_cparams(("arbitrary",)),
        name="experts",
    )(block_e, nused, xs, wg, wu, wd)


def _combine_kernel(tbl_ref, tbl_next_ref, pos_ref, wts_ref, h_ref, gfin_ref, y_ref, o_ref, ys_ref, sems):
    i = pl.program_id(0)
    n_sub = ys_ref.shape[0] // 2
    tm = h_ref.shape[0] // n_sub
    cur = (i % 2) * n_sub
    nxt = n_sub - cur

    def copy_into(b):
        def copy(sorted_row, slot_row, rows):
            return pltpu.make_async_copy(y_ref.at[pl.ds(slot_row, rows)],
                                         ys_ref.at[b, pl.ds(sorted_row, rows)], sems.at[b])
        return copy

    @pl.when(i == 0)
    def _():
        ys_ref[...] = jnp.zeros_like(ys_ref)
        for t in range(n_sub):
            _start_run_copies(tbl_ref, t, copy_into(t))

    @pl.when(i + 1 < pl.num_programs(0))
    def _():
        for t in range(n_sub):
            _start_run_copies(tbl_next_ref, t, copy_into(nxt + t))

    li = lax.broadcasted_iota(i32, (tm, SORT_ROWS), 1).astype(f32).astype(bf16)
    none = jnp.zeros((tm, SORT_ROWS), bf16)
    pad = jnp.zeros((LANES - SUBLANES, tm), f32)
    for t in range(n_sub):
        rows = slice(t * tm, (t + 1) * tm)
        pos_cols = jnp.transpose(jnp.concatenate([pos_ref[:, rows].astype(f32), pad], axis=0))
        w_cols = jnp.transpose(jnp.concatenate([wts_ref[:, rows], pad], axis=0))
        p1, p2 = pos_cols[:, 0:1], pos_cols[:, 1:2]
        w1b = w_cols[:, 0:1].astype(bf16) + none
        w2b = w_cols[:, 1:2].astype(bf16) + none
        sel = jnp.concatenate(
            [jnp.where(li == (p1 - r0).astype(bf16), w1b, none) + jnp.where(li == (p2 - r0).astype(bf16), w2b, none)
             for r0 in range(0, ys_ref.shape[1], SORT_ROWS)], axis=1)
        _wait_rows(tbl_ref[t * SUBLANES + 4, 0], copy_into(cur + t))
        ya, yb = _unpack_rows(ys_ref[cur + t])
        ha = h_ref[rows, :HALF] + _dot(sel, ya.astype(bf16))
        hb = h_ref[rows, HALF:] + _dot(sel, yb.astype(bf16))
        ms = (jnp.sum(ha * ha, axis=-1, keepdims=True) + jnp.sum(hb * hb, axis=-1, keepdims=True)) * (1.0 / D_MODEL)
        inv = lax.rsqrt(ms + EPS)
        o_ref[rows, :HALF] = ha * inv * gfin_ref[:, :HALF]
        o_ref[rows, HALF:] = hb * inv * gfin_ref[:, HALF:]


def _combine(tbl, pos, wts, h, g_final, ybuf, tm, rmax):
    n = h.shape[0]
    group = _tile(n // tm, COMBINE_GROUP)
    n_steps = n // (tm * group)
    col = pl.BlockSpec((SUBLANES, group * tm), lambda i: (0, i))
    return pl.pallas_call(
        _combine_kernel,
        grid=(n_steps,),
        in_specs=[pl.BlockSpec((group * SUBLANES, LANES), lambda i: (i, 0), memory_space=pltpu.SMEM),
                  pl.BlockSpec((group * SUBLANES, LANES), lambda i: (jnp.minimum(i + 1, n_steps - 1), 0),
                               memory_space=pltpu.SMEM),
                  col, col,
                  pl.BlockSpec((group * tm, D_MODEL), lambda i: (i, 0)),
                  pl.BlockSpec(g_final.shape, lambda i: (0, 0)),
                  pl.BlockSpec(memory_space=pl.ANY)],
        out_specs=pl.BlockSpec((group * tm, D_MODEL), lambda i: (i, 0)),
        out_shape=jax.ShapeDtypeStruct((n, D_MODEL), f32),
        scratch_shapes=[pltpu.VMEM((2 * group, rmax, HALF), u32), pltpu.SemaphoreType.DMA((2 * group,))],
        compiler_params=_cparams(("arbitrary",)),
        name="combine",
    )(tbl, tbl, pos, wts, h, g_final, ybuf)


def _tile(n, pref):
    t = pref
    while n % t:
        t //= 2
    return t


def _prep_weights(g_mix, w_in, w_af2, b_af, w_ab2, b_ab, g_head, w_dw, b_dw, g_ln, b_ln, w_pw2, w_out,
                  g_ffn, w_rg, b_rg, w_re, b_re, w_e_gate, w_e_up, w_e_down, g_final):
    n_qkvo = 2 * GLA_KEY + 2 * GLA_VAL
    n_a = 2 * GATE_RANK
    w_in = w_in[0]
    zg = jnp.zeros((GATE_RANK, GLA_KEY), f32)
    wgate = jnp.concatenate([jnp.concatenate([w_af2[0], zg], axis=1), jnp.concatenate([zg, w_ab2[0]], axis=1)], axis=0)
    wdw = jnp.concatenate([w_dw[0], jnp.zeros((1, CONV_CH), f32)], axis=0)
    n_pad = 2 * SUBLANES - N_GROUPS
    wr = jnp.concatenate([w_re[0].T, w_rg[0].T, jnp.zeros((n_pad, D_MODEL), f32)], axis=0)
    br = jnp.concatenate([b_re[0], b_rg[0], jnp.zeros((n_pad,), f32)])[:, None]
    return dict(
        g_mix=g_mix[0][None, :],
        wqkvo=w_in[:, :n_qkvo].astype(bf16),
        wa=w_in[:, n_qkvo:n_qkvo + n_a].astype(bf16),
        wc=w_in[:, n_qkvo + n_a:].astype(bf16),
        wgate=wgate.astype(bf16),
        bgate=jnp.concatenate([b_af[0], b_ab[0]])[None, :],
        g_head=g_head[0][None, :],
        wdw=wdw, bdw=b_dw[0][None, :], gln=g_ln[0][None, :], bln=b_ln[0][None, :],
        wpw=w_pw2[0].astype(bf16),
        woa=w_out[0, :GLA_VAL].astype(bf16), wob=w_out[0, GLA_VAL:].astype(bf16),
        g_ffn=g_ffn[0][None, :], wr=wr, br=br,
        wg=w_e_gate[0], wu=w_e_up[0], wd=w_e_down[0],
        g_final=g_final[None, :],
    )


def _trunk(x3, w):
    n_seq, seq_len, _ = x3.shape
    n = n_seq * seq_len
    x = x3.reshape(n, D_MODEL)
    tm = _tile(n, MOE_TILE)
    q, k, v, og, la, glu = _inproj(x, w["g_mix"], w["wqkvo"], w["wa"], w["wc"], w["wgate"], w["bgate"],
                                   _tile(n, 1024))
    o_f, o_b = _gla(q, k, v, la, n_seq, seq_len)
    oc = _conv(glu, w["wdw"], w["bdw"], w["gln"], w["bln"], w["wpw"], seq_len, _tile(seq_len, CONV_TILE))
    h, up, eid, wts = _mix(x, o_f, o_b, og, oc, w["g_head"], w["woa"], w["wob"], w["g_ffn"], w["wr"], w["br"],
                            _tile(n, 1024))

    n_tiles = n // tm
    max_slots = n * TOP_K + n_tiles * N_EXPERTS * (RUN_ALIGN - 1)
    nb = -(-max_slots // MOE_BLOCK) + N_EXPERTS
    nbp = -(-nb // LANES) * LANES
    rmax = -(-(tm * TOP_K + N_EXPERTS * (RUN_ALIGN - 1)) // SORT_ROWS) * SORT_ROWS
    pos, tbl, blkinfo, tail = _rank(eid, MOE_BLOCK, nbp, tm)
    pos = pos[1]
    tbl = tbl[1].reshape(n_tiles * SUBLANES, LANES)
    xs = _dispatch(tbl, tail, pos, up, nb * MOE_BLOCK, tm, rmax)
    ybuf = _experts(blkinfo[0], blkinfo[1, :1], xs, w["wg"], w["wu"], w["wd"], MOE_BLOCK)
    y = _combine(tbl, pos, wts, h, w["g_final"], ybuf, tm, rmax)
    return y.reshape(x3.shape)


def kernel(x_prompt, x_sample, g_mix, w_in, w_af2, b_af, w_ab2, b_ab, g_head, w_dw, b_dw, g_ln, b_ln, w_pw2,
           w_out, g_ffn, w_rg, b_rg, w_re, b_re, w_e_gate, w_e_up, w_e_down, g_final):
    w = _prep_weights(g_mix, w_in, w_af2, b_af, w_ab2, b_ab, g_head, w_dw, b_dw, g_ln, b_ln, w_pw2, w_out,
                      g_ffn, w_rg, b_rg, w_re, b_re, w_e_gate, w_e_up, w_e_down, g_final)
    return (_trunk(x_prompt, w), _trunk(x_sample, w))
```

```python
import functools

import jax
import jax.numpy as jnp
from jax import lax
from jax.experimental import pallas as pl
from jax.experimental.pallas import tpu as pltpu

f32 = jnp.float32
bf16 = jnp.bfloat16
i32 = jnp.int32
u32 = jnp.uint32

D_MODEL = 1024
GLA_HEADS = 4
GLA_DK = 64
GLA_DV = 128
GLA_KEY = GLA_HEADS * GLA_DK
GLA_VAL = GLA_HEADS * GLA_DV
GATE_RANK = 16
GATE_NORMALIZER = 16
CONV_CH = 512
CONV_K = 31
N_GROUPS = 4
EXPERTS_PER_GROUP = 8
N_EXPERTS = N_GROUPS * EXPERTS_PER_GROUP
TOP_K = 2
D_EXPERT = 512
EPS = 1e-6

LANES = 128
SUBLANES = 8
GLA_CHUNK = 128
GLA_TILE = 1024
GLA_CUM_ROWS = 256
PAIR_DK = 2 * GLA_DK
PAIR_DV = 2 * GLA_DV
CONV_HALO = 16
CONV_ROWS = 32
CONV_TILE = 1024
HALF = D_MODEL // 2
MOE_TILE = 512
MOE_BLOCK = 512
RUN_ALIGN = SUBLANES
SORT_ROWS = 256
RANK_GROUP = 8
DISPATCH_GROUP = 2
COMBINE_GROUP = 2
BIG_CHUNK = 32
WAIT_ROWS = 256
VMEM_LIMIT = 56 * 1024 * 1024


def _cparams(sem, **kw):
    return pltpu.CompilerParams(dimension_semantics=sem, vmem_limit_bytes=VMEM_LIMIT, **kw)


def _dot(a, b):
    return jnp.dot(a, b, preferred_element_type=f32)


def _dot_nt(a, b):
    return lax.dot_general(a, b, (((1,), (1,)), ((), ())), preferred_element_type=f32)


def _dot_tn(a, b):
    return lax.dot_general(a, b, (((0,), (0,)), ((), ())), preferred_element_type=f32)


def _split_bf16(x):
    hi = x.astype(bf16)
    lo = (x - hi.astype(f32)).astype(bf16)
    return hi, lo


def _pack_rows(x):
    a = lax.bitcast_convert_type(x[:, :HALF].astype(bf16).astype(f32), u32)
    b = lax.bitcast_convert_type(x[:, HALF:].astype(bf16).astype(f32), u32)
    return a | (b >> 16)


def _pack_exact_rows(x):
    a = lax.bitcast_convert_type(x[:, :HALF], u32)
    b = lax.bitcast_convert_type(x[:, HALF:], u32)
    return a | (b >> 16)


def _unpack_rows(w):
    a = lax.bitcast_convert_type(w & jnp.uint32(0xFFFF0000), f32)
    b = lax.bitcast_convert_type(w << 16, f32)
    return a, b


def _rms(x):
    return x * lax.rsqrt(jnp.mean(x * x, axis=-1, keepdims=True) + EPS)


def _inproj_kernel(x_ref, g_ref, wqkvo_ref, wa_ref, wc_ref, wgate_ref, bgate_ref,
                   q_ref, k_ref, v_ref, og_ref, la_ref, glu_ref):
    u = (_rms(x_ref[...]) * g_ref[...]).astype(bf16)
    a = _dot(u, wa_ref[...]).astype(bf16)
    p = _dot(u, wqkvo_ref[...])
    q_ref[...] = (p[:, :GLA_KEY] * (GLA_DK ** -0.5)).astype(bf16)
    k_ref[...] = p[:, GLA_KEY:2 * GLA_KEY].astype(bf16)
    v_ref[...] = p[:, 2 * GLA_KEY:2 * GLA_KEY + GLA_VAL].astype(bf16)
    og_ref[...] = p[:, 2 * GLA_KEY + GLA_VAL:].astype(bf16)
    z = _dot(a, wgate_ref[...]) + bgate_ref[...]
    la_ref[...] = (jnp.minimum(z, 0.0) - jnp.log1p(jnp.exp(-jnp.abs(z)))) * (1.0 / GATE_NORMALIZER)
    c = _dot(u, wc_ref[...])
    glu_ref[...] = (c[:, :CONV_CH] * jax.nn.sigmoid(c[:, CONV_CH:])).astype(bf16)


def _inproj(x, g_mix, wqkvo, wa, wc, wgate, bgate, tm):
    n = x.shape[0]
    row = lambda w: pl.BlockSpec((tm, w), lambda i: (i, 0))
    full = lambda a: pl.BlockSpec(a.shape, lambda i: (0, 0))
    return pl.pallas_call(
        _inproj_kernel,
        grid=(n // tm,),
        in_specs=[row(D_MODEL), full(g_mix), full(wqkvo), full(wa), full(wc), full(wgate), full(bgate)],
        out_specs=[row(GLA_KEY), row(GLA_KEY), row(GLA_VAL), row(GLA_VAL), row(2 * GLA_KEY), row(CONV_CH)],
        out_shape=[jax.ShapeDtypeStruct((n, GLA_KEY), bf16), jax.ShapeDtypeStruct((n, GLA_KEY), bf16),
                   jax.ShapeDtypeStruct((n, GLA_VAL), bf16), jax.ShapeDtypeStruct((n, GLA_VAL), bf16),
                   jax.ShapeDtypeStruct((n, 2 * GLA_KEY), f32), jax.ShapeDtypeStruct((n, CONV_CH), bf16)],
        compiler_params=_cparams(("parallel",)),
        name="inproj",
    )(x, g_mix, wqkvo, wa, wc, wgate, bgate)


def _gla_unit(q_ref, k_ref, v_ref, cs, la, pair, reverse, masks):
    head0_k, head0_v, keep = masks
    tl = cs.shape[0]
    kl = slice(pair * PAIR_DK, (pair + 1) * PAIR_DK)
    vl = slice(pair * PAIR_DV, (pair + 1) * PAIR_DV)
    n_chunks = tl // GLA_CHUNK
    order = list(range(n_chunks - 1, -1, -1) if reverse else range(n_chunks))

    def stack_heads(x, head0):
        z = jnp.zeros_like(x)
        return jnp.concatenate([jnp.where(head0, x, z), jnp.where(head0, z, x)], axis=0)

    local = {}
    for j in order:
        rows = slice(j * GLA_CHUNK, (j + 1) * GLA_CHUNK)
        cj = cs[rows, kl]
        q = q_ref[rows, kl].astype(f32)
        k = k_ref[rows, kl].astype(f32)
        c_last = cj[GLA_CHUNK - 1:GLA_CHUNK, :]
        mid = 0.5 * c_last
        rel = (mid - (cj - la[rows, kl])) if reverse else (cj - mid)
        qd = q * jnp.exp(rel)
        kd = k * jnp.exp(-rel)
        edge = jnp.exp(mid)
        qi = qd * edge
        ku = kd * edge
        vbd = stack_heads(v_ref[rows, vl], head0_v)
        sc = _dot_nt(qd.astype(bf16), stack_heads(kd.astype(bf16), head0_k))
        p = jnp.where(keep, sc, 0.0).astype(bf16)
        kv = _dot_tn(stack_heads(ku.astype(bf16), head0_k), vbd)
        local[j] = (p, vbd, qi.astype(bf16), kv, c_last)
    return order, vl, local


def _gla_finish(o_ref, st_ref, order, vl, local):
    row_id = lax.broadcasted_iota(i32, (PAIR_DK, PAIR_DK), 0)
    c_rows = jnp.zeros((PAIR_DK, PAIR_DK), f32)
    for j in order:
        c_rows = jnp.where(row_id == j, local[j][4], c_rows)
    decay_cols = jnp.exp(jnp.transpose(c_rows))
    st = st_ref[...]
    st_before = {}
    for j in order:
        st_before[j] = st.astype(bf16)
        st = decay_cols[:, j:j + 1] * st + local[j][3]
    st_ref[...] = st
    for j in order:
        p, vbd, qi, _, _ = local[j]
        o = _dot(p, vbd) + _dot(qi, st_before[j])
        o_ref[j * GLA_CHUNK:(j + 1) * GLA_CHUNK, vl] = o.astype(o_ref.dtype)


def _gla_kernel(qf, kf, vf, laf, qb, kb, vb, lab, of, ob, st_ref, tri_ref):
    tl = qf.shape[0]

    @pl.when(pl.program_id(1) == 0)
    def _():
        st_ref[...] = jnp.zeros_like(st_ref)
        r = lax.broadcasted_iota(i32, tri_ref.shape, 0)
        c = lax.broadcasted_iota(i32, tri_ref.shape, 1)
        tri_ref[...] = jnp.where((r // GLA_CHUNK == c // GLA_CHUNK) & (c <= r), 1.0, 0.0).astype(bf16)

    tri = tri_ref[...]
    cum_rows = tri_ref.shape[0]

    def chunk_cumsum(x):
        blocks = [_dot(tri, x[r0:r0 + cum_rows]) for r0 in range(0, tl, cum_rows)]
        return blocks[0] if len(blocks) == 1 else jnp.concatenate(blocks, axis=0)

    head0_k = lax.broadcasted_iota(i32, (GLA_CHUNK, PAIR_DK), 1) < GLA_DK
    head0_v = lax.broadcasted_iota(i32, (GLA_CHUNK, PAIR_DV), 1) < GLA_DV
    t_idx = lax.broadcasted_iota(i32, (GLA_CHUNK, 2 * GLA_CHUNK), 0)
    s_idx = lax.broadcasted_iota(i32, (GLA_CHUNK, 2 * GLA_CHUNK), 1) % GLA_CHUNK
    units = []
    for d, (q_ref, k_ref, v_ref, la_ref, o_ref) in enumerate(((qf, kf, vf, laf, of), (qb, kb, vb, lab, ob))):
        reverse = d == 1
        la = la_ref[...]
        la_hi, la_lo = _split_bf16(la)
        cs = chunk_cumsum(la_hi) + chunk_cumsum(la_lo)
        keep = (s_idx > t_idx) if reverse else (s_idx <= t_idx)
        for pair in range(GLA_HEADS // 2):
            units.append((o_ref, st_ref.at[d, pair]) + _gla_unit(q_ref, k_ref, v_ref, cs, la, pair, reverse,
                                                                  (head0_k, head0_v, keep)))
    for unit in units:
        _gla_finish(*unit)


def _gla(q, k, v, la, n_seq, seq_len):
    n = q.shape[0]
    tl = _tile(seq_len, GLA_TILE)
    assert tl % GLA_CUM_ROWS == 0 and GLA_CUM_ROWS % GLA_CHUNK == 0
    nblk = seq_len // tl
    fwd = lambda b, i: (b * nblk + i, 0)
    bwd = lambda b, i: (b * nblk + nblk - 1 - i, 0)
    bwd_la = lambda b, i: (b * nblk + nblk - 1 - i, 1)
    kq = lambda im: pl.BlockSpec((tl, GLA_KEY), im)
    vv = lambda im: pl.BlockSpec((tl, GLA_VAL), im)
    return pl.pallas_call(
        _gla_kernel,
        grid=(n_seq, nblk),
        in_specs=[kq(fwd), kq(fwd), vv(fwd), kq(fwd), kq(bwd), kq(bwd), vv(bwd), kq(bwd_la)],
        out_specs=[vv(fwd), vv(bwd)],
        out_shape=[jax.ShapeDtypeStruct((n, GLA_VAL), bf16), jax.ShapeDtypeStruct((n, GLA_VAL), bf16)],
        scratch_shapes=[pltpu.VMEM((2, GLA_HEADS // 2, PAIR_DK, PAIR_DV), f32), pltpu.VMEM((GLA_CUM_ROWS, GLA_CUM_ROWS), bf16)],
        compiler_params=_cparams(("parallel", "arbitrary")),
        name="gla",
    )(q, k, v, la, q, k, v, la)


def _conv_kernel(prev_ref, cur_ref, next_ref, wdw_ref, bdw_ref, gln_ref, bln_ref, wpw_ref,
                 o_ref, ext_ref, y_ref, *, tiles_per_seq):
    tl = cur_ref.shape[0]
    pos = pl.program_id(0) % tiles_per_seq
    prev = jnp.where(pos == 0, 0.0, prev_ref[...].astype(f32))
    nxt = jnp.where(pos == tiles_per_seq - 1, 0.0, next_ref[...].astype(f32))
    cur = cur_ref[...].astype(f32)
    n_slabs = CONV_CH // LANES
    for c in range(n_slabs):
        lanes = slice(c * LANES, (c + 1) * LANES)
        ext_ref[c, 0:CONV_HALO, :] = prev[:, lanes]
        ext_ref[c, CONV_HALO:CONV_HALO + tl, :] = cur[:, lanes]
        ext_ref[c, CONV_HALO + tl:, :] = nxt[:, lanes]
    off = CONV_HALO - CONV_K // 2

    def body(rb, carry):
        base = pl.multiple_of(rb * CONV_ROWS, CONV_ROWS)
        for c in range(n_slabs):
            lanes = slice(c * LANES, (c + 1) * LANES)
            a = jnp.zeros((CONV_ROWS, LANES), f32) + bdw_ref[:, lanes]
            for j in range(CONV_K):
                taps = ext_ref.at[c, pl.ds(off + j, tl)]
                a = a + taps[pl.ds(base, CONV_ROWS), :] * wdw_ref[j:j + 1, lanes]
            y_ref[pl.ds(base, CONV_ROWS), lanes] = a
        return carry

    lax.fori_loop(0, tl // CONV_ROWS, body, 0)
    acc = y_ref[...]
    mu = jnp.mean(acc, axis=-1, keepdims=True)
    xc = acc - mu
    yn = xc * lax.rsqrt(jnp.mean(xc * xc, axis=-1, keepdims=True) + EPS) * gln_ref[...] + bln_ref[...]
    o_ref[...] = _dot((yn * jax.nn.sigmoid(yn)).astype(bf16), wpw_ref[...]).astype(bf16)


def _conv(glu, wdw, bdw, gln, bln, wpw, seq_len, tl):
    n = glu.shape[0]
    hb = tl // CONV_HALO
    n_halo = n // CONV_HALO
    full = lambda a: pl.BlockSpec(a.shape, lambda i: (0, 0))
    return pl.pallas_call(
        functools.partial(_conv_kernel, tiles_per_seq=seq_len // tl),
        grid=(n // tl,),
        in_specs=[pl.BlockSpec((CONV_HALO, CONV_CH), lambda i: (jnp.maximum(i * hb - 1, 0), 0)),
                  pl.BlockSpec((tl, CONV_CH), lambda i: (i, 0)),
                  pl.BlockSpec((CONV_HALO, CONV_CH), lambda i: (jnp.minimum((i + 1) * hb, n_halo - 1), 0)),
                  full(wdw), full(bdw), full(gln), full(bln), full(wpw)],
        out_specs=pl.BlockSpec((tl, CONV_CH), lambda i: (i, 0)),
        out_shape=jax.ShapeDtypeStruct((n, CONV_CH), bf16),
        scratch_shapes=[pltpu.VMEM((CONV_CH // LANES, tl + 2 * CONV_HALO, LANES), f32),
                        pltpu.VMEM((tl, CONV_CH), f32)],
        compiler_params=_cparams(("parallel",)),
        name="conv",
    )(glu, glu, glu, wdw, bdw, gln, bln, wpw)


def _mix_kernel(x_ref, of_ref, ob_ref, og_ref, oc_ref, gh_ref, woa_ref, wob_ref, gf_ref, wr_ref, br_ref,
                h_ref, up_ref, eid_ref, wts_ref):
    tm = x_ref.shape[0]
    o = of_ref[...].astype(f32) + ob_ref[...].astype(f32)
    o = jnp.concatenate([_rms(o[:, h * GLA_DV:(h + 1) * GLA_DV]) for h in range(GLA_HEADS)], axis=1)
    og = og_ref[...].astype(f32)
    o = o * gh_ref[...] * (og * jax.nn.sigmoid(og))
    h = x_ref[...] + _dot(o.astype(bf16), woa_ref[...]) + _dot(oc_ref[...], wob_ref[...])
    h_ref[...] = h
    u = _rms(h) * gf_ref[...]
    u_hi, u_lo = _split_bf16(u)
    up_ref[...] = u_hi

    w_hi, w_lo = _split_bf16(wr_ref[...])
    n_r = w_hi.shape[0]
    both = _dot_nt(jnp.concatenate([w_hi, w_lo], axis=0), u_hi)
    logits = both[:n_r] + both[n_r:] + _dot_nt(w_hi, u_lo) + br_ref[...]
    le = logits[0:N_EXPERTS]
    lg = logits[N_EXPERTS:N_EXPERTS + N_GROUPS]
    gmax = jnp.max(lg, axis=0, keepdims=True)
    gi = lax.broadcasted_iota(i32, lg.shape, 0)
    gsel = jnp.min(jnp.where(lg == gmax, gi, N_GROUPS), axis=0, keepdims=True)
    gate = 1.0 / jnp.sum(jnp.exp(lg - gmax), axis=0, keepdims=True)
    ri = lax.broadcasted_iota(i32, le.shape, 0)
    lm = jnp.where(ri // EXPERTS_PER_GROUP == gsel, le, -jnp.inf)
    m1 = jnp.max(lm, axis=0, keepdims=True)
    i1 = jnp.min(jnp.where(lm == m1, ri, N_EXPERTS), axis=0, keepdims=True)
    lm2 = jnp.where(ri == i1, -jnp.inf, lm)
    m2 = jnp.max(lm2, axis=0, keepdims=True)
    i2 = jnp.min(jnp.where(lm2 == m2, ri, N_EXPERTS), axis=0, keepdims=True)
    t = jnp.exp(m2 - m1)
    den = 1.0 / (1.0 + t)
    r8 = lax.broadcasted_iota(i32, (SUBLANES, tm), 0)
    eid_ref[...] = jnp.where(r8 == 0, i1, jnp.where(r8 == 1, i2, 0))
    wts_ref[...] = jnp.where(r8 == 0, gate * den, jnp.where(r8 == 1, gate * t * den, 0.0))


def _mix(x, o_f, o_b, og, oc, g_head, woa, wob, g_ffn, wr, br, tm):
    n = x.shape[0]
    row = lambda w: pl.BlockSpec((tm, w), lambda i: (i, 0))
    col = pl.BlockSpec((SUBLANES, tm), lambda i: (0, i))
    full = lambda a: pl.BlockSpec(a.shape, lambda i: (0, 0))
    return pl.pallas_call(
        _mix_kernel,
        grid=(n // tm,),
        in_specs=[row(D_MODEL), row(GLA_VAL), row(GLA_VAL), row(GLA_VAL), row(CONV_CH),
                  full(g_head), full(woa), full(wob), full(g_ffn), full(wr), full(br)],
        out_specs=[row(D_MODEL), row(D_MODEL), col, col],
        out_shape=[jax.ShapeDtypeStruct((n, D_MODEL), f32), jax.ShapeDtypeStruct((n, D_MODEL), bf16),
                   jax.ShapeDtypeStruct((SUBLANES, n), i32), jax.ShapeDtypeStruct((SUBLANES, n), f32)],
        compiler_params=_cparams(("parallel",)),
        name="mix",
    )(x, o_f, o_b, og, oc, g_head, woa, wob, g_ffn, wr, br)


def _rank_kernel(eid_ref, pos_ref, tbl_ref, blk_ref, tail_ref, cnt_ref, base_ref, end_ref, *, blk, nbp, tm):
    ps = pl.program_id(0)
    i = pl.program_id(1)
    rr = lax.broadcasted_iota(i32, (N_EXPERTS, LANES), 0)

    def cumsum_experts(x):
        for s in (1, 2, 4, 8, 16):
            x = x + jnp.where(rr >= s, pltpu.roll(x, s, axis=0), 0.0)
        return x

    @pl.when((ps == 0) & (i == 0))
    def _():
        cnt_ref[...] = jnp.zeros_like(cnt_ref)

    @pl.when(ps == 0)
    def _():
        pos_ref[...] = jnp.zeros_like(pos_ref)
        tbl_ref[...] = jnp.zeros_like(tbl_ref)

    @pl.when((ps == 1) & (i == 0))
    def _():
        cnt = cnt_ref[...]
        pc = jnp.floor((cnt + (blk - 1)) * (1.0 / blk)) * blk
        inc = cumsum_experts(pc)
        base_ref[...] = inc - pc
        end_ref[...] = inc
        pend = jnp.concatenate([inc] * (nbp // LANES), axis=1)
        jl = lax.broadcasted_iota(i32, (N_EXPERTS, nbp), 1).astype(f32) * blk
        be = jnp.minimum(jnp.sum(jnp.where(pend <= jl, 1.0, 0.0), axis=0, keepdims=True), N_EXPERTS - 1.0)
        nused = jnp.concatenate([inc[N_EXPERTS - 1:N_EXPERTS, :]] * (nbp // LANES), axis=1) * (1.0 / blk)
        r8 = lax.broadcasted_iota(i32, (SUBLANES, nbp), 0)
        blk_ref[...] = jnp.where(r8 == 0, be, jnp.where(r8 == 1, nused, 0.0)).astype(i32)

    ri = lax.broadcasted_iota(i32, (N_EXPERTS, tm), 0)
    for t in range(eid_ref.shape[1] // tm):
        lanes = slice(t * tm, (t + 1) * tm)
        oh1 = ri == eid_ref[0:1, lanes]
        oh2 = ri == eid_ref[1:2, lanes]
        ohf = jnp.where(oh1 | oh2, 1.0, 0.0)
        tile_cnt = jnp.sum(ohf, axis=1, keepdims=True)
        run_len = (jnp.floor((tile_cnt + (RUN_ALIGN - 1)) * (1.0 / RUN_ALIGN)) * RUN_ALIGN
                   + jnp.zeros((N_EXPERTS, LANES), f32))

        @pl.when(ps == 0)
        def _():
            cnt_ref[...] += run_len

        @pl.when(ps == 1)
        def _():
            a = lax.broadcasted_iota(i32, (tm, tm), 0)
            b = lax.broadcasted_iota(i32, (tm, tm), 1)
            upper = jnp.where(a < b, 1.0, 0.0).astype(bf16)
            before = _dot(ohf.astype(bf16), upper)
            run_start = cumsum_experts(run_len) - run_len
            row = run_start[:, 0:1] + before
            p1 = jnp.sum(jnp.where(oh1, row, 0.0), axis=0, keepdims=True)
            p2 = jnp.sum(jnp.where(oh2, row, 0.0), axis=0, keepdims=True)
            r8 = lax.broadcasted_iota(i32, (SUBLANES, tm), 0)
            pos_ref[0, :, lanes] = jnp.where(r8 == 0, p1, jnp.where(r8 == 1, p2, 0.0)).astype(i32)
            diag = lax.broadcasted_iota(i32, (N_EXPERTS, LANES), 1) == rr
            on_lanes = lambda x: jnp.sum(jnp.where(diag, x, 0.0), axis=0, keepdims=True)
            big = jnp.floor(run_len * (1.0 / BIG_CHUNK))
            small = (run_len - big * BIG_CHUNK) * (1.0 / RUN_ALIGN)
            rows = (on_lanes(run_start), on_lanes(base_ref[...]), on_lanes(big), on_lanes(small),
                    jnp.sum(run_len, axis=0, keepdims=True))
            t8 = lax.broadcasted_iota(i32, (SUBLANES, LANES), 0)
            tbl = jnp.zeros((SUBLANES, LANES), f32)
            for k, x in enumerate(rows):
                tbl = jnp.where(t8 == k, x, tbl)
            tbl_ref[0, t] = tbl.astype(i32)
            base_ref[...] += run_len

    @pl.when((ps == 1) & (i == pl.num_programs(1) - 1))
    def _():
        diag = lax.broadcasted_iota(i32, (N_EXPERTS, LANES), 1) == rr
        on_lanes = lambda x: jnp.sum(jnp.where(diag, x, 0.0), axis=0, keepdims=True)
        first = base_ref[...]
        count = end_ref[...] - first
        rows = (on_lanes(first), on_lanes(count), jnp.sum(count, axis=0, keepdims=True),
                end_ref[N_EXPERTS - 1:N_EXPERTS, :] * (1.0 / blk))
        t8 = lax.broadcasted_iota(i32, (SUBLANES, LANES), 0)
        tail = jnp.zeros((SUBLANES, LANES), f32)
        for k, x in enumerate(rows):
            tail = jnp.where(t8 == k, x, tail)
        tail_ref[...] = tail.astype(i32)


def _rank(eid, blk, nbp, tm):
    n = eid.shape[1]
    group = _tile(n // tm, RANK_GROUP)
    return pl.pallas_call(
        functools.partial(_rank_kernel, blk=blk, nbp=nbp, tm=tm),
        grid=(2, n // (tm * group)),
        in_specs=[pl.BlockSpec((SUBLANES, tm * group), lambda p, i: (0, i))],
        out_specs=[pl.BlockSpec((1, SUBLANES, tm * group), lambda p, i: (p, 0, i)),
                   pl.BlockSpec((1, group, SUBLANES, LANES), lambda p, i: (p, i, 0, 0)),
                   pl.BlockSpec((SUBLANES, nbp), lambda p, i: (0, 0)),
                   pl.BlockSpec((SUBLANES, LANES), lambda p, i: (0, 0))],
        out_shape=[jax.ShapeDtypeStruct((2, SUBLANES, n), i32),
                   jax.ShapeDtypeStruct((2, n // tm, SUBLANES, LANES), i32),
                   jax.ShapeDtypeStruct((SUBLANES, nbp), i32),
                   jax.ShapeDtypeStruct((SUBLANES, LANES), i32)],
        scratch_shapes=[pltpu.VMEM((N_EXPERTS, LANES), f32)] * 3,
        compiler_params=_cparams(("arbitrary", "arbitrary")),
        name="rank",
    )(eid)


def _start_run_copies(tbl_ref, t, copy):
    base = t * SUBLANES

    def per_expert(e, carry):
        sorted0 = tbl_ref[base, e]
        slot0 = tbl_ref[base + 1, e]
        n_big = tbl_ref[base + 2, e]

        def pieces(rows, first):
            def body(c, carry2):
                off = first + c * rows
                copy(pl.multiple_of(sorted0 + off, RUN_ALIGN), pl.multiple_of(slot0 + off, RUN_ALIGN), rows).start()
                return carry2
            return body

        lax.fori_loop(0, n_big, pieces(BIG_CHUNK, 0), 0)
        lax.fori_loop(0, tbl_ref[base + 3, e], pieces(RUN_ALIGN, n_big * BIG_CHUNK), 0)
        return carry

    lax.fori_loop(0, N_EXPERTS, per_expert, 0)


def _wait_rows(total_rows, copy):
    def waits(rows):
        def body(c, carry):
            copy(0, 0, rows).wait()
            return carry
        return body

    n_wide = total_rows // WAIT_ROWS
    lax.fori_loop(0, n_wide, waits(WAIT_ROWS), 0)
    lax.fori_loop(0, (total_rows - n_wide * WAIT_ROWS) // RUN_ALIGN, waits(RUN_ALIGN), 0)


def _dispatch_kernel(tbl_ref, tail_ref, pos_ref, u_ref, xs_ref, sorted_ref, zero_ref, sems, pending_ref):
    i = pl.program_id(0)
    n_sub = sorted_ref.shape[0]
    tm = u_ref.shape[0] // n_sub

    def copy_from(b):
        def copy(sorted_row, slot_row, rows):
            return pltpu.make_async_copy(sorted_ref.at[b, pl.ds(sorted_row, rows)],
                                         xs_ref.at[pl.ds(slot_row, rows)], sems.at[b])
        return copy

    @pl.when(i == 0)
    def _():
        for b in range(n_sub):
            pending_ref[b] = 0

    ri = lax.broadcasted_iota(i32, (SORT_ROWS, tm), 0).astype(f32).astype(bf16)
    one = jnp.ones((SORT_ROWS, tm), bf16)
    for t in range(n_sub):
        _wait_rows(pending_ref[t], copy_from(t))
    for t in range(n_sub):
        p1 = pos_ref[0:1, t * tm:(t + 1) * tm]
        p2 = pos_ref[1:2, t * tm:(t + 1) * tm]
        u = u_ref[t * tm:(t + 1) * tm, :]
        for r0 in range(0, sorted_ref.shape[1], SORT_ROWS):
            q1 = (p1 - r0).astype(f32).astype(bf16)
            q2 = (p2 - r0).astype(f32).astype(bf16)
            sel = jnp.where((ri == q1) | (ri == q2), one, jnp.zeros_like(one))
            sorted_ref[t, r0:r0 + SORT_ROWS, :] = _pack_exact_rows(_dot(sel, u))
    for t in range(n_sub):
        _start_run_copies(tbl_ref, t, copy_from(t))
        pending_ref[t] = tbl_ref[t * SUBLANES + 4, 0]

    @pl.when(i == pl.num_programs(0) - 1)
    def _():
        for b in range(n_sub):
            _wait_rows(pending_ref[b], copy_from(b))
        blk = zero_ref.shape[0]
        n_blocks = xs_ref.shape[0] // blk
        zero_ref[...] = jnp.zeros_like(zero_ref)

        def zero_copy(unused_row, slot_row, rows):
            return pltpu.make_async_copy(zero_ref.at[pl.ds(0, rows)], xs_ref.at[pl.ds(slot_row, rows)], sems.at[n_sub])

        def per_expert(e, carry):
            first = tail_ref[0, e]
            n_big = tail_ref[1, e] // BIG_CHUNK

            def pieces(rows, start):
                def body(c, carry2):
                    zero_copy(0, pl.multiple_of(first + start + c * rows, RUN_ALIGN), rows).start()
                    return carry2
                return body

            lax.fori_loop(0, n_big, pieces(BIG_CHUNK, 0), 0)
            lax.fori_loop(0, (tail_ref[1, e] - n_big * BIG_CHUNK) // RUN_ALIGN, pieces(RUN_ALIGN, n_big * BIG_CHUNK), 0)
            return carry

        lax.fori_loop(0, N_EXPERTS, per_expert, 0)
        n_used = tail_ref[3, 0]

        def per_block(j, carry):
            zero_copy(0, pl.multiple_of(j * blk, blk), blk).start()
            return carry

        lax.fori_loop(n_used, n_blocks, per_block, 0)
        _wait_rows(tail_ref[2, 0] + (n_blocks - n_used) * blk, zero_copy)


def _dispatch(tbl, tail, pos, up, n_slots, tm, rmax):
    n = up.shape[0]
    group = _tile(n // tm, DISPATCH_GROUP)
    return pl.pallas_call(
        _dispatch_kernel,
        grid=(n // (tm * group),),
        in_specs=[pl.BlockSpec((group * SUBLANES, LANES), lambda i: (i, 0), memory_space=pltpu.SMEM),
                  pl.BlockSpec((SUBLANES, LANES), lambda i: (0, 0), memory_space=pltpu.SMEM),
                  pl.BlockSpec((SUBLANES, group * tm), lambda i: (0, i)),
                  pl.BlockSpec((group * tm, D_MODEL), lambda i: (i, 0))],
        out_specs=pl.BlockSpec(memory_space=pl.ANY),
        out_shape=jax.ShapeDtypeStruct((n_slots, HALF), u32),
        scratch_shapes=[pltpu.VMEM((group, rmax, HALF), u32), pltpu.VMEM((MOE_BLOCK, HALF), u32),
                        pltpu.SemaphoreType.DMA((group + 1,)), pltpu.SMEM((group,), i32)],
        compiler_params=_cparams(("arbitrary",)),
        name="dispatch",
    )(tbl, tail, pos, up)


def _expert_kernel(be_ref, nu_ref, xs_ref, wg_hbm, wu_hbm, wd_hbm, y_ref,
                   wg_f32, wu_f32, wd_f32, wg_bf, wu_bf, wd_bf, sems, slot_ref):
    j = pl.program_id(0)
    n_steps = pl.num_programs(0)
    expert = be_ref[j]

    def fetch(e, s):
        return [pltpu.make_async_copy(hbm.at[e], stage.at[s], sems.at[s, k])
                for k, (hbm, stage) in enumerate(((wg_hbm, wg_f32), (wu_hbm, wu_f32), (wd_hbm, wd_f32)))]

    @pl.when(j == 0)
    def _():
        slot_ref[0] = 0
        for c in fetch(expert, 0):
            c.start()

    @pl.when((j == 0) | (expert != be_ref[jnp.maximum(j - 1, 0)]))
    def _():
        s = slot_ref[0]
        for c in fetch(expert, s):
            c.wait()
        wg_bf[...] = wg_f32[s].astype(bf16)
        wu_bf[...] = wu_f32[s].astype(bf16)
        wd_bf[...] = wd_f32[s].astype(bf16)
        nxt = lax.while_loop(lambda k: (k < n_steps) & (be_ref[jnp.minimum(k, n_steps - 1)] == expert),
                             lambda k: k + 1, j + 1)

        @pl.when(nxt < n_steps)
        def _():
            for c in fetch(be_ref[jnp.minimum(nxt, n_steps - 1)], 1 - s):
                c.start()

        slot_ref[0] = 1 - s

    @pl.when(j < nu_ref[0])
    def _():
        xa, xb = _unpack_rows(xs_ref[...])
        x = jnp.concatenate([xa.astype(bf16), xb.astype(bf16)], axis=1)
        g = _dot(x, wg_bf[...])
        u = _dot(x, wu_bf[...])
        hb = (g * jax.nn.sigmoid(g) * u).astype(bf16)
        y_ref[...] = _pack_rows(_dot(hb, wd_bf[...]))

    @pl.when(j >= nu_ref[0])
    def _():
        y_ref[...] = jnp.zeros_like(y_ref)


def _experts(block_e, nused, xs, wg, wu, wd, blk):
    nb = xs.shape[0] // blk
    used = lambda j, be, nu: (jnp.minimum(j, nu[0] - 1), 0)
    hbm = pl.BlockSpec(memory_space=pl.ANY)
    return pl.pallas_call(
        _expert_kernel,
        grid_spec=pltpu.PrefetchScalarGridSpec(
            num_scalar_prefetch=2,
            grid=(nb,),
            in_specs=[pl.BlockSpec((blk, HALF), used), hbm, hbm, hbm],
            out_specs=pl.BlockSpec((blk, HALF), lambda j, be, nu: (j, 0)),
            scratch_shapes=[pltpu.VMEM((2, D_MODEL, D_EXPERT), f32), pltpu.VMEM((2, D_MODEL, D_EXPERT), f32),
                            pltpu.VMEM((2, D_EXPERT, D_MODEL), f32),
                            pltpu.VMEM((D_MODEL, D_EXPERT), bf16), pltpu.VMEM((D_MODEL, D_EXPERT), bf16),
                            pltpu.VMEM((D_EXPERT, D_MODEL), bf16),
                            pltpu.SemaphoreType.DMA((2, 3)), pltpu.SMEM((1,), i32)],
        ),
        out_shape=jax.ShapeDtypeStruct(xs.shape, u32),
        compiler_params=_cparams(("arbitrary",)),
        name="experts",
    )(block_e, nused, xs, wg, wu, wd)


def _combine_kernel(tbl_ref, tbl_next_ref, pos_ref, wts_ref, h_ref, gfin_ref, y_ref, o_ref, ys_ref, sems):
    i = pl.program_id(0)
    n_sub = ys_ref.shape[0] // 2
    tm = h_ref.shape[0] // n_sub
    cur = (i % 2) * n_sub
    nxt = n_sub - cur

    def copy_into(b):
        def copy(sorted_row, slot_row, rows):
            return pltpu.make_async_copy(y_ref.at[pl.ds(slot_row, rows)],
                                         ys_ref.at[b, pl.ds(sorted_row, rows)], sems.at[b])
        return copy

    @pl.when(i == 0)
    def _():
        ys_ref[...] = jnp.zeros_like(ys_ref)
        for t in range(n_sub):
            _start_run_copies(tbl_ref, t, copy_into(t))

    @pl.when(i + 1 < pl.num_programs(0))
    def _():
        for t in range(n_sub):
            _start_run_copies(tbl_next_ref, t, copy_into(nxt + t))

    for t in range(n_sub):
        _wait_rows(tbl_ref[t * SUBLANES + 4, 0], copy_into(cur + t))
    li = lax.broadcasted_iota(i32, (tm, SORT_ROWS), 1).astype(f32).astype(bf16)
    none = jnp.zeros((tm, SORT_ROWS), bf16)
    pad = jnp.zeros((LANES - SUBLANES, tm), f32)
    for t in range(n_sub):
        rows = slice(t * tm, (t + 1) * tm)
        pos_cols = jnp.transpose(jnp.concatenate([pos_ref[:, rows].astype(f32), pad], axis=0))
        w_cols = jnp.transpose(jnp.concatenate([wts_ref[:, rows], pad], axis=0))
        p1, p2 = pos_cols[:, 0:1], pos_cols[:, 1:2]
        w1b = w_cols[:, 0:1].astype(bf16) + none
        w2b = w_cols[:, 1:2].astype(bf16) + none
        sel = jnp.concatenate(
            [jnp.where(li == (p1 - r0).astype(bf16), w1b, none) + jnp.where(li == (p2 - r0).astype(bf16), w2b, none)
             for r0 in range(0, ys_ref.shape[1], SORT_ROWS)], axis=1)
        ya, yb = _unpack_rows(ys_ref[cur + t])
        ha = h_ref[rows, :HALF] + _dot(sel, ya.astype(bf16))
        hb = h_ref[rows, HALF:] + _dot(sel, yb.astype(bf16))
        ms = (jnp.sum(ha * ha, axis=-1, keepdims=True) + jnp.sum(hb * hb, axis=-1, keepdims=True)) * (1.0 / D_MODEL)
        inv = lax.rsqrt(ms + EPS)
        o_ref[rows, :HALF] = ha * inv * gfin_ref[:, :HALF]
        o_ref[rows, HALF:] = hb * inv * gfin_ref[:, HALF:]


def _combine(tbl, pos, wts, h, g_final, ybuf, tm, rmax):
    n = h.shape[0]
    group = _tile(n // tm, COMBINE_GROUP)
    n_steps = n // (tm * group)
    col = pl.BlockSpec((SUBLANES, group * tm), lambda i: (0, i))
    return pl.pallas_call(
        _combine_kernel,
        grid=(n_steps,),
        in_specs=[pl.BlockSpec((group * SUBLANES, LANES), lambda i: (i, 0), memory_space=pltpu.SMEM),
                  pl.BlockSpec((group * SUBLANES, LANES), lambda i: (jnp.minimum(i + 1, n_steps - 1), 0),
                               memory_space=pltpu.SMEM),
                  col, col,
                  pl.BlockSpec((group * tm, D_MODEL), lambda i: (i, 0)),
                  pl.BlockSpec(g_final.shape, lambda i: (0, 0)),
                  pl.BlockSpec(memory_space=pl.ANY)],
        out_specs=pl.BlockSpec((group * tm, D_MODEL), lambda i: (i, 0)),
        out_shape=jax.ShapeDtypeStruct((n, D_MODEL), f32),
        scratch_shapes=[pltpu.VMEM((2 * group, rmax, HALF), u32), pltpu.SemaphoreType.DMA((2 * group,))],
        compiler_params=_cparams(("arbitrary",)),
        name="combine",
    )(tbl, tbl, pos, wts, h, g_final, ybuf)


def _tile(n, pref):
    t = pref
    while n % t:
        t //= 2
    return t


def _prep_weights(g_mix, w_in, w_af2, b_af, w_ab2, b_ab, g_head, w_dw, b_dw, g_ln, b_ln, w_pw2, w_out,
                  g_ffn, w_rg, b_rg, w_re, b_re, w_e_gate, w_e_up, w_e_down, g_final):
    n_qkvo = 2 * GLA_KEY + 2 * GLA_VAL
    n_a = 2 * GATE_RANK
    w_in = w_in[0]
    zg = jnp.zeros((GATE_RANK, GLA_KEY), f32)
    wgate = jnp.concatenate([jnp.concatenate([w_af2[0], zg], axis=1), jnp.concatenate([zg, w_ab2[0]], axis=1)], axis=0)
    wdw = jnp.concatenate([w_dw[0], jnp.zeros((1, CONV_CH), f32)], axis=0)
    n_pad = 2 * SUBLANES - N_GROUPS
    wr = jnp.concatenate([w_re[0].T, w_rg[0].T, jnp.zeros((n_pad, D_MODEL), f32)], axis=0)
    br = jnp.concatenate([b_re[0], b_rg[0], jnp.zeros((n_pad,), f32)])[:, None]
    return dict(
        g_mix=g_mix[0][None, :],
        wqkvo=w_in[:, :n_qkvo].astype(bf16),
        wa=w_in[:, n_qkvo:n_qkvo + n_a].astype(bf16),
        wc=w_in[:, n_qkvo + n_a:].astype(bf16),
        wgate=wgate.astype(bf16),
        bgate=jnp.concatenate([b_af[0], b_ab[0]])[None, :],
        g_head=g_head[0][None, :],
        wdw=wdw, bdw=b_dw[0][None, :], gln=g_ln[0][None, :], bln=b_ln[0][None, :],
        wpw=w_pw2[0].astype(bf16),
        woa=w_out[0, :GLA_VAL].astype(bf16), wob=w_out[0, GLA_VAL:].astype(bf16),
        g_ffn=g_ffn[0][None, :], wr=wr, br=br,
        wg=w_e_gate[0], wu=w_e_up[0], wd=w_e_down[0],
        g_final=g_final[None, :],
    )


def _trunk(x3, w):
    n_seq, seq_len, _ = x3.shape
    n = n_seq * seq_len
    x = x3.reshape(n, D_MODEL)
    tm = _tile(n, MOE_TILE)
    q, k, v, og, la, glu = _inproj(x, w["g_mix"], w["wqkvo"], w["wa"], w["wc"], w["wgate"], w["bgate"],
                                   _tile(n, 1024))
    o_f, o_b = _gla(q, k, v, la, n_seq, seq_len)
    oc = _conv(glu, w["wdw"], w["bdw"], w["gln"], w["bln"], w["wpw"], seq_len, _tile(seq_len, CONV_TILE))
    h, up, eid, wts = _mix(x, o_f, o_b, og, oc, w["g_head"], w["woa"], w["wob"], w["g_ffn"], w["wr"], w["br"],
                            _tile(n, 1024))

    n_tiles = n // tm
    max_slots = n * TOP_K + n_tiles * N_EXPERTS * (RUN_ALIGN - 1)
    nb = -(-max_slots // MOE_BLOCK) + N_EXPERTS
    nbp = -(-nb // LANES) * LANES
    rmax = -(-(tm * TOP_K + N_EXPERTS * (RUN_ALIGN - 1)) // SORT_ROWS) * SORT_ROWS
    pos, tbl, blkinfo, tail = _rank(eid, MOE_BLOCK, nbp, tm)
    pos = pos[1]
    tbl = tbl[1].reshape(n_tiles * SUBLANES, LANES)
    xs = _dispatch(tbl, tail, pos, up, nb * MOE_BLOCK, tm, rmax)
    ybuf = _experts(blkinfo[0], blkinfo[1, :1], xs, w["wg"], w["wu"], w["wd"], MOE_BLOCK)
    y = _combine(tbl, pos, wts, h, w["g_final"], ybuf, tm, rmax)
    return y.reshape(x3.shape)


def kernel(x_prompt, x_sample, g_mix, w_in, w_af2, b_af, w_ab2, b_ab, g_head, w_dw, b_dw, g_ln, b_ln, w_pw2,
           w_out, g_ffn, w_rg, b_rg, w_re, b_re, w_e_gate, w_e_up, w_e_down, g_final):
    w = _prep_weights(g_mix, w_in, w_af2, b_af, w_ab2, b_ab, g_head, w_dw, b_dw, g_ln, b_ln, w_pw2, w_out,
                      g_ffn, w_rg, b_rg, w_re, b_re, w_e_gate, w_e_up, w_e_down, g_final)
    return (_trunk(x_prompt, w), _trunk(x_sample, w))
```

```python
import functools

import jax
import jax.numpy as jnp
from jax import lax
from jax.experimental import pallas as pl
from jax.experimental.pallas import tpu as pltpu

f32 = jnp.float32
bf16 = jnp.bfloat16
i32 = jnp.int32
u32 = jnp.uint32

D_MODEL = 1024
GLA_HEADS = 4
GLA_DK = 64
GLA_DV = 128
GLA_KEY = GLA_HEADS * GLA_DK
GLA_VAL = GLA_HEADS * GLA_DV
GATE_RANK = 16
GATE_NORMALIZER = 16
CONV_CH = 512
CONV_K = 31
N_GROUPS = 4
EXPERTS_PER_GROUP = 8
N_EXPERTS = N_GROUPS * EXPERTS_PER_GROUP
TOP_K = 2
D_EXPERT = 512
EPS = 1e-6

LANES = 128
SUBLANES = 8
GLA_CHUNK = 128
GLA_TILE = 1024
GLA_CUM_ROWS = 256
PAIR_DK = 2 * GLA_DK
PAIR_DV = 2 * GLA_DV
CONV_HALO = 16
CONV_ROWS = 32
CONV_TILE = 1024
HALF = D_MODEL // 2
MOE_TILE = 512
MOE_BLOCK = 512
RUN_ALIGN = SUBLANES
SORT_ROWS = 256
RANK_GROUP = 8
DISPATCH_GROUP = 2
COMBINE_GROUP = 2
BIG_CHUNK = 32
WAIT_ROWS = 256
VMEM_LIMIT = 56 * 1024 * 1024


def _cparams(sem, **kw):
    return pltpu.CompilerParams(dimension_semantics=sem, vmem_limit_bytes=VMEM_LIMIT, **kw)


def _dot(a, b):
    return jnp.dot(a, b, preferred_element_type=f32)


def _dot_nt(a, b):
    return lax.dot_general(a, b, (((1,), (1,)), ((), ())), preferred_element_type=f32)


def _dot_tn(a, b):
    return lax.dot_general(a, b, (((0,), (0,)), ((), ())), preferred_element_type=f32)


def _split_bf16(x):
    hi = x.astype(bf16)
    lo = (x - hi.astype(f32)).astype(bf16)
    return hi, lo


def _pack_rows(x):
    a = lax.bitcast_convert_type(x[:, :HALF].astype(bf16).astype(f32), u32)
    b = lax.bitcast_convert_type(x[:, HALF:].astype(bf16).astype(f32), u32)
    return a | (b >> 16)


def _pack_exact_rows(x):
    a = lax.bitcast_convert_type(x[:, :HALF], u32)
    b = lax.bitcast_convert_type(x[:, HALF:], u32)
    return a | (b >> 16)


def _unpack_rows(w):
    a = lax.bitcast_convert_type(w & jnp.uint32(0xFFFF0000), f32)
    b = lax.bitcast_convert_type(w << 16, f32)
    return a, b


def _rms(x):
    return x * lax.rsqrt(jnp.mean(x * x, axis=-1, keepdims=True) + EPS)


def _inproj_kernel(x_ref, g_ref, wqkvo_ref, wa_ref, wc_ref, wgate_ref, bgate_ref,
                   q_ref, k_ref, v_ref, og_ref, la_ref, glu_ref):
    u = (_rms(x_ref[...]) * g_ref[...]).astype(bf16)
    a = _dot(u, wa_ref[...]).astype(bf16)
    p = _dot(u, wqkvo_ref[...])
    q_ref[...] = (p[:, :GLA_KEY] * (GLA_DK ** -0.5)).astype(bf16)
    k_ref[...] = p[:, GLA_KEY:2 * GLA_KEY].astype(bf16)
    v_ref[...] = p[:, 2 * GLA_KEY:2 * GLA_KEY + GLA_VAL].astype(bf16)
    og_ref[...] = p[:, 2 * GLA_KEY + GLA_VAL:].astype(bf16)
    z = _dot(a, wgate_ref[...]) + bgate_ref[...]
    la_ref[...] = (jnp.minimum(z, 0.0) - jnp.log1p(jnp.exp(-jnp.abs(z)))) * (1.0 / GATE_NORMALIZER)
    c = _dot(u, wc_ref[...])
    glu_ref[...] = (c[:, :CONV_CH] * jax.nn.sigmoid(c[:, CONV_CH:])).astype(bf16)


def _inproj(x, g_mix, wqkvo, wa, wc, wgate, bgate, tm):
    n = x.shape[0]
    row = lambda w: pl.BlockSpec((tm, w), lambda i: (i, 0))
    full = lambda a: pl.BlockSpec(a.shape, lambda i: (0, 0))
    return pl.pallas_call(
        _inproj_kernel,
        grid=(n // tm,),
        in_specs=[row(D_MODEL), full(g_mix), full(wqkvo), full(wa), full(wc), full(wgate), full(bgate)],
        out_specs=[row(GLA_KEY), row(GLA_KEY), row(GLA_VAL), row(GLA_VAL), row(2 * GLA_KEY), row(CONV_CH)],
        out_shape=[jax.ShapeDtypeStruct((n, GLA_KEY), bf16), jax.ShapeDtypeStruct((n, GLA_KEY), bf16),
                   jax.ShapeDtypeStruct((n, GLA_VAL), bf16), jax.ShapeDtypeStruct((n, GLA_VAL), bf16),
                   jax.ShapeDtypeStruct((n, 2 * GLA_KEY), f32), jax.ShapeDtypeStruct((n, CONV_CH), bf16)],
        compiler_params=_cparams(("parallel",)),
        name="inproj",
    )(x, g_mix, wqkvo, wa, wc, wgate, bgate)


def _gla_unit(q_ref, k_ref, v_ref, cs, la, pair, reverse, masks):
    head0_k, head0_v, keep = masks
    tl = cs.shape[0]
    kl = slice(pair * PAIR_DK, (pair + 1) * PAIR_DK)
    vl = slice(pair * PAIR_DV, (pair + 1) * PAIR_DV)
    n_chunks = tl // GLA_CHUNK
    order = list(range(n_chunks - 1, -1, -1) if reverse else range(n_chunks))

    def stack_heads(x, head0):
        z = jnp.zeros_like(x)
        return jnp.concatenate([jnp.where(head0, x, z), jnp.where(head0, z, x)], axis=0)

    local = {}
    for j in order:
        rows = slice(j * GLA_CHUNK, (j + 1) * GLA_CHUNK)
        cj = cs[rows, kl]
        q = q_ref[rows, kl].astype(f32)
        k = k_ref[rows, kl].astype(f32)
        c_last = cj[GLA_CHUNK - 1:GLA_CHUNK, :]
        mid = 0.5 * c_last
        rel = (mid - (cj - la[rows, kl])) if reverse else (cj - mid)
        qd = q * jnp.exp(rel)
        kd = k * jnp.exp(-rel)
        edge = jnp.exp(mid)
        qi = qd * edge
        ku = kd * edge
        vbd = stack_heads(v_ref[rows, vl], head0_v)
        sc = _dot_nt(qd.astype(bf16), stack_heads(kd.astype(bf16), head0_k))
        p = jnp.where(keep, sc, 0.0).astype(bf16)
        kv = _dot_tn(stack_heads(ku.astype(bf16), head0_k), vbd)
        local[j] = (p, vbd, qi.astype(bf16), kv, c_last)
    return order, vl, local


def _gla_finish(o_ref, st_ref, order, vl, local):
    row_id = lax.broadcasted_iota(i32, (PAIR_DK, PAIR_DK), 0)
    c_rows = jnp.zeros((PAIR_DK, PAIR_DK), f32)
    for j in order:
        c_rows = jnp.where(row_id == j, local[j][4], c_rows)
    decay_cols = jnp.exp(jnp.transpose(c_rows))
    st = st_ref[...]
    st_before = {}
    for j in order:
        st_before[j] = st.astype(bf16)
        st = decay_cols[:, j:j + 1] * st + local[j][3]
    st_ref[...] = st
    for j in order:
        p, vbd, qi, _, _ = local[j]
        o = _dot(p, vbd) + _dot(qi, st_before[j])
        o_ref[j * GLA_CHUNK:(j + 1) * GLA_CHUNK, vl] = o.astype(o_ref.dtype)


def _gla_kernel(qf, kf, vf, laf, qb, kb, vb, lab, of, ob, st_ref, tri_ref):
    tl = qf.shape[0]

    @pl.when(pl.program_id(1) == 0)
    def _():
        st_ref[...] = jnp.zeros_like(st_ref)
        r = lax.broadcasted_iota(i32, tri_ref.shape, 0)
        c = lax.broadcasted_iota(i32, tri_ref.shape, 1)
        tri_ref[...] = jnp.where((r // GLA_CHUNK == c // GLA_CHUNK) & (c <= r), 1.0, 0.0).astype(bf16)

    tri = tri_ref[...]
    cum_rows = tri_ref.shape[0]

    def chunk_cumsum(x):
        blocks = [_dot(tri, x[r0:r0 + cum_rows]) for r0 in range(0, tl, cum_rows)]
        return blocks[0] if len(blocks) == 1 else jnp.concatenate(blocks, axis=0)

    head0_k = lax.broadcasted_iota(i32, (GLA_CHUNK, PAIR_DK), 1) < GLA_DK
    head0_v = lax.broadcasted_iota(i32, (GLA_CHUNK, PAIR_DV), 1) < GLA_DV
    t_idx = lax.broadcasted_iota(i32, (GLA_CHUNK, 2 * GLA_CHUNK), 0)
    s_idx = lax.broadcasted_iota(i32, (GLA_CHUNK, 2 * GLA_CHUNK), 1) % GLA_CHUNK
    units = []
    for d, (q_ref, k_ref, v_ref, la_ref, o_ref) in enumerate(((qf, kf, vf, laf, of), (qb, kb, vb, lab, ob))):
        reverse = d == 1
        la = la_ref[...]
        la_hi, la_lo = _split_bf16(la)
        cs = chunk_cumsum(la_hi) + chunk_cumsum(la_lo)
        keep = (s_idx > t_idx) if reverse else (s_idx <= t_idx)
        for pair in range(GLA_HEADS // 2):
            units.append((o_ref, st_ref.at[d, pair]) + _gla_unit(q_ref, k_ref, v_ref, cs, la, pair, reverse,
                                                                  (head0_k, head0_v, keep)))
    for unit in units:
        _gla_finish(*unit)


def _gla(q, k, v, la, n_seq, seq_len):
    n = q.shape[0]
    tl = _tile(seq_len, GLA_TILE)
    assert tl % GLA_CUM_ROWS == 0 and GLA_CUM_ROWS % GLA_CHUNK == 0
    nblk = seq_len // tl
    fwd = lambda b, i: (b * nblk + i, 0)
    bwd = lambda b, i: (b * nblk + nblk - 1 - i, 0)
    bwd_la = lambda b, i: (b * nblk + nblk - 1 - i, 1)
    kq = lambda im: pl.BlockSpec((tl, GLA_KEY), im)
    vv = lambda im: pl.BlockSpec((tl, GLA_VAL), im)
    return pl.pallas_call(
        _gla_kernel,
        grid=(n_seq, nblk),
        in_specs=[kq(fwd), kq(fwd), vv(fwd), kq(fwd), kq(bwd), kq(bwd), vv(bwd), kq(bwd_la)],
        out_specs=[vv(fwd), vv(bwd)],
        out_shape=[jax.ShapeDtypeStruct((n, GLA_VAL), bf16), jax.ShapeDtypeStruct((n, GLA_VAL), bf16)],
        scratch_shapes=[pltpu.VMEM((2, GLA_HEADS // 2, PAIR_DK, PAIR_DV), f32), pltpu.VMEM((GLA_CUM_ROWS, GLA_CUM_ROWS), bf16)],
        compiler_params=_cparams(("parallel", "arbitrary")),
        name="gla",
    )(q, k, v, la, q, k, v, la)


def _conv_kernel(prev_ref, cur_ref, next_ref, wdw_ref, bdw_ref, gln_ref, bln_ref, wpw_ref,
                 o_ref, ext_ref, y_ref, *, tiles_per_seq):
    tl = cur_ref.shape[0]
    pos = pl.program_id(0) % tiles_per_seq
    prev = jnp.where(pos == 0, 0.0, prev_ref[...].astype(f32))
    nxt = jnp.where(pos == tiles_per_seq - 1, 0.0, next_ref[...].astype(f32))
    cur = cur_ref[...].astype(f32)
    n_slabs = CONV_CH // LANES
    for c in range(n_slabs):
        lanes = slice(c * LANES, (c + 1) * LANES)
        ext_ref[c, 0:CONV_HALO, :] = prev[:, lanes]
        ext_ref[c, CONV_HALO:CONV_HALO + tl, :] = cur[:, lanes]
        ext_ref[c, CONV_HALO + tl:, :] = nxt[:, lanes]
    off = CONV_HALO - CONV_K // 2

    def body(rb, carry):
        base = pl.multiple_of(rb * CONV_ROWS, CONV_ROWS)
        for c in range(n_slabs):
            lanes = slice(c * LANES, (c + 1) * LANES)
            a = jnp.zeros((CONV_ROWS, LANES), f32) + bdw_ref[:, lanes]
            for j in range(CONV_K):
                taps = ext_ref.at[c, pl.ds(off + j, tl)]
                a = a + taps[pl.ds(base, CONV_ROWS), :] * wdw_ref[j:j + 1, lanes]
            y_ref[pl.ds(base, CONV_ROWS), lanes] = a
        return carry

    lax.fori_loop(0, tl // CONV_ROWS, body, 0)
    acc = y_ref[...]
    mu = jnp.mean(acc, axis=-1, keepdims=True)
    xc = acc - mu
    yn = xc * lax.rsqrt(jnp.mean(xc * xc, axis=-1, keepdims=True) + EPS) * gln_ref[...] + bln_ref[...]
    o_ref[...] = _dot((yn * jax.nn.sigmoid(yn)).astype(bf16), wpw_ref[...]).astype(bf16)


def _conv(glu, wdw, bdw, gln, bln, wpw, seq_len, tl):
    n = glu.shape[0]
    hb = tl // CONV_HALO
    n_halo = n // CONV_HALO
    full = lambda a: pl.BlockSpec(a.shape, lambda i: (0, 0))
    return pl.pallas_call(
        functools.partial(_conv_kernel, tiles_per_seq=seq_len // tl),
        grid=(n // tl,),
        in_specs=[pl.BlockSpec((CONV_HALO, CONV_CH), lambda i: (jnp.maximum(i * hb - 1, 0), 0)),
                  pl.BlockSpec((tl, CONV_CH), lambda i: (i, 0)),
                  pl.BlockSpec((CONV_HALO, CONV_CH), lambda i: (jnp.minimum((i + 1) * hb, n_halo - 1), 0)),
                  full(wdw), full(bdw), full(gln), full(bln), full(wpw)],
        out_specs=pl.BlockSpec((tl, CONV_CH), lambda i: (i, 0)),
        out_shape=jax.ShapeDtypeStruct((n, CONV_CH), bf16),
        scratch_shapes=[pltpu.VMEM((CONV_CH // LANES, tl + 2 * CONV_HALO, LANES), f32),
                        pltpu.VMEM((tl, CONV_CH), f32)],
        compiler_params=_cparams(("parallel",)),
        name="conv",
    )(glu, glu, glu, wdw, bdw, gln, bln, wpw)


def _mix_kernel(x_ref, of_ref, ob_ref, og_ref, oc_ref, gh_ref, woa_ref, wob_ref, gf_ref, wr_ref, br_ref,
                h_ref, up_ref, eid_ref, wts_ref):
    tm = x_ref.shape[0]
    o = of_ref[...].astype(f32) + ob_ref[...].astype(f32)
    o = jnp.concatenate([_rms(o[:, h * GLA_DV:(h + 1) * GLA_DV]) for h in range(GLA_HEADS)], axis=1)
    og = og_ref[...].astype(f32)
    o = o * gh_ref[...] * (og * jax.nn.sigmoid(og))
    h = x_ref[...] + _dot(o.astype(bf16), woa_ref[...]) + _dot(oc_ref[...], wob_ref[...])
    h_ref[...] = h
    u = _rms(h) * gf_ref[...]
    u_hi, u_lo = _split_bf16(u)
    up_ref[...] = u_hi

    w_hi, w_lo = _split_bf16(wr_ref[...])
    n_r = w_hi.shape[0]
    both = _dot_nt(jnp.concatenate([w_hi, w_lo], axis=0), u_hi)
    logits = both[:n_r] + both[n_r:] + _dot_nt(w_hi, u_lo) + br_ref[...]
    le = logits[0:N_EXPERTS]
    lg = logits[N_EXPERTS:N_EXPERTS + N_GROUPS]
    gmax = jnp.max(lg, axis=0, keepdims=True)
    gi = lax.broadcasted_iota(i32, lg.shape, 0)
    gsel = jnp.min(jnp.where(lg == gmax, gi, N_GROUPS), axis=0, keepdims=True)
    gate = 1.0 / jnp.sum(jnp.exp(lg - gmax), axis=0, keepdims=True)
    ri = lax.broadcasted_iota(i32, le.shape, 0)
    lm = jnp.where(ri // EXPERTS_PER_GROUP == gsel, le, -jnp.inf)
    m1 = jnp.max(lm, axis=0, keepdims=True)
    i1 = jnp.min(jnp.where(lm == m1, ri, N_EXPERTS), axis=0, keepdims=True)
    lm2 = jnp.where(ri == i1, -jnp.inf, lm)
    m2 = jnp.max(lm2, axis=0, keepdims=True)
    i2 = jnp.min(jnp.where(lm2 == m2, ri, N_EXPERTS), axis=0, keepdims=True)
    t = jnp.exp(m2 - m1)
    den = 1.0 / (1.0 + t)
    r8 = lax.broadcasted_iota(i32, (SUBLANES, tm), 0)
    eid_ref[...] = jnp.where(r8 == 0, i1, jnp.where(r8 == 1, i2, 0))
    wts_ref[...] = jnp.where(r8 == 0, gate * den, jnp.where(r8 == 1, gate * t * den, 0.0))


def _mix(x, o_f, o_b, og, oc, g_head, woa, wob, g_ffn, wr, br, tm):
    n = x.shape[0]
    row = lambda w: pl.BlockSpec((tm, w), lambda i: (i, 0))
    col = pl.BlockSpec((SUBLANES, tm), lambda i: (0, i))
    full = lambda a: pl.BlockSpec(a.shape, lambda i: (0, 0))
    return pl.pallas_call(
        _mix_kernel,
        grid=(n // tm,),
        in_specs=[row(D_MODEL), row(GLA_VAL), row(GLA_VAL), row(GLA_VAL), row(CONV_CH),
                  full(g_head), full(woa), full(wob), full(g_ffn), full(wr), full(br)],
        out_specs=[row(D_MODEL), row(D_MODEL), col, col],
        out_shape=[jax.ShapeDtypeStruct((n, D_MODEL), f32), jax.ShapeDtypeStruct((n, D_MODEL), bf16),
                   jax.ShapeDtypeStruct((SUBLANES, n), i32), jax.ShapeDtypeStruct((SUBLANES, n), f32)],
        compiler_params=_cparams(("parallel",)),
        name="mix",
    )(x, o_f, o_b, og, oc, g_head, woa, wob, g_ffn, wr, br)


def _rank_kernel(eid_ref, pos_ref, tbl_ref, blk_ref, tail_ref, cnt_ref, base_ref, end_ref, *, blk, nbp, tm):
    ps = pl.program_id(0)
    i = pl.program_id(1)
    rr = lax.broadcasted_iota(i32, (N_EXPERTS, LANES), 0)

    def cumsum_experts(x):
        for s in (1, 2, 4, 8, 16):
            x = x + jnp.where(rr >= s, pltpu.roll(x, s, axis=0), 0.0)
        return x

    @pl.when((ps == 0) & (i == 0))
    def _():
        cnt_ref[...] = jnp.zeros_like(cnt_ref)

    @pl.when(ps == 0)
    def _():
        pos_ref[...] = jnp.zeros_like(pos_ref)
        tbl_ref[...] = jnp.zeros_like(tbl_ref)

    @pl.when((ps == 1) & (i == 0))
    def _():
        cnt = cnt_ref[...]
        pc = jnp.floor((cnt + (blk - 1)) * (1.0 / blk)) * blk
        inc = cumsum_experts(pc)
        base_ref[...] = inc - pc
        end_ref[...] = inc
        pend = jnp.concatenate([inc] * (nbp // LANES), axis=1)
        jl = lax.broadcasted_iota(i32, (N_EXPERTS, nbp), 1).astype(f32) * blk
        be = jnp.minimum(jnp.sum(jnp.where(pend <= jl, 1.0, 0.0), axis=0, keepdims=True), N_EXPERTS - 1.0)
        nused = jnp.concatenate([inc[N_EXPERTS - 1:N_EXPERTS, :]] * (nbp // LANES), axis=1) * (1.0 / blk)
        r8 = lax.broadcasted_iota(i32, (SUBLANES, nbp), 0)
        blk_ref[...] = jnp.where(r8 == 0, be, jnp.where(r8 == 1, nused, 0.0)).astype(i32)

    ri = lax.broadcasted_iota(i32, (N_EXPERTS, tm), 0)
    for t in range(eid_ref.shape[1] // tm):
        lanes = slice(t * tm, (t + 1) * tm)
        oh1 = ri == eid_ref[0:1, lanes]
        oh2 = ri == eid_ref[1:2, lanes]
        ohf = jnp.where(oh1 | oh2, 1.0, 0.0)
        tile_cnt = jnp.sum(ohf, axis=1, keepdims=True)
        run_len = (jnp.floor((tile_cnt + (RUN_ALIGN - 1)) * (1.0 / RUN_ALIGN)) * RUN_ALIGN
                   + jnp.zeros((N_EXPERTS, LANES), f32))

        @pl.when(ps == 0)
        def _():
            cnt_ref[...] += run_len

        @pl.when(ps == 1)
        def _():
            a = lax.broadcasted_iota(i32, (tm, tm), 0)
            b = lax.broadcasted_iota(i32, (tm, tm), 1)
            upper = jnp.where(a < b, 1.0, 0.0).astype(bf16)
            before = _dot(ohf.astype(bf16), upper)
            run_start = cumsum_experts(run_len) - run_len
            row = run_start[:, 0:1] + before
            p1 = jnp.sum(jnp.where(oh1, row, 0.0), axis=0, keepdims=True)
            p2 = jnp.sum(jnp.where(oh2, row, 0.0), axis=0, keepdims=True)
            r8 = lax.broadcasted_iota(i32, (SUBLANES, tm), 0)
            pos_ref[0, :, lanes] = jnp.where(r8 == 0, p1, jnp.where(r8 == 1, p2, 0.0)).astype(i32)
            diag = lax.broadcasted_iota(i32, (N_EXPERTS, LANES), 1) == rr
            on_lanes = lambda x: jnp.sum(jnp.where(diag, x, 0.0), axis=0, keepdims=True)
            big = jnp.floor(run_len * (1.0 / BIG_CHUNK))
            small = (run_len - big * BIG_CHUNK) * (1.0 / RUN_ALIGN)
            rows = (on_lanes(run_start), on_lanes(base_ref[...]), on_lanes(big), on_lanes(small),
                    jnp.sum(run_len, axis=0, keepdims=True))
            t8 = lax.broadcasted_iota(i32, (SUBLANES, LANES), 0)
            tbl = jnp.zeros((SUBLANES, LANES), f32)
            for k, x in enumerate(rows):
                tbl = jnp.where(t8 == k, x, tbl)
            tbl_ref[0, t] = tbl.astype(i32)
            base_ref[...] += run_len

    @pl.when((ps == 1) & (i == pl.num_programs(1) - 1))
    def _():
        diag = lax.broadcasted_iota(i32, (N_EXPERTS, LANES), 1) == rr
        on_lanes = lambda x: jnp.sum(jnp.where(diag, x, 0.0), axis=0, keepdims=True)
        first = base_ref[...]
        count = end_ref[...] - first
        rows = (on_lanes(first), on_lanes(count), jnp.sum(count, axis=0, keepdims=True),
                end_ref[N_EXPERTS - 1:N_EXPERTS, :] * (1.0 / blk))
        t8 = lax.broadcasted_iota(i32, (SUBLANES, LANES), 0)
        tail = jnp.zeros((SUBLANES, LANES), f32)
        for k, x in enumerate(rows):
            tail = jnp.where(t8 == k, x, tail)
        tail_ref[...] = tail.astype(i32)


def _rank(eid, blk, nbp, tm):
    n = eid.shape[1]
    group = _tile(n // tm, RANK_GROUP)
    return pl.pallas_call(
        functools.partial(_rank_kernel, blk=blk, nbp=nbp, tm=tm),
        grid=(2, n // (tm * group)),
        in_specs=[pl.BlockSpec((SUBLANES, tm * group), lambda p, i: (0, i))],
        out_specs=[pl.BlockSpec((1, SUBLANES, tm * group), lambda p, i: (p, 0, i)),
                   pl.BlockSpec((1, group, SUBLANES, LANES), lambda p, i: (p, i, 0, 0)),
                   pl.BlockSpec((SUBLANES, nbp), lambda p, i: (0, 0)),
                   pl.BlockSpec((SUBLANES, LANES), lambda p, i: (0, 0))],
        out_shape=[jax.ShapeDtypeStruct((2, SUBLANES, n), i32),
                   jax.ShapeDtypeStruct((2, n // tm, SUBLANES, LANES), i32),
                   jax.ShapeDtypeStruct((SUBLANES, nbp), i32),
                   jax.ShapeDtypeStruct((SUBLANES, LANES), i32)],
        scratch_shapes=[pltpu.VMEM((N_EXPERTS, LANES), f32)] * 3,
        compiler_params=_cparams(("arbitrary", "arbitrary")),
        name="rank",
    )(eid)


def _start_run_copies(tbl_ref, t, copy):
    base = t * SUBLANES

    def per_expert(e, carry):
        sorted0 = tbl_ref[base, e]
        slot0 = tbl_ref[base + 1, e]
        n_big = tbl_ref[base + 2, e]

        def pieces(rows, first):
            def body(c, carry2):
                off = first + c * rows
                copy(pl.multiple_of(sorted0 + off, RUN_ALIGN), pl.multiple_of(slot0 + off, RUN_ALIGN), rows).start()
                return carry2
            return body

        lax.fori_loop(0, n_big, pieces(BIG_CHUNK, 0), 0)
        lax.fori_loop(0, tbl_ref[base + 3, e], pieces(RUN_ALIGN, n_big * BIG_CHUNK), 0)
        return carry

    lax.fori_loop(0, N_EXPERTS, per_expert, 0)


def _wait_rows(total_rows, copy):
    def waits(rows):
        def body(c, carry):
            copy(0, 0, rows).wait()
            return carry
        return body

    n_wide = total_rows // WAIT_ROWS
    lax.fori_loop(0, n_wide, waits(WAIT_ROWS), 0)
    lax.fori_loop(0, (total_rows - n_wide * WAIT_ROWS) // RUN_ALIGN, waits(RUN_ALIGN), 0)


def _dispatch_kernel(tbl_ref, tail_ref, pos_ref, u_ref, xs_ref, sorted_ref, zero_ref, sems, pending_ref):
    i = pl.program_id(0)
    n_buf = sorted_ref.shape[0]
    n_sub = n_buf // 2
    tm = u_ref.shape[0] // n_sub
    cur = (i % 2) * n_sub

    def copy_from(b):
        def copy(sorted_row, slot_row, rows):
            return pltpu.make_async_copy(sorted_ref.at[b, pl.ds(sorted_row, rows)],
                                         xs_ref.at[pl.ds(slot_row, rows)], sems.at[b])
        return copy

    @pl.when(i == 0)
    def _():
        for b in range(n_buf):
            pending_ref[b] = 0

    ri = lax.broadcasted_iota(i32, (SORT_ROWS, tm), 0).astype(f32).astype(bf16)
    one = jnp.ones((SORT_ROWS, tm), bf16)
    for t in range(n_sub):
        _wait_rows(pending_ref[cur + t], copy_from(cur + t))
    for t in range(n_sub):
        p1 = pos_ref[0:1, t * tm:(t + 1) * tm]
        p2 = pos_ref[1:2, t * tm:(t + 1) * tm]
        u = u_ref[t * tm:(t + 1) * tm, :]
        for r0 in range(0, sorted_ref.shape[1], SORT_ROWS):
            q1 = (p1 - r0).astype(f32).astype(bf16)
            q2 = (p2 - r0).astype(f32).astype(bf16)
            sel = jnp.where((ri == q1) | (ri == q2), one, jnp.zeros_like(one))
            sorted_ref[cur + t, r0:r0 + SORT_ROWS, :] = _pack_exact_rows(_dot(sel, u))
    for t in range(n_sub):
        _start_run_copies(tbl_ref, t, copy_from(cur + t))
        pending_ref[cur + t] = tbl_ref[t * SUBLANES + 4, 0]

    @pl.when(i == pl.num_programs(0) - 1)
    def _():
        for b in range(n_buf):
            _wait_rows(pending_ref[b], copy_from(b))
        blk = zero_ref.shape[0]
        n_blocks = xs_ref.shape[0] // blk
        zero_ref[...] = jnp.zeros_like(zero_ref)

        def zero_copy(unused_row, slot_row, rows):
            return pltpu.make_async_copy(zero_ref.at[pl.ds(0, rows)], xs_ref.at[pl.ds(slot_row, rows)], sems.at[n_buf])

        def per_expert(e, carry):
            first = tail_ref[0, e]
            n_big = tail_ref[1, e] // BIG_CHUNK

            def pieces(rows, start):
                def body(c, carry2):
                    zero_copy(0, pl.multiple_of(first + start + c * rows, RUN_ALIGN), rows).start()
                    return carry2
                return body

            lax.fori_loop(0, n_big, pieces(BIG_CHUNK, 0), 0)
            lax.fori_loop(0, (tail_ref[1, e] - n_big * BIG_CHUNK) // RUN_ALIGN, pieces(RUN_ALIGN, n_big * BIG_CHUNK), 0)
            return carry

        lax.fori_loop(0, N_EXPERTS, per_expert, 0)
        n_used = tail_ref[3, 0]

        def per_block(j, carry):
            zero_copy(0, pl.multiple_of(j * blk, blk), blk).start()
            return carry

        lax.fori_loop(n_used, n_blocks, per_block, 0)
        _wait_rows(tail_ref[2, 0] + (n_blocks - n_used) * blk, zero_copy)


def _dispatch(tbl, tail, pos, up, n_slots, tm, rmax):
    n = up.shape[0]
    group = _tile(n // tm, DISPATCH_GROUP)
    return pl.pallas_call(
        _dispatch_kernel,
        grid=(n // (tm * group),),
        in_specs=[pl.BlockSpec((group * SUBLANES, LANES), lambda i: (i, 0), memory_space=pltpu.SMEM),
                  pl.BlockSpec((SUBLANES, LANES), lambda i: (0, 0), memory_space=pltpu.SMEM),
                  pl.BlockSpec((SUBLANES, group * tm), lambda i: (0, i)),
                  pl.BlockSpec((group * tm, D_MODEL), lambda i: (i, 0))],
        out_specs=pl.BlockSpec(memory_space=pl.ANY),
        out_shape=jax.ShapeDtypeStruct((n_slots, HALF), u32),
        scratch_shapes=[pltpu.VMEM((2 * group, rmax, HALF), u32), pltpu.VMEM((MOE_BLOCK, HALF), u32),
                        pltpu.SemaphoreType.DMA((2 * group + 1,)), pltpu.SMEM((2 * group,), i32)],
        compiler_params=_cparams(("arbitrary",)),
        name="dispatch",
    )(tbl, tail, pos, up)


def _expert_kernel(be_ref, nu_ref, xs_ref, wg_hbm, wu_hbm, wd_hbm, y_ref,
                   wg_f32, wu_f32, wd_f32, wg_bf, wu_bf, wd_bf, sems, slot_ref):
    j = pl.program_id(0)
    n_steps = pl.num_programs(0)
    expert = be_ref[j]

    def fetch(e, s):
        return [pltpu.make_async_copy(hbm.at[e], stage.at[s], sems.at[s, k])
                for k, (hbm, stage) in enumerate(((wg_hbm, wg_f32), (wu_hbm, wu_f32), (wd_hbm, wd_f32)))]

    @pl.when(j == 0)
    def _():
        slot_ref[0] = 0
        for c in fetch(expert, 0):
            c.start()

    @pl.when((j == 0) | (expert != be_ref[jnp.maximum(j - 1, 0)]))
    def _():
        s = slot_ref[0]
        for c in fetch(expert, s):
            c.wait()
        wg_bf[...] = wg_f32[s].astype(bf16)
        wu_bf[...] = wu_f32[s].astype(bf16)
        wd_bf[...] = wd_f32[s].astype(bf16)
        nxt = lax.while_loop(lambda k: (k < n_steps) & (be_ref[jnp.minimum(k, n_steps - 1)] == expert),
                             lambda k: k + 1, j + 1)

        @pl.when(nxt < n_steps)
        def _():
            for c in fetch(be_ref[jnp.minimum(nxt, n_steps - 1)], 1 - s):
                c.start()

        slot_ref[0] = 1 - s

    @pl.when(j < nu_ref[0])
    def _():
        xa, xb = _unpack_rows(xs_ref[...])
        x = jnp.concatenate([xa.astype(bf16), xb.astype(bf16)], axis=1)
        g = _dot(x, wg_bf[...])
        u = _dot(x, wu_bf[...])
        hb = (g * jax.nn.sigmoid(g) * u).astype(bf16)
        y_ref[...] = _pack_rows(_dot(hb, wd_bf[...]))

    @pl.when(j >= nu_ref[0])
    def _():
        y_ref[...] = jnp.zeros_like(y_ref)


def _experts(block_e, nused, xs, wg, wu, wd, blk):
    nb = xs.shape[0] // blk
    used = lambda j, be, nu: (jnp.minimum(j, nu[0] - 1), 0)
    hbm = pl.BlockSpec(memory_space=pl.ANY)
    return pl.pallas_call(
        _expert_kernel,
        grid_spec=pltpu.PrefetchScalarGridSpec(
            num_scalar_prefetch=2,
            grid=(nb,),
            in_specs=[pl.BlockSpec((blk, HALF), used), hbm, hbm, hbm],
            out_specs=pl.BlockSpec((blk, HALF), lambda j, be, nu: (j, 0)),
            scratch_shapes=[pltpu.VMEM((2, D_MODEL, D_EXPERT), f32), pltpu.VMEM((2, D_MODEL, D_EXPERT), f32),
                            pltpu.VMEM((2, D_EXPERT, D_MODEL), f32),
                            pltpu.VMEM((D_MODEL, D_EXPERT), bf16), pltpu.VMEM((D_MODEL, D_EXPERT), bf16),
                            pltpu.VMEM((D_EXPERT, D_MODEL), bf16),
                            pltpu.SemaphoreType.DMA((2, 3)), pltpu.SMEM((1,), i32)],
        ),
        out_shape=jax.ShapeDtypeStruct(xs.shape, u32),
        compiler_params=_cparams(("arbitrary",)),
        name="experts",
    )(block_e, nused, xs, wg, wu, wd)


def _combine_kernel(tbl_ref, tbl_next_ref, pos_ref, wts_ref, h_ref, gfin_ref, y_ref, o_ref, ys_ref, sems):
    i = pl.program_id(0)
    n_sub = ys_ref.shape[0] // 2
    tm = h_ref.shape[0] // n_sub
    cur = (i % 2) * n_sub
    nxt = n_sub - cur

    def copy_into(b):
        def copy(sorted_row, slot_row, rows):
            return pltpu.make_async_copy(y_ref.at[pl.ds(slot_row, rows)],
                                         ys_ref.at[b, pl.ds(sorted_row, rows)], sems.at[b])
        return copy

    @pl.when(i == 0)
    def _():
        ys_ref[...] = jnp.zeros_like(ys_ref)
        for t in range(n_sub):
            _start_run_copies(tbl_ref, t, copy_into(t))

    @pl.when(i + 1 < pl.num_programs(0))
    def _():
        for t in range(n_sub):
            _start_run_copies(tbl_next_ref, t, copy_into(nxt + t))

    for t in range(n_sub):
        _wait_rows(tbl_ref[t * SUBLANES + 4, 0], copy_into(cur + t))
    li = lax.broadcasted_iota(i32, (tm, SORT_ROWS), 1).astype(f32).astype(bf16)
    none = jnp.zeros((tm, SORT_ROWS), bf16)
    pad = jnp.zeros((LANES - SUBLANES, tm), f32)
    for t in range(n_sub):
        rows = slice(t * tm, (t + 1) * tm)
        pos_cols = jnp.transpose(jnp.concatenate([pos_ref[:, rows].astype(f32), pad], axis=0))
        w_cols = jnp.transpose(jnp.concatenate([wts_ref[:, rows], pad], axis=0))
        p1, p2 = pos_cols[:, 0:1], pos_cols[:, 1:2]
        w1b = w_cols[:, 0:1].astype(bf16) + none
        w2b = w_cols[:, 1:2].astype(bf16) + none
        sel = jnp.concatenate(
            [jnp.where(li == (p1 - r0).astype(bf16), w1b, none) + jnp.where(li == (p2 - r0).astype(bf16), w2b, none)
             for r0 in range(0, ys_ref.shape[1], SORT_ROWS)], axis=1)
        ya, yb = _unpack_rows(ys_ref[cur + t])
        ha = h_ref[rows, :HALF] + _dot(sel, ya.astype(bf16))
        hb = h_ref[rows, HALF:] + _dot(sel, yb.astype(bf16))
        ms = (jnp.sum(ha * ha, axis=-1, keepdims=True) + jnp.sum(hb * hb, axis=-1, keepdims=True)) * (1.0 / D_MODEL)
        inv = lax.rsqrt(ms + EPS)
        o_ref[rows, :HALF] = ha * inv * gfin_ref[:, :HALF]
        o_ref[rows, HALF:] = hb * inv * gfin_ref[:, HALF:]


def _combine(tbl, pos, wts, h, g_final, ybuf, tm, rmax):
    n = h.shape[0]
    group = _tile(n // tm, COMBINE_GROUP)
    n_steps = n // (tm * group)
    col = pl.BlockSpec((SUBLANES, group * tm), lambda i: (0, i))
    return pl.pallas_call(
        _combine_kernel,
        grid=(n_steps,),
        in_specs=[pl.BlockSpec((group * SUBLANES, LANES), lambda i: (i, 0), memory_space=pltpu.SMEM),
                  pl.BlockSpec((group * SUBLANES, LANES), lambda i: (jnp.minimum(i + 1, n_steps - 1), 0),
                               memory_space=pltpu.SMEM),
                  col, col,
                  pl.BlockSpec((group * tm, D_MODEL), lambda i: (i, 0)),
                  pl.BlockSpec(g_final.shape, lambda i: (0, 0)),
                  pl.BlockSpec(memory_space=pl.ANY)],
        out_specs=pl.BlockSpec((group * tm, D_MODEL), lambda i: (i, 0)),
        out_shape=jax.ShapeDtypeStruct((n, D_MODEL), f32),
        scratch_shapes=[pltpu.VMEM((2 * group, rmax, HALF), u32), pltpu.SemaphoreType.DMA((2 * group,))],
        compiler_params=_cparams(("arbitrary",)),
        name="combine",
    )(tbl, tbl, pos, wts, h, g_final, ybuf)


def _tile(n, pref):
    t = pref
    while n % t:
        t //= 2
    return t


def _prep_weights(g_mix, w_in, w_af2, b_af, w_ab2, b_ab, g_head, w_dw, b_dw, g_ln, b_ln, w_pw2, w_out,
                  g_ffn, w_rg, b_rg, w_re, b_re, w_e_gate, w_e_up, w_e_down, g_final):
    n_qkvo = 2 * GLA_KEY + 2 * GLA_VAL
    n_a = 2 * GATE_RANK
    w_in = w_in[0]
    zg = jnp.zeros((GATE_RANK, GLA_KEY), f32)
    wgate = jnp.concatenate([jnp.concatenate([w_af2[0], zg], axis=1), jnp.concatenate([zg, w_ab2[0]], axis=1)], axis=0)
    wdw = jnp.concatenate([w_dw[0], jnp.zeros((1, CONV_CH), f32)], axis=0)
    n_pad = 2 * SUBLANES - N_GROUPS
    wr = jnp.concatenate([w_re[0].T, w_rg[0].T, jnp.zeros((n_pad, D_MODEL), f32)], axis=0)
    br = jnp.concatenate([b_re[0], b_rg[0], jnp.zeros((n_pad,), f32)])[:, None]
    return dict(
        g_mix=g_mix[0][None, :],
        wqkvo=w_in[:, :n_qkvo].astype(bf16),
        wa=w_in[:, n_qkvo:n_qkvo + n_a].astype(bf16),
        wc=w_in[:, n_qkvo + n_a:].astype(bf16),
        wgate=wgate.astype(bf16),
        bgate=jnp.concatenate([b_af[0], b_ab[0]])[None, :],
        g_head=g_head[0][None, :],
        wdw=wdw, bdw=b_dw[0][None, :], gln=g_ln[0][None, :], bln=b_ln[0][None, :],
        wpw=w_pw2[0].astype(bf16),
        woa=w_out[0, :GLA_VAL].astype(bf16), wob=w_out[0, GLA_VAL:].astype(bf16),
        g_ffn=g_ffn[0][None, :], wr=wr, br=br,
        wg=w_e_gate[0], wu=w_e_up[0], wd=w_e_down[0],
        g_final=g_final[None, :],
    )


def _trunk(x3, w):
    n_seq, seq_len, _ = x3.shape
    n = n_seq * seq_len
    x = x3.reshape(n, D_MODEL)
    tm = _tile(n, MOE_TILE)
    q, k, v, og, la, glu = _inproj(x, w["g_mix"], w["wqkvo"], w["wa"], w["wc"], w["wgate"], w["bgate"],
                                   _tile(n, 1024))
    o_f, o_b = _gla(q, k, v, la, n_seq, seq_len)
    oc = _conv(glu, w["wdw"], w["bdw"], w["gln"], w["bln"], w["wpw"], seq_len, _tile(seq_len, CONV_TILE))
    h, up, eid, wts = _mix(x, o_f, o_b, og, oc, w["g_head"], w["woa"], w["wob"], w["g_ffn"], w["wr"], w["br"],
                            _tile(n, 1024))

    n_tiles = n // tm
    max_slots = n * TOP_K + n_tiles * N_EXPERTS * (RUN_ALIGN - 1)
    nb = -(-max_slots // MOE_BLOCK) + N_EXPERTS
    nbp = -(-nb // LANES) * LANES
    rmax = -(-(tm * TOP_K + N_EXPERTS * (RUN_ALIGN - 1)) // SORT_ROWS) * SORT_ROWS
    pos, tbl, blkinfo, tail = _rank(eid, MOE_BLOCK, nbp, tm)
    pos = pos[1]
    tbl = tbl[1].reshape(n_tiles * SUBLANES, LANES)
    xs = _dispatch(tbl, tail, pos, up, nb * MOE_BLOCK, tm, rmax)
    ybuf = _experts(blkinfo[0], blkinfo[1, :1], xs, w["wg"], w["wu"], w["wd"], MOE_BLOCK)
    y = _combine(tbl, pos, wts, h, w["g_final"], ybuf, tm, rmax)
    return y.reshape(x3.shape)


def kernel(x_prompt, x_sample, g_mix, w_in, w_af2, b_af, w_ab2, b_ab, g_head, w_dw, b_dw, g_ln, b_ln, w_pw2,
           w_out, g_ffn, w_rg, b_rg, w_re, b_re, w_e_gate, w_e_up, w_e_down, g_final):
    w = _prep_weights(g_mix, w_in, w_af2, b_af, w_ab2, b_ab, g_head, w_dw, b_dw, g_ln, b_ln, w_pw2, w_out,
                      g_ffn, w_rg, b_rg, w_re, b_re, w_e_gate, w_e_up, w_e_down, g_final)
    return (_trunk(x_prompt, w), _trunk(x_sample, w))
```
